```python
import jax, jax.numpy as jnp
from jax import lax
import numpy as np

D_MODEL = 1024
BATCH = 1
SEQ = 16384
DEPTH = 1
DEC_BATCH = 128
DEC_SEQ = 4
PAST_LEN = 16384
PAGE_SIZE = 128

RET_HEADS = 4
RET_DK = 128
RET_DV = 256
RET_CHUNK = 128
ROPE_BASE = 10000.0
SWA_HEADS = 16
SWA_KV_HEADS = 4
SWA_DH = 64
WINDOW = 128
SWA_BLOCK = 128
N_MEM = 256
MEM_HEADS = 4
MEM_DH = 256
N_BRANCH = 3
BR_WIDTH = RET_HEADS * RET_DV
N_EXPERTS = 256
TOP_K = 8
N_GROUPS = 8
TOPK_GROUPS = 4
D_EXPERT = 256
D_SHARED = 256
ROUTED_SCALE = 2.5
MOE_BLOCK = 128
DN_ALPHA = (2.0 * DEPTH) ** 0.25
DN_BETA = (8.0 * DEPTH) ** -0.25
LN_EPS = 1e-5
NORM_EPS = 1e-6
NEG = -1e30
IN_SIZES = (RET_HEADS * RET_DK, RET_HEADS * RET_DK, RET_HEADS * RET_DV, RET_HEADS * RET_DV,
            SWA_HEADS * SWA_DH, SWA_KV_HEADS * SWA_DH, SWA_KV_HEADS * SWA_DH,
            MEM_HEADS * MEM_DH, N_BRANCH * D_MODEL)
D_IN = sum(IN_SIZES)

kernel_name = 'hybrid_retention_swa_memory_moe_step'


def split_in(h):
    points = [int(p) for p in np.cumsum(IN_SIZES)[:-1]]
    return jnp.split(h, points, axis=-1)


def layer_norm(x, g, b):
    xf = x.astype(jnp.float32)
    xc = xf - jnp.mean(xf, -1, keepdims=True)
    var = jnp.mean(xc * xc, -1, keepdims=True)
    return (xc * lax.rsqrt(var + LN_EPS) * g.astype(jnp.float32) + b.astype(jnp.float32)).astype(x.dtype)


def head_rms_norm(o):
    return o * lax.rsqrt(jnp.mean(o * o, -1, keepdims=True) + NORM_EPS)


def rotate(x, pos):
    half = RET_DK // 2
    inv_freq = 1.0 / (ROPE_BASE ** (jnp.arange(half, dtype=jnp.float32) / half))
    ang = pos.astype(jnp.float32)[:, None] * inv_freq[None, :]
    cos = jnp.cos(ang)[None, :, None, :]
    sin = jnp.sin(ang)[None, :, None, :]
    xf = x.astype(jnp.float32)
    x1, x2 = xf[..., :half], xf[..., half:]
    return jnp.concatenate([x1 * cos - x2 * sin, x2 * cos + x1 * sin], -1)


def retention(q, k, v, state0):
    B, L, H, _ = q.shape
    DV = v.shape[-1]
    C = RET_CHUNK if L % RET_CHUNK == 0 else L
    n = L // C
    lg = jnp.log1p(-jnp.exp2(-5.0 - jnp.arange(H, dtype=jnp.float32)))
    idx = jnp.arange(C, dtype=jnp.float32)
    rel = idx[:, None] - idx[None, :]
    intra = jnp.where(rel >= 0, jnp.exp(lg[:, None, None] * jnp.maximum(rel, 0.0)), 0.0)
    q_dec = jnp.exp((idx[:, None] + 1.0) * lg[None, :])[None, :, :, None]
    k_dec = jnp.exp((C - 1.0 - idx)[:, None] * lg[None, :])[None, :, :, None]
    c_dec = jnp.exp(C * lg)[None, :, None, None]

    def to_chunks(a):
        return jnp.moveaxis(a.astype(jnp.float32).reshape(B, n, C, H, a.shape[-1]), 1, 0)

    def step(S, inp):
        qc, kc, vc = inp
        a = jnp.einsum('bihd,bjhd->bhij', qc, kc) * intra
        o = jnp.einsum('bhij,bjhe->bihe', a, vc) + jnp.einsum('bihd,bhde->bihe', qc * q_dec, S)
        S = S * c_dec + jnp.einsum('bjhd,bjhe->bhde', kc * k_dec, vc)
        return S, o

    S, o = lax.scan(step, state0.astype(jnp.float32), (to_chunks(q), to_chunks(k), to_chunks(v)))
    return jnp.moveaxis(o, 0, 1).reshape(B, L, H, DV), S


def banded_attend(q, k, v, q_pos, k_pos, sinks):
    B, N, Q, Hq, dh = q.shape
    G = Hq // SWA_KV_HEADS
    qg = q.reshape(B, N, Q, SWA_KV_HEADS, G, dh)
    s = jnp.einsum('bnqhgd,bnkhd->bnhgqk', qg, k, preferred_element_type=jnp.float32) * (dh ** -0.5)
    rel = q_pos[:, :, None] - k_pos[:, None, :]
    valid = (rel >= 0) & (rel <= WINDOW) & (k_pos[:, None, :] >= 0)
    s = jnp.where(valid[None, :, None, None], s, NEG)
    sink = sinks.astype(jnp.float32).reshape(SWA_KV_HEADS, G)[None, None, :, :, None, None]
    m = jnp.maximum(jnp.max(s, -1, keepdims=True), sink)
    p = jnp.exp(s - m)
    p = p / (jnp.sum(p, -1, keepdims=True) + jnp.exp(sink - m))
    o = jnp.einsum('bnhgqk,bnkhd->bnqhgd', p, v.astype(jnp.float32))
    return o.reshape(B, N, Q, Hq, dh).astype(q.dtype)


def swa_prompt(q, k, v, sinks):
    B, S = q.shape[:2]
    nb = S // SWA_BLOCK
    pad = jnp.zeros_like(k[:, :SWA_BLOCK])
    kp = jnp.concatenate([pad, k], 1).reshape(B, nb + 1, SWA_BLOCK, SWA_KV_HEADS, SWA_DH)
    vp = jnp.concatenate([pad, v], 1).reshape(B, nb + 1, SWA_BLOCK, SWA_KV_HEADS, SWA_DH)
    kb = jnp.concatenate([kp[:, :-1], kp[:, 1:]], 2)
    vb = jnp.concatenate([vp[:, :-1], vp[:, 1:]], 2)
    qb = q.reshape(B, nb, SWA_BLOCK, SWA_HEADS, SWA_DH)
    q_pos = jnp.arange(S).reshape(nb, SWA_BLOCK)
    kpos = jnp.arange(-SWA_BLOCK, S).reshape(nb + 1, SWA_BLOCK)
    k_pos = jnp.concatenate([kpos[:-1], kpos[1:]], 1)
    return banded_attend(qb, kb, vb, q_pos, k_pos, sinks).reshape(B, S, SWA_HEADS, SWA_DH)


def swa_sample(q, k, v, prev_k, prev_v, sinks):
    W = prev_k.shape[1]
    L = q.shape[1]
    kk = jnp.concatenate([prev_k.astype(k.dtype), k], 1)
    vv = jnp.concatenate([prev_v.astype(v.dtype), v], 1)
    q_pos = (W + jnp.arange(L))[None]
    k_pos = jnp.arange(W + L)[None]
    o = banded_attend(q[:, None], kk[:, None], vv[:, None], q_pos, k_pos, sinks)[:, 0]
    return o, kk[:, -W:], vv[:, -W:]


def memory_kv(mem, w_mem_kv):
    B, M, _ = mem.shape
    mk, mv = jnp.split(mem @ w_mem_kv, 2, axis=-1)
    return mk.reshape(B, M, MEM_HEADS, MEM_DH), mv.reshape(B, M, MEM_HEADS, MEM_DH)


def memory_attend(q, mk, mv):
    s = jnp.einsum('blhd,bmhd->bhlm', q, mk.astype(q.dtype), preferred_element_type=jnp.float32) * (MEM_DH ** -0.5)
    p = jax.nn.softmax(s, axis=-1)
    return jnp.einsum('bhlm,bmhd->blhd', p, mv.astype(jnp.float32)).astype(q.dtype)


def token_mixing(x, pos, ret_state, swa_prev_k, swa_prev_v, mem_k, mem_v, w_in, swa_sinks, w_branch, w_o):
    B, L, D = x.shape
    rq, rk, rv, rg, sq, sk, sv, mq, gl = split_in(x @ w_in)
    q = rotate(rq.reshape(B, L, RET_HEADS, RET_DK), pos)
    k = rotate(rk.reshape(B, L, RET_HEADS, RET_DK), pos) * (RET_DK ** -0.5)
    ro, ret_new = retention(q, k, rv.reshape(B, L, RET_HEADS, RET_DV), ret_state)
    ro = head_rms_norm(ro) * jax.nn.silu(rg.reshape(B, L, RET_HEADS, RET_DV).astype(jnp.float32))
    sq = sq.reshape(B, L, SWA_HEADS, SWA_DH)
    sk = sk.reshape(B, L, SWA_KV_HEADS, SWA_DH)
    sv = sv.reshape(B, L, SWA_KV_HEADS, SWA_DH)
    if swa_prev_k is None:
        so = swa_prompt(sq, sk, sv, swa_sinks)
        new_k, new_v = sk[:, -WINDOW:], sv[:, -WINDOW:]
    else:
        so, new_k, new_v = swa_sample(sq, sk, sv, swa_prev_k, swa_prev_v, swa_sinks)
    mo = memory_attend(mq.reshape(B, L, MEM_HEADS, MEM_DH), mem_k, mem_v)
    br = jnp.stack([ro.reshape(B, L, BR_WIDTH).astype(x.dtype), so.reshape(B, L, BR_WIDTH),
                    mo.reshape(B, L, BR_WIDTH)], axis=2)
    proj = jnp.einsum('blnc,ncd->blnd', br, w_branch)
    gates = jax.nn.sigmoid(gl.reshape(B, L, N_BRANCH, D).astype(jnp.float32))
    merged = jnp.sum(gates * proj.astype(jnp.float32), axis=2).astype(x.dtype)
    return merged @ w_o, ret_new, new_k, new_v


def route(x2, w_router, router_bias):
    T = x2.shape[0]
    s = jax.nn.sigmoid(jnp.matmul(x2, w_router, preferred_element_type=jnp.float32))
    sb = s + router_bias.astype(jnp.float32)
    gsc = jnp.sum(lax.top_k(sb.reshape(T, N_GROUPS, N_EXPERTS // N_GROUPS), 2)[0], -1)
    _, gidx = lax.top_k(gsc, TOPK_GROUPS)
    gmask = jnp.any(gidx[:, :, None] == jnp.arange(N_GROUPS)[None, None, :], axis=1)
    emask = jnp.repeat(gmask, N_EXPERTS // N_GROUPS, axis=1)
    _, eidx = lax.top_k(jnp.where(emask, sb, -jnp.inf), TOP_K)
    w = jnp.take_along_axis(s, eidx, axis=1)
    w = w / jnp.sum(w, -1, keepdims=True) * ROUTED_SCALE
    return eidx, w


def routed_experts(x2, eidx, ew, w_gate_e, w_up_e, w_down_e):
    T, D = x2.shape
    A = T * TOP_K
    flat_e = eidx.reshape(-1)
    flat_t = jnp.repeat(jnp.arange(T, dtype=jnp.int32), TOP_K)
    order = jnp.argsort(flat_e)
    se, st, sw = flat_e[order], flat_t[order], ew.reshape(-1)[order]
    counts = jnp.bincount(flat_e, length=N_EXPERTS)
    padded = (counts + MOE_BLOCK - 1) // MOE_BLOCK * MOE_BLOCK
    pad_end = jnp.cumsum(padded)
    pad_start = pad_end - padded
    grp_start = jnp.cumsum(counts) - counts
    dest = pad_start[se] + (jnp.arange(A) - grp_start[se])
    nblk = -(-A // MOE_BLOCK) + N_EXPERTS
    P = nblk * MOE_BLOCK
    row_tok = jnp.full((P,), T, jnp.int32).at[dest].set(st)
    row_w = jnp.zeros((P,), jnp.float32).at[dest].set(sw)
    blk_exp = jnp.minimum(jnp.searchsorted(pad_end, jnp.arange(nblk) * MOE_BLOCK, side='right'), N_EXPERTS - 1)
    xp = jnp.concatenate([x2, jnp.zeros((1, D), x2.dtype)], 0)
    xb = xp[row_tok].reshape(nblk, MOE_BLOCK, D)

    def expert_block(args):
        xi, e = args
        return (jax.nn.silu(xi @ w_gate_e[e]) * (xi @ w_up_e[e])) @ w_down_e[e]

    yb = lax.map(expert_block, (xb, blk_exp)).reshape(P, D)
    y = jax.ops.segment_sum(yb.astype(jnp.float32) * row_w[:, None], row_tok, num_segments=T + 1)
    return y[:T]


def moe_ffn(x, w_router, router_bias, w_gate_e, w_up_e, w_down_e, w_sh_gate, w_sh_up, w_sh_down):
    B, L, D = x.shape
    x2 = x.reshape(B * L, D)
    eidx, ew = route(x2, w_router, router_bias)
    shared = (jax.nn.silu(x2 @ w_sh_gate) * (x2 @ w_sh_up)) @ w_sh_down
    y = routed_experts(x2, eidx, ew, w_gate_e, w_up_e, w_down_e) + shared.astype(jnp.float32)
    return y.reshape(B, L, D).astype(x.dtype)


def decoder_layer(x, pos, ret_state, swa_k, swa_v, mem_k, mem_v, w_in, swa_sinks, w_branch, w_o,
                  ln1_g, ln1_b, w_router, router_bias, w_gate_e, w_up_e, w_down_e,
                  w_sh_gate, w_sh_up, w_sh_down, ln2_g, ln2_b):
    a, ret_new, new_k, new_v = token_mixing(x, pos, ret_state, swa_k, swa_v, mem_k, mem_v,
                                            w_in, swa_sinks, w_branch, w_o)
    h = layer_norm(DN_ALPHA * x + a, ln1_g, ln1_b)
    f = moe_ffn(h, w_router, router_bias, w_gate_e, w_up_e, w_down_e, w_sh_gate, w_sh_up, w_sh_down)
    y = layer_norm(DN_ALPHA * h + f, ln2_g, ln2_b)
    return y, ret_new, new_k, new_v


def setup_inputs(seed: int = 0) -> dict:
    key = jax.random.key(seed)
    ks = jax.random.split(key, 26)
    f32 = jnp.float32

    def nrm(k, shape, scale):
        return jax.random.normal(k, shape, f32) * scale

    w_buf = min(WINDOW, PAST_LEN)
    col_scales = (1.0, 1.0, DN_BETA, 1.0, 1.0, 1.0, DN_BETA, 1.0, 1.0)
    col_scale = jnp.concatenate([jnp.full((n,), s, f32) for n, s in zip(IN_SIZES, col_scales)])
    kv_scale = jnp.concatenate([jnp.ones((MEM_HEADS * MEM_DH,), f32), jnp.full((MEM_HEADS * MEM_DH,), DN_BETA, f32)])
    return {
        'x_prompt': nrm(ks[0], (BATCH, SEQ, D_MODEL), 1.0),
        'x_sample': nrm(ks[1], (DEC_BATCH, DEC_SEQ, D_MODEL), 1.0),
        'mem_prompt': nrm(ks[2], (BATCH, N_MEM, D_MODEL), 1.0),
        'cache_ret_state': nrm(ks[3], (DEPTH, DEC_BATCH, RET_HEADS, RET_DK, RET_DV), 0.3),
        'cache_swa_k': nrm(ks[4], (DEPTH, DEC_BATCH, w_buf, SWA_KV_HEADS, SWA_DH), 1.0),
        'cache_swa_v': nrm(ks[5], (DEPTH, DEC_BATCH, w_buf, SWA_KV_HEADS, SWA_DH), DN_BETA),
        'cache_mem_k': nrm(ks[6], (DEPTH, DEC_BATCH, N_MEM, MEM_HEADS, MEM_DH), 1.0),
        'cache_mem_v': nrm(ks[7], (DEPTH, DEC_BATCH, N_MEM, MEM_HEADS, MEM_DH), DN_BETA),
        'w_in': nrm(ks[8], (DEPTH, D_MODEL, D_IN), D_MODEL ** -0.5) * col_scale,
        'swa_sinks': nrm(ks[9], (DEPTH, SWA_HEADS), 0.5),
        'w_mem_kv': nrm(ks[10], (DEPTH, D_MODEL, 2 * MEM_HEADS * MEM_DH), D_MODEL ** -0.5) * kv_scale,
        'w_branch': nrm(ks[11], (DEPTH, N_BRANCH, BR_WIDTH, D_MODEL), BR_WIDTH ** -0.5 * DN_BETA),
        'w_o': nrm(ks[12], (DEPTH, D_MODEL, D_MODEL), D_MODEL ** -0.5 * DN_BETA),
        'ln1_g': 1.0 + nrm(ks[13], (DEPTH, D_MODEL), 0.02),
        'ln1_b': nrm(ks[14], (DEPTH, D_MODEL), 0.02),
        'w_router': nrm(ks[15], (DEPTH, D_MODEL, N_EXPERTS), D_MODEL ** -0.5),
        'router_bias': nrm(ks[16], (DEPTH, N_EXPERTS), 0.01),
        'w_gate_e': nrm(ks[17], (DEPTH, N_EXPERTS, D_MODEL, D_EXPERT), D_MODEL ** -0.5),
        'w_up_e': nrm(ks[18], (DEPTH, N_EXPERTS, D_MODEL, D_EXPERT), D_MODEL ** -0.5),
        'w_down_e': nrm(ks[19], (DEPTH, N_EXPERTS, D_EXPERT, D_MODEL), D_EXPERT ** -0.5 * DN_BETA),
        'w_sh_gate': nrm(ks[20], (DEPTH, D_MODEL, D_SHARED), D_MODEL ** -0.5),
        'w_sh_up': nrm(ks[21], (DEPTH, D_MODEL, D_SHARED), D_MODEL ** -0.5),
        'w_sh_down': nrm(ks[22], (DEPTH, D_SHARED, D_MODEL), D_SHARED ** -0.5 * DN_BETA),
        'ln2_g': 1.0 + nrm(ks[23], (DEPTH, D_MODEL), 0.02),
        'ln2_b': nrm(ks[24], (DEPTH, D_MODEL), 0.02),
    }


def reference(x_prompt, x_sample, mem_prompt, cache_ret_state, cache_swa_k, cache_swa_v, cache_mem_k, cache_mem_v,
              w_in, swa_sinks, w_mem_kv, w_branch, w_o, ln1_g, ln1_b, w_router, router_bias,
              w_gate_e, w_up_e, w_down_e, w_sh_gate, w_sh_up, w_sh_down, ln2_g, ln2_b):
    pos_p = jnp.arange(x_prompt.shape[1])
    pos_s = PAST_LEN + jnp.arange(x_sample.shape[1])
    hp, hs = x_prompt, x_sample
    rs_p_l, k_p_l, v_p_l, mk_p_l, mv_p_l, rs_s_l, k_s_l, v_s_l = [], [], [], [], [], [], [], []
    for l in range(DEPTH):
        lw = (w_in[l], swa_sinks[l], w_branch[l], w_o[l], ln1_g[l], ln1_b[l], w_router[l], router_bias[l],
              w_gate_e[l], w_up_e[l], w_down_e[l], w_sh_gate[l], w_sh_up[l], w_sh_down[l], ln2_g[l], ln2_b[l])
        mk_p, mv_p = memory_kv(mem_prompt, w_mem_kv[l])
        rs0 = jnp.zeros((hp.shape[0], RET_HEADS, RET_DK, RET_DV), jnp.float32)
        hp, rs_p, k_p, v_p = decoder_layer(hp, pos_p, rs0, None, None, mk_p, mv_p, *lw)
        hs, rs_s, k_s, v_s = decoder_layer(hs, pos_s, cache_ret_state[l], cache_swa_k[l], cache_swa_v[l],
                                           cache_mem_k[l], cache_mem_v[l], *lw)
        rs_p_l.append(rs_p); k_p_l.append(k_p); v_p_l.append(v_p); mk_p_l.append(mk_p); mv_p_l.append(mv_p)
        rs_s_l.append(rs_s); k_s_l.append(k_s); v_s_l.append(v_s)
    ret_state_prompt = jnp.stack(rs_p_l, 0)
    swa_k_prompt = jnp.stack(k_p_l, 0)
    swa_v_prompt = jnp.stack(v_p_l, 0)
    mem_k_prompt = jnp.stack(mk_p_l, 0)
    mem_v_prompt = jnp.stack(mv_p_l, 0)
    ret_state_sample = jnp.stack(rs_s_l, 0)
    swa_k_sample = jnp.stack(k_s_l, 0)
    swa_v_sample = jnp.stack(v_s_l, 0)
    return (hp, hs, ret_state_prompt, swa_k_prompt, swa_v_prompt, mem_k_prompt, mem_v_prompt,
            ret_state_sample, swa_k_sample, swa_v_sample)
```

```python
import functools

import jax
import jax.numpy as jnp
from jax import lax
from jax.experimental import pallas as pl
from jax.experimental.pallas import tpu as pltpu

BF = jnp.bfloat16
F32 = jnp.float32

D_MODEL = 1024
RET_HEADS = 4
RET_DK = 128
RET_DV = 256
RET_CHUNK = 128
ROPE_BASE = 10000.0
SWA_HEADS = 16
SWA_KV_HEADS = 4
SWA_GROUP = SWA_HEADS // SWA_KV_HEADS
SWA_DH = 64
WINDOW = 128
SWA_BLOCK = 128
N_MEM = 256
MEM_HEADS = 4
MEM_DH = 256
N_BRANCH = 3
N_EXPERTS = 256
TOP_K = 8
N_GROUPS = 8
GROUP_SIZE = N_EXPERTS // N_GROUPS
TOPK_GROUPS = 4
D_EXPERT = 256
ROUTED_SCALE = 2.5
MOE_BLOCK = 128
LN_EPS = 1e-5
NORM_EPS = 1e-6
NEG = -1e30
PAST_LEN = 16384
SAMPLE_PAD = 8

COL_RQ, COL_RK, COL_RV, COL_RG, COL_SQ, COL_MQ, COL_GL, COL_SK, COL_SV = (
    0, 512, 1024, 2048, 3072, 4096, 5120, 8192, 8448)
D_IN = 8704

VMEM_LIMIT = 56 * 1024 * 1024


def _params(*sem):
    return pltpu.CompilerParams(dimension_semantics=sem, vmem_limit_bytes=VMEM_LIMIT)


def _bdot(a, b):
    return jnp.dot(a.astype(BF), b.astype(BF), preferred_element_type=F32)


def _bdot_nt(a, b):
    return lax.dot_general(a.astype(BF), b.astype(BF), (((1,), (1,)), ((), ())),
                           preferred_element_type=F32)


def _bdot_tn(a, b):
    return lax.dot_general(a.astype(BF), b.astype(BF), (((0,), (0,)), ((), ())),
                           preferred_element_type=F32)


def _layer_norm(z, g, b):
    zc = z - jnp.mean(z, -1, keepdims=True)
    var = jnp.mean(zc * zc, -1, keepdims=True)
    return zc * lax.rsqrt(var + LN_EPS) * g + b


def _proj_kernel(x_ref, w_ref, o_ref, xb_ref):
    @pl.when(pl.program_id(1) == 0)
    def _():
        xb_ref[...] = x_ref[...].astype(BF)

    o_ref[...] = jnp.dot(xb_ref[...], w_ref[...], preferred_element_type=F32)


def _proj(x, w_bf, tm, tn):
    m, k = x.shape
    n = w_bf.shape[1]
    return pl.pallas_call(
        _proj_kernel,
        grid=(m // tm, n // tn),
        in_specs=[pl.BlockSpec((tm, k), lambda i, j: (i, 0)),
                  pl.BlockSpec((k, tn), lambda i, j: (0, j))],
        out_specs=pl.BlockSpec((tm, tn), lambda i, j: (i, j)),
        out_shape=jax.ShapeDtypeStruct((m, n), F32),
        scratch_shapes=[pltpu.VMEM((tm, k), BF)],
        compiler_params=_params("parallel", "arbitrary"),
        name="proj",
    )(x, w_bf)


def _ret_tables(c_real, c_pad):
    lg = jnp.log1p(-jnp.exp2(-5.0 - jnp.arange(RET_HEADS, dtype=F32)))
    idx = jnp.arange(c_pad, dtype=F32)
    real = idx < c_real
    rel = idx[:, None] - idx[None, :]
    intra = jnp.where(rel >= 0, jnp.exp(lg[:, None, None] * jnp.maximum(rel, 0.0)), 0.0)
    intra = jnp.where(real[None, :, None] & real[None, None, :], intra, 0.0)
    q_dec = jnp.where(real[None, :], jnp.exp((idx[None, :] + 1.0) * lg[:, None]), 0.0)
    k_dec = jnp.where(real[None, :], jnp.exp((c_real - 1.0 - idx)[None, :] * lg[:, None]), 0.0)
    c_dec = jnp.exp(c_real * lg)
    bc = lambda t: jnp.broadcast_to(t[:, :, None], (RET_HEADS, c_pad, RET_DK))
    return intra, bc(q_dec), bc(k_dec), c_dec


def _rope_tables(pos):
    half = RET_DK // 2
    inv_freq = 1.0 / (ROPE_BASE ** (jnp.arange(half, dtype=F32) / half))
    ang = pos.astype(F32)[:, None] * inv_freq[None, :]
    cos, sin = jnp.cos(ang), jnp.sin(ang)
    return jnp.concatenate([cos, cos], -1), jnp.concatenate([-sin, sin], -1)


def _ret_kernel(cdec_ref, rq_ref, rk_ref, rv_ref, rg_ref, cos_ref, sin_ref, intra_ref, qdec_ref,
                kdec_ref, s0_ref, o_ref, s_out_ref, s_scr, *, n_chunks):
    c = pl.program_id(1)

    @pl.when(c == 0)
    def _():
        s_scr[...] = s0_ref[0]

    cos2 = cos_ref[...]
    sin2 = sin_ref[...]

    def rot(x):
        return x * cos2 + pltpu.roll(x, RET_DK // 2, 1) * sin2

    for h in range(RET_HEADS):
        q = rot(rq_ref[0, :, h * RET_DK:(h + 1) * RET_DK])
        k = rot(rk_ref[0, :, h * RET_DK:(h + 1) * RET_DK]) * (RET_DK ** -0.5)
        v = rv_ref[0, :, h * RET_DV:(h + 1) * RET_DV].astype(BF)
        s_old = s_scr[h]
        a = _bdot_nt(q, k) * intra_ref[h]
        o = _bdot(a, v) + _bdot(q * qdec_ref[h], s_old)
        s_scr[h] = s_old * cdec_ref[h] + _bdot_tn(k * kdec_ref[h], v)
        o = o * lax.rsqrt(jnp.mean(o * o, -1, keepdims=True) + NORM_EPS)
        g = rg_ref[0, :, h * RET_DV:(h + 1) * RET_DV]
        o_ref[0, :, h * RET_DV:(h + 1) * RET_DV] = o * (g * jax.nn.sigmoid(g))

    @pl.when(c == n_chunks - 1)
    def _():
        s_out_ref[0] = s_scr[...]


def _retention(h3, pos, state0, c_real, c_pad):
    b, l, _ = h3.shape
    n_chunks = l // c_pad
    intra, qdec, kdec, cdec = _ret_tables(c_real, c_pad)
    cos2, sin2 = _rope_tables(pos)
    full3 = lambda shape: pl.BlockSpec(shape, lambda i, c: (0, 0, 0))
    return pl.pallas_call(
        functools.partial(_ret_kernel, n_chunks=n_chunks),
        grid=(b, n_chunks),
        in_specs=[
            pl.BlockSpec(memory_space=pltpu.SMEM),
            pl.BlockSpec((1, c_pad, 512), lambda i, c: (i, c, COL_RQ // 512)),
            pl.BlockSpec((1, c_pad, 512), lambda i, c: (i, c, COL_RK // 512)),
            pl.BlockSpec((1, c_pad, 1024), lambda i, c: (i, c, COL_RV // 1024)),
            pl.BlockSpec((1, c_pad, 1024), lambda i, c: (i, c, COL_RG // 1024)),
            pl.BlockSpec((c_pad, RET_DK), lambda i, c: (c, 0)),
            pl.BlockSpec((c_pad, RET_DK), lambda i, c: (c, 0)),
            full3((RET_HEADS, c_pad, c_pad)),
            full3((RET_HEADS, c_pad, RET_DK)),
            full3((RET_HEADS, c_pad, RET_DK)),
            pl.BlockSpec((1, RET_HEADS, RET_DK, RET_DV), lambda i, c: (i, 0, 0, 0)),
        ],
        out_specs=[
            pl.BlockSpec((1, c_pad, 1024), lambda i, c: (i, c, 0)),
            pl.BlockSpec((1, RET_HEADS, RET_DK, RET_DV), lambda i, c: (i, 0, 0, 0)),
        ],
        out_shape=[jax.ShapeDtypeStruct((b, l, 1024), F32),
                   jax.ShapeDtypeStruct((b, RET_HEADS, RET_DK, RET_DV), F32)],
        scratch_shapes=[pltpu.VMEM((RET_HEADS, RET_DK, RET_DV), F32)],
        compiler_params=_params("parallel", "arbitrary"),
        name="retention",
    )(cdec, h3, h3, h3, h3, cos2, sin2, intra, qdec, kdec, state0)


def _sink_softmax(s, sink):
    m = jnp.maximum(jnp.max(s, -1, keepdims=True), sink)
    p = jnp.exp(s - m)
    return p / (jnp.sum(p, -1, keepdims=True) + jnp.exp(sink - m))


def _swa_prompt_kernel(sinks_ref, q_ref, kp_ref, kc_ref, vp_ref, vc_ref, o_ref):
    n = pl.program_id(0)
    qi = lax.broadcasted_iota(jnp.int32, (SWA_BLOCK, 2 * SWA_BLOCK), 0)
    kj = lax.broadcasted_iota(jnp.int32, (SWA_BLOCK, 2 * SWA_BLOCK), 1)
    rel = SWA_BLOCK + qi - kj
    valid = (rel >= 0) & (rel <= WINDOW) & ((kj >= SWA_BLOCK) | (n > 0))
    for h in range(SWA_KV_HEADS):
        sl = slice(h * SWA_DH, (h + 1) * SWA_DH)
        k2 = jnp.concatenate([kp_ref[:, sl], kc_ref[:, sl]], 0).astype(BF)
        v2 = jnp.concatenate([vp_ref[:, sl], vc_ref[:, sl]], 0).astype(BF)
        for g in range(SWA_GROUP):
            hq = h * SWA_GROUP + g
            qsl = slice(hq * SWA_DH, (hq + 1) * SWA_DH)
            s = _bdot_nt(q_ref[:, qsl], k2) * (SWA_DH ** -0.5)
            s = jnp.where(valid, s, NEG)
            p = _sink_softmax(s, sinks_ref[hq])
            o_ref[:, qsl] = _bdot(p, v2)


def _swa_prompt(h2, sinks):
    t = h2.shape[0]
    nb = t // SWA_BLOCK
    prev = lambda col: (lambda n: (jnp.maximum(n - 1, 0), col))
    cur = lambda col: (lambda n: (n, col))
    ck, cv = COL_SK // 256, COL_SV // 256
    return pl.pallas_call(
        _swa_prompt_kernel,
        grid=(nb,),
        in_specs=[
            pl.BlockSpec(memory_space=pltpu.SMEM),
            pl.BlockSpec((SWA_BLOCK, 1024), cur(COL_SQ // 1024)),
            pl.BlockSpec((SWA_BLOCK, 256), prev(ck)),
            pl.BlockSpec((SWA_BLOCK, 256), cur(ck)),
            pl.BlockSpec((SWA_BLOCK, 256), prev(cv)),
            pl.BlockSpec((SWA_BLOCK, 256), cur(cv)),
        ],
        out_specs=pl.BlockSpec((SWA_BLOCK, 1024), lambda n: (n, 0)),
        out_shape=jax.ShapeDtypeStruct((t, 1024), F32),
        compiler_params=_params("parallel"),
        name="swa_prompt",
    )(sinks, h2, h2, h2, h2, h2)


def _swa_sample_kernel(sinks_ref, q_ref, kn_ref, vn_ref, kp_ref, vp_ref, o_ref, *, n_new):
    tb = q_ref.shape[0]
    w = kp_ref.shape[1]
    p_ = SAMPLE_PAD
    qi = lax.broadcasted_iota(jnp.int32, (tb, p_, w), 1)
    kj = lax.broadcasted_iota(jnp.int32, (tb, p_, w), 2)
    rel_prev = w + qi - kj
    valid_prev = (rel_prev >= 0) & (rel_prev <= WINDOW)
    qn = lax.broadcasted_iota(jnp.int32, (tb, p_, p_), 1)
    kn = lax.broadcasted_iota(jnp.int32, (tb, p_, p_), 2)
    valid_new = (qn - kn >= 0) & (qn - kn <= WINDOW) & (kn < n_new)
    bdot = lambda eq, a, b: jnp.einsum(eq, a.astype(BF), b.astype(BF), preferred_element_type=F32)
    for h in range(SWA_KV_HEADS):
        sl = slice(h * SWA_DH, (h + 1) * SWA_DH)
        kp, vp = kp_ref[:, :, sl], vp_ref[:, :, sl]
        kn_h, vn_h = kn_ref[:, :, sl], vn_ref[:, :, sl]
        for g in range(SWA_GROUP):
            hq = h * SWA_GROUP + g
            qsl = slice(hq * SWA_DH, (hq + 1) * SWA_DH)
            q = q_ref[:, :, qsl]
            sp = bdot('bqd,bkd->bqk', q, kp) * (SWA_DH ** -0.5)
            sn = bdot('bqd,bkd->bqk', q, kn_h) * (SWA_DH ** -0.5)
            sp = jnp.where(valid_prev, sp, NEG)
            sn = jnp.where(valid_new, sn, NEG)
            sink = sinks_ref[hq]
            m = jnp.maximum(jnp.maximum(jnp.max(sp, -1, keepdims=True),
                                        jnp.max(sn, -1, keepdims=True)), sink)
            pp = jnp.exp(sp - m)
            pn = jnp.exp(sn - m)
            den = jnp.sum(pp, -1, keepdims=True) + jnp.sum(pn, -1, keepdims=True) + jnp.exp(sink - m)
            o = bdot('bqk,bkd->bqd', pp / den, vp) + bdot('bqk,bkd->bqd', pn / den, vn_h)
            o_ref[:, :, qsl] = o


def _swa_sample(h3, prev_k, prev_v, sinks, n_new, tb=8):
    b = h3.shape[0]
    w = prev_k.shape[1]
    return pl.pallas_call(
        functools.partial(_swa_sample_kernel, n_new=n_new),
        grid=(b // tb,),
        in_specs=[
            pl.BlockSpec(memory_space=pltpu.SMEM),
            pl.BlockSpec((tb, SAMPLE_PAD, 1024), lambda i: (i, 0, COL_SQ // 1024)),
            pl.BlockSpec((tb, SAMPLE_PAD, 256), lambda i: (i, 0, COL_SK // 256)),
            pl.BlockSpec((tb, SAMPLE_PAD, 256), lambda i: (i, 0, COL_SV // 256)),
            pl.BlockSpec((tb, w, 256), lambda i: (i, 0, 0)),
            pl.BlockSpec((tb, w, 256), lambda i: (i, 0, 0)),
        ],
        out_specs=pl.BlockSpec((tb, SAMPLE_PAD, 1024), lambda i: (i, 0, 0)),
        out_shape=jax.ShapeDtypeStruct((b, SAMPLE_PAD, 1024), F32),
        compiler_params=_params("parallel"),
        name="swa_sample",
    )(sinks, h3, h3, h3, prev_k, prev_v)


def _mem_kernel(q_ref, mk_ref, mv_ref, o_ref):
    for h in range(MEM_HEADS):
        sl = slice(h * MEM_DH, (h + 1) * MEM_DH)
        s = _bdot_nt(q_ref[0, :, sl], mk_ref[0, :, sl]) * (MEM_DH ** -0.5)
        m = jnp.max(s, -1, keepdims=True)
        e = jnp.exp(s - m)
        p = e / jnp.sum(e, -1, keepdims=True)
        o_ref[0, :, sl] = _bdot(p, mv_ref[0, :, sl])


def _mem_attend(h3, mk, mv, tl):
    b, l, _ = h3.shape
    return pl.pallas_call(
        _mem_kernel,
        grid=(b, l // tl),
        in_specs=[
            pl.BlockSpec((1, tl, 1024), lambda i, j: (i, j, COL_MQ // 1024)),
            pl.BlockSpec((1, N_MEM, 1024), lambda i, j: (i, 0, 0)),
            pl.BlockSpec((1, N_MEM, 1024), lambda i, j: (i, 0, 0)),
        ],
        out_specs=pl.BlockSpec((1, tl, 1024), lambda i, j: (i, j, 0)),
        out_shape=jax.ShapeDtypeStruct((b, l, 1024), F32),
        compiler_params=_params("parallel", "parallel"),
        name="mem_attend",
    )(h3, mk, mv)


def _merge_kernel(ro_ref, so_ref, mo_ref, g0_ref, g1_ref, g2_ref, x_ref, wb_ref, wo_ref, g_ref,
                  b_ref, o_ref, *, alpha):
    acc = None
    for n, (br, gl) in enumerate(((ro_ref, g0_ref), (so_ref, g1_ref), (mo_ref, g2_ref))):
        term = jax.nn.sigmoid(gl[...]) * jnp.dot(br[...].astype(BF), wb_ref[n],
                                                 preferred_element_type=F32)
        acc = term if acc is None else acc + term
    a = jnp.dot(acc.astype(BF), wo_ref[...], preferred_element_type=F32)
    o_ref[...] = _layer_norm(alpha * x_ref[...] + a, g_ref[...], b_ref[...])


def _merge(ro, so, mo, h2, x2, wb_bf, wo_bf, g, b, alpha, tm):
    t = x2.shape[0]
    tile = lambda col: pl.BlockSpec((tm, 1024), lambda i: (i, col))
    gl0 = COL_GL // 1024
    return pl.pallas_call(
        functools.partial(_merge_kernel, alpha=alpha),
        grid=(t // tm,),
        in_specs=[tile(0), tile(0), tile(0), tile(gl0), tile(gl0 + 1), tile(gl0 + 2), tile(0),
                  pl.BlockSpec((N_BRANCH, 1024, 1024), lambda i: (0, 0, 0)),
                  pl.BlockSpec((1024, 1024), lambda i: (0, 0)),
                  pl.BlockSpec((1, 1024), lambda i: (0, 0)),
                  pl.BlockSpec((1, 1024), lambda i: (0, 0))],
        out_specs=tile(0),
        out_shape=jax.ShapeDtypeStruct((t, 1024), F32),
        compiler_params=_params("parallel"),
        name="merge_ln1",
    )(ro, so, mo, h2, h2, h2, x2, wb_bf, wo_bf, g, b)


def _first_index_of_max(v, iota, big, axes):
    m = jnp.max(v, axis=axes, keepdims=True)
    idx = jnp.min(jnp.where(v == m, iota, big), axis=axes, keepdims=True)
    return m, idx


def _router_kernel(x_ref, wr_ref, bias_ref, eidx_ref, ew_ref):
    tt = x_ref.shape[0]
    logits = lax.dot_general(wr_ref[...], x_ref[...].astype(BF), (((1,), (1,)), ((), ())),
                             preferred_element_type=F32)
    s = jax.nn.sigmoid(logits).reshape(N_GROUPS, GROUP_SIZE, tt)
    sb = s + bias_ref[...].reshape(N_GROUPS, GROUP_SIZE, 1)
    ninf = -jnp.inf
    r_iota = lax.broadcasted_iota(jnp.int32, sb.shape, 1)
    m1, i1 = _first_index_of_max(sb, r_iota, GROUP_SIZE, 1)
    m2 = jnp.max(jnp.where(r_iota == i1, ninf, sb), axis=1, keepdims=True)
    gsc = (m1 + m2).reshape(N_GROUPS, tt)
    g_iota = lax.broadcasted_iota(jnp.int32, gsc.shape, 0)
    gmask = jnp.zeros(gsc.shape, jnp.bool_)
    for _ in range(TOPK_GROUPS):
        _, gi = _first_index_of_max(gsc, g_iota, N_GROUPS, 0)
        hit = g_iota == gi
        gmask = gmask | hit
        gsc = jnp.where(hit, ninf, gsc)
    cand = jnp.where(gmask.reshape(N_GROUPS, 1, tt), sb, ninf)
    e_iota = lax.broadcasted_iota(jnp.int32, sb.shape, 0) * GROUP_SIZE + r_iota
    idxs, ws = [], []
    for _ in range(TOP_K):
        _, ei = _first_index_of_max(cand, e_iota, N_EXPERTS, (0, 1))
        hit = e_iota == ei
        idxs.append(ei.reshape(1, tt))
        ws.append(jnp.sum(jnp.where(hit, s, 0.0), axis=(0, 1)).reshape(1, tt))
        cand = jnp.where(hit, ninf, cand)
    w = jnp.concatenate(ws, 0)
    eidx_ref[...] = jnp.concatenate(idxs, 0)
    ew_ref[...] = w / jnp.sum(w, 0, keepdims=True) * ROUTED_SCALE


def _router(x2, wr_t_bf, bias_col, tt):
    t = x2.shape[0]
    return pl.pallas_call(
        _router_kernel,
        grid=(t // tt,),
        in_specs=[pl.BlockSpec((tt, 1024), lambda i: (i, 0)),
                  pl.BlockSpec((N_EXPERTS, 1024), lambda i: (0, 0)),
                  pl.BlockSpec((N_EXPERTS, 1), lambda i: (0, 0))],
        out_specs=[pl.BlockSpec((TOP_K, tt), lambda i: (0, i)),
                   pl.BlockSpec((TOP_K, tt), lambda i: (0, i))],
        out_shape=[jax.ShapeDtypeStruct((TOP_K, t), jnp.int32),
                   jax.ShapeDtypeStruct((TOP_K, t), F32)],
        compiler_params=_params("parallel"),
        name="router",
    )(x2, wr_t_bf, bias_col)


def _expert_kernel(be_ref, nused_ref, xb_ref, wg_ref, wu_ref, wd_ref, yb_ref):
    i = pl.program_id(0)

    @pl.when(i < nused_ref[0])
    def _():
        x = xb_ref[...].astype(BF)
        g = jnp.dot(x, wg_ref[0].astype(BF), preferred_element_type=F32)
        u = jnp.dot(x, wu_ref[0].astype(BF), preferred_element_type=F32)
        act = (g * jax.nn.sigmoid(g)) * u
        yb_ref[...] = jnp.dot(act.astype(BF), wd_ref[0].astype(BF), preferred_element_type=F32)

    @pl.when(i >= nused_ref[0])
    def _():
        yb_ref[...] = jnp.zeros_like(yb_ref)


def _experts(xb, blk_exp, nused, w_gate, w_up, w_down):
    p = xb.shape[0]
    nblk = p // MOE_BLOCK
    wmap = lambda i, be, nu: (be[i], 0, 0)
    return pl.pallas_call(
        _expert_kernel,
        grid_spec=pltpu.PrefetchScalarGridSpec(
            num_scalar_prefetch=2,
            grid=(nblk,),
            in_specs=[pl.BlockSpec((MOE_BLOCK, 1024), lambda i, be, nu: (i, 0)),
                      pl.BlockSpec((1, 1024, D_EXPERT), wmap),
                      pl.BlockSpec((1, 1024, D_EXPERT), wmap),
                      pl.BlockSpec((1, D_EXPERT, 1024), wmap)],
            out_specs=pl.BlockSpec((MOE_BLOCK, 1024), lambda i, be, nu: (i, 0)),
        ),
        out_shape=jax.ShapeDtypeStruct((p, 1024), F32),
        compiler_params=_params("arbitrary"),
        name="experts",
    )(blk_exp, nused, xb, w_gate, w_up, w_down)


def _dispatch(eidx, ew, t):
    a = t * TOP_K
    flat_e = eidx.reshape(-1)
    flat_t = jnp.repeat(jnp.arange(t, dtype=jnp.int32), TOP_K)
    order = jnp.argsort(flat_e)
    se, st, sw = flat_e[order], flat_t[order], ew.reshape(-1)[order]
    counts = jnp.bincount(flat_e, length=N_EXPERTS)
    padded = (counts + MOE_BLOCK - 1) // MOE_BLOCK * MOE_BLOCK
    pad_end = jnp.cumsum(padded)
    pad_start = pad_end - padded
    grp_start = jnp.cumsum(counts) - counts
    dest = pad_start[se] + (jnp.arange(a) - grp_start[se])
    nblk = -(-a // MOE_BLOCK) + N_EXPERTS
    p = nblk * MOE_BLOCK
    row_tok = jnp.full((p,), t, jnp.int32).at[dest].set(st)
    row_w = jnp.zeros((p,), F32).at[dest].set(sw)
    blk_exp = jnp.minimum(jnp.searchsorted(pad_end, jnp.arange(nblk) * MOE_BLOCK, side='right'),
                          N_EXPERTS - 1).astype(jnp.int32)
    nused = (pad_end[-1] // MOE_BLOCK).astype(jnp.int32).reshape(1)
    return row_tok, row_w, blk_exp, nused


def _final_kernel(h_ref, yr_ref, wg_ref, wu_ref, wd_ref, g_ref, b_ref, o_ref, *, alpha):
    h = h_ref[...]
    hb = h.astype(BF)
    g = jnp.dot(hb, wg_ref[...], preferred_element_type=F32)
    u = jnp.dot(hb, wu_ref[...], preferred_element_type=F32)
    shared = jnp.dot(((g * jax.nn.sigmoid(g)) * u).astype(BF), wd_ref[...],
                     preferred_element_type=F32)
    f = yr_ref[...] + shared
    o_ref[...] = _layer_norm(alpha * h + f, g_ref[...], b_ref[...])


def _final(h2, yr, wg_bf, wu_bf, wd_bf, g, b, alpha, tm):
    t = h2.shape[0]
    tile = pl.BlockSpec((tm, 1024), lambda i: (i, 0))
    full = lambda shape: pl.BlockSpec(shape, lambda i: (0, 0))
    return pl.pallas_call(
        functools.partial(_final_kernel, alpha=alpha),
        grid=(t // tm,),
        in_specs=[tile, tile, full((1024, 256)), full((1024, 256)), full((256, 1024)),
                  full((1, 1024)), full((1, 1024))],
        out_specs=tile,
        out_shape=jax.ShapeDtypeStruct((t, 1024), F32),
        compiler_params=_params("parallel"),
        name="shared_ln2",
    )(h2, yr, wg_bf, wu_bf, wd_bf, g, b)


def _permute_w_in(w_in):
    rq, rk, rv, rg, sq, sk, sv, mq, gl = jnp.split(
        w_in, [512, 1024, 2048, 3072, 4096, 4352, 4608, 5632], axis=-1)
    return jnp.concatenate([rq, rk, rv, rg, sq, mq, gl, sk, sv], -1)


def kernel(x_prompt, x_sample, mem_prompt, cache_ret_state, cache_swa_k, cache_swa_v, cache_mem_k,
           cache_mem_v, w_in, swa_sinks, w_mem_kv, w_branch, w_o, ln1_g, ln1_b, w_router,
           router_bias, w_gate_e, w_up_e, w_down_e, w_sh_gate, w_sh_up, w_sh_down, ln2_g, ln2_b):
    depth = w_in.shape[0]
    assert depth == 1
    alpha = (2.0 * depth) ** 0.25
    bp, lp, d = x_prompt.shape
    bs, ls, _ = x_sample.shape
    l = 0

    w_in_bf = _permute_w_in(w_in[l]).astype(BF)
    sinks = swa_sinks[l]
    wb_bf = w_branch[l].astype(BF)
    wo_bf = w_o[l].astype(BF)
    g1, b1 = ln1_g[l].reshape(1, d), ln1_b[l].reshape(1, d)
    g2, b2 = ln2_g[l].reshape(1, d), ln2_b[l].reshape(1, d)

    tp = bp * lp
    xp2 = x_prompt.reshape(tp, d)
    hp2 = _proj(xp2, w_in_bf, 1024, 512)
    hp3 = hp2.reshape(bp, lp, D_IN)
    mkv = _proj(mem_prompt.reshape(bp * N_MEM, d), w_mem_kv[l].astype(BF), N_MEM, 512)
    mk_p, mv_p = mkv[:, :1024].reshape(bp, N_MEM, 1024), mkv[:, 1024:].reshape(bp, N_MEM, 1024)
    rs0 = jnp.zeros((bp, RET_HEADS, RET_DK, RET_DV), F32)
    ro_p, rs_p = _retention(hp3, jnp.arange(lp), rs0, RET_CHUNK, RET_CHUNK)
    so_p = _swa_prompt(hp2, sinks)
    mo_p = _mem_attend(hp3, mk_p, mv_p, 256)
    hmid_p = _merge(ro_p.reshape(tp, d), so_p, mo_p.reshape(tp, d), hp2, xp2, wb_bf, wo_bf,
                    g1, b1, alpha, 256)

    ts = bs * SAMPLE_PAD
    xs3 = jnp.pad(x_sample, ((0, 0), (0, SAMPLE_PAD - ls), (0, 0)))
    xs2 = xs3.reshape(ts, d)
    hs2 = _proj(xs2, w_in_bf, ts, 512)
    hs3 = hs2.reshape(bs, SAMPLE_PAD, D_IN)
    pos_s = PAST_LEN + jnp.arange(SAMPLE_PAD)
    ro_s, rs_s = _retention(hs3, pos_s, cache_ret_state[l], ls, SAMPLE_PAD)
    w_buf = cache_swa_k.shape[2]
    prev_k = cache_swa_k[l].reshape(bs, w_buf, SWA_KV_HEADS * SWA_DH)
    prev_v = cache_swa_v[l].reshape(bs, w_buf, SWA_KV_HEADS * SWA_DH)
    so_s = _swa_sample(hs3, prev_k, prev_v, sinks, ls)
    mo_s = _mem_attend(hs3, cache_mem_k[l].reshape(bs, N_MEM, 1024),
                       cache_mem_v[l].reshape(bs, N_MEM, 1024), SAMPLE_PAD)
    hmid_s = _merge(ro_s.reshape(ts, d), so_s.reshape(ts, d), mo_s.reshape(ts, d), hs2, xs2,
                    wb_bf, wo_bf, g1, b1, alpha, 256)
    hmid_s = hmid_s.reshape(bs, SAMPLE_PAD, d)[:, :ls].reshape(bs * ls, d)

    hmid = jnp.concatenate([hmid_p, hmid_s], 0)
    t = hmid.shape[0]
    eidx_t, ew_t = _router(hmid, w_router[l].T.astype(BF), router_bias[l].reshape(N_EXPERTS, 1), 512)
    row_tok, row_w, blk_exp, nused = _dispatch(eidx_t.T, ew_t.T, t)
    xb = jnp.concatenate([hmid, jnp.zeros((1, d), F32)], 0)[row_tok]
    yb = _experts(xb, blk_exp, nused, w_gate_e[l], w_up_e[l], w_down_e[l])
    yr = jax.ops.segment_sum(yb * row_w[:, None], row_tok, num_segments=t + 1)[:t]
    y = _final(hmid, yr, w_sh_gate[l].astype(BF), w_sh_up[l].astype(BF), w_sh_down[l].astype(BF),
               g2, b2, alpha, 256)

    y_p = y[:tp].reshape(bp, lp, d)
    y_s = y[tp:].reshape(bs, ls, d)
    kv4 = lambda a, n: a.reshape(1, a.shape[0], n, SWA_KV_HEADS, SWA_DH)
    k_p = kv4(hp3[:, lp - WINDOW:, COL_SK:COL_SK + 256], WINDOW)
    v_p = kv4(hp3[:, lp - WINDOW:, COL_SV:COL_SV + 256], WINDOW)
    mem4 = lambda a: a.reshape(1, bp, N_MEM, MEM_HEADS, MEM_DH)
    k_s = kv4(jnp.concatenate([prev_k, hs3[:, :ls, COL_SK:COL_SK + 256]], 1)[:, -w_buf:], w_buf)
    v_s = kv4(jnp.concatenate([prev_v, hs3[:, :ls, COL_SV:COL_SV + 256]], 1)[:, -w_buf:], w_buf)
    return (y_p, y_s, rs_p[None], k_p, v_p, mem4(mk_p), mem4(mv_p), rs_s[None], k_s, v_s)
```

```python
import functools

import jax
import jax.numpy as jnp
from jax import lax
from jax.experimental import pallas as pl
from jax.experimental.pallas import tpu as pltpu

BF = jnp.bfloat16
F32 = jnp.float32

D_MODEL = 1024
RET_HEADS = 4
RET_DK = 128
RET_DV = 256
RET_CHUNK = 128
ROPE_BASE = 10000.0
SWA_HEADS = 16
SWA_KV_HEADS = 4
SWA_GROUP = SWA_HEADS // SWA_KV_HEADS
SWA_DH = 64
WINDOW = 128
SWA_BLOCK = 128
N_MEM = 256
MEM_HEADS = 4
MEM_DH = 256
N_BRANCH = 3
N_EXPERTS = 256
TOP_K = 8
N_GROUPS = 8
GROUP_SIZE = N_EXPERTS // N_GROUPS
TOPK_GROUPS = 4
D_EXPERT = 256
ROUTED_SCALE = 2.5
MOE_BLOCK = 128
LN_EPS = 1e-5
NORM_EPS = 1e-6
NEG = -1e30
PAST_LEN = 16384
SAMPLE_PAD = 8

COL_RQ, COL_RK, COL_RV, COL_RG, COL_SQ, COL_MQ, COL_GL, COL_SK, COL_SV = (
    0, 512, 1024, 2048, 3072, 4096, 5120, 8192, 8448)
D_IN = 8704

VMEM_LIMIT = 56 * 1024 * 1024


def _params(*sem):
    return pltpu.CompilerParams(dimension_semantics=sem, vmem_limit_bytes=VMEM_LIMIT)


def _bdot(a, b):
    return jnp.dot(a.astype(BF), b.astype(BF), preferred_element_type=F32)


def _bdot_nt(a, b):
    return lax.dot_general(a.astype(BF), b.astype(BF), (((1,), (1,)), ((), ())),
                           preferred_element_type=F32)


def _bdot_tn(a, b):
    return lax.dot_general(a.astype(BF), b.astype(BF), (((0,), (0,)), ((), ())),
                           preferred_element_type=F32)


def _layer_norm(z, g, b):
    zc = z - jnp.mean(z, -1, keepdims=True)
    var = jnp.mean(zc * zc, -1, keepdims=True)
    return zc * lax.rsqrt(var + LN_EPS) * g + b


def _proj_kernel(x_ref, w_ref, o_ref, xb_ref):
    @pl.when(pl.program_id(1) == 0)
    def _():
        xb_ref[...] = x_ref[...].astype(BF)

    o_ref[...] = jnp.dot(xb_ref[...], w_ref[...], preferred_element_type=F32)


def _proj(x, w_bf, tm, tn):
    m, k = x.shape
    n = w_bf.shape[1]
    return pl.pallas_call(
        _proj_kernel,
        grid=(m // tm, n // tn),
        in_specs=[pl.BlockSpec((tm, k), lambda i, j: (i, 0)),
                  pl.BlockSpec((k, tn), lambda i, j: (0, j))],
        out_specs=pl.BlockSpec((tm, tn), lambda i, j: (i, j)),
        out_shape=jax.ShapeDtypeStruct((m, n), F32),
        scratch_shapes=[pltpu.VMEM((tm, k), BF)],
        compiler_params=_params("parallel", "arbitrary"),
        name="proj",
    )(x, w_bf)


def _ret_tables(c_real, c_pad):
    lg = jnp.log1p(-jnp.exp2(-5.0 - jnp.arange(RET_HEADS, dtype=F32)))
    idx = jnp.arange(c_pad, dtype=F32)
    real = idx < c_real
    rel = idx[:, None] - idx[None, :]
    intra = jnp.where(rel >= 0, jnp.exp(lg[:, None, None] * jnp.maximum(rel, 0.0)), 0.0)
    intra = jnp.where(real[None, :, None] & real[None, None, :], intra, 0.0)
    q_dec = jnp.where(real[None, :], jnp.exp((idx[None, :] + 1.0) * lg[:, None]), 0.0)
    k_dec = jnp.where(real[None, :], jnp.exp((c_real - 1.0 - idx)[None, :] * lg[:, None]), 0.0)
    c_dec = jnp.exp(c_real * lg)
    bc = lambda t: jnp.broadcast_to(t[:, :, None], (RET_HEADS, c_pad, RET_DK))
    return intra, bc(q_dec), bc(k_dec), c_dec


def _rope_tables(pos):
    half = RET_DK // 2
    inv_freq = 1.0 / (ROPE_BASE ** (jnp.arange(half, dtype=F32) / half))
    ang = pos.astype(F32)[:, None] * inv_freq[None, :]
    cos, sin = jnp.cos(ang), jnp.sin(ang)
    return jnp.concatenate([cos, cos], -1), jnp.concatenate([-sin, sin], -1)


def _ret_kernel(cdec_ref, rq_ref, rk_ref, rv_ref, rg_ref, cos_ref, sin_ref, intra_ref, qdec_ref,
                kdec_ref, s0_ref, o_ref, s_out_ref, s_scr, *, n_chunks):
    c = pl.program_id(1)

    @pl.when(c == 0)
    def _():
        s_scr[...] = s0_ref[0]

    cos2 = cos_ref[...]
    sin2 = sin_ref[...]

    def rot(x):
        return x * cos2 + pltpu.roll(x, RET_DK // 2, 1) * sin2

    for h in range(RET_HEADS):
        q = rot(rq_ref[0, :, h * RET_DK:(h + 1) * RET_DK])
        k = rot(rk_ref[0, :, h * RET_DK:(h + 1) * RET_DK]) * (RET_DK ** -0.5)
        v = rv_ref[0, :, h * RET_DV:(h + 1) * RET_DV].astype(BF)
        s_old = s_scr[h]
        a = _bdot_nt(q, k) * intra_ref[h]
        o = _bdot(a, v) + _bdot(q * qdec_ref[h], s_old)
        s_scr[h] = s_old * cdec_ref[h] + _bdot_tn(k * kdec_ref[h], v)
        o = o * lax.rsqrt(jnp.mean(o * o, -1, keepdims=True) + NORM_EPS)
        g = rg_ref[0, :, h * RET_DV:(h + 1) * RET_DV]
        o_ref[0, :, h * RET_DV:(h + 1) * RET_DV] = o * (g * jax.nn.sigmoid(g))

    @pl.when(c == n_chunks - 1)
    def _():
        s_out_ref[0] = s_scr[...]


def _retention(h3, pos, state0, c_real, c_pad):
    b, l, _ = h3.shape
    n_chunks = l // c_pad
    intra, qdec, kdec, cdec = _ret_tables(c_real, c_pad)
    cos2, sin2 = _rope_tables(pos)
    full3 = lambda shape: pl.BlockSpec(shape, lambda i, c: (0, 0, 0))
    return pl.pallas_call(
        functools.partial(_ret_kernel, n_chunks=n_chunks),
        grid=(b, n_chunks),
        in_specs=[
            pl.BlockSpec(memory_space=pltpu.SMEM),
            pl.BlockSpec((1, c_pad, 512), lambda i, c: (i, c, COL_RQ // 512)),
            pl.BlockSpec((1, c_pad, 512), lambda i, c: (i, c, COL_RK // 512)),
            pl.BlockSpec((1, c_pad, 1024), lambda i, c: (i, c, COL_RV // 1024)),
            pl.BlockSpec((1, c_pad, 1024), lambda i, c: (i, c, COL_RG // 1024)),
            pl.BlockSpec((c_pad, RET_DK), lambda i, c: (c, 0)),
            pl.BlockSpec((c_pad, RET_DK), lambda i, c: (c, 0)),
            full3((RET_HEADS, c_pad, c_pad)),
            full3((RET_HEADS, c_pad, RET_DK)),
            full3((RET_HEADS, c_pad, RET_DK)),
            pl.BlockSpec((1, RET_HEADS, RET_DK, RET_DV), lambda i, c: (i, 0, 0, 0)),
        ],
        out_specs=[
            pl.BlockSpec((1, c_pad, 1024), lambda i, c: (i, c, 0)),
            pl.BlockSpec((1, RET_HEADS, RET_DK, RET_DV), lambda i, c: (i, 0, 0, 0)),
        ],
        out_shape=[jax.ShapeDtypeStruct((b, l, 1024), F32),
                   jax.ShapeDtypeStruct((b, RET_HEADS, RET_DK, RET_DV), F32)],
        scratch_shapes=[pltpu.VMEM((RET_HEADS, RET_DK, RET_DV), F32)],
        compiler_params=_params("parallel", "arbitrary"),
        name="retention",
    )(cdec, h3, h3, h3, h3, cos2, sin2, intra, qdec, kdec, state0)


def _sink_softmax(s, sink):
    m = jnp.maximum(jnp.max(s, -1, keepdims=True), sink)
    p = jnp.exp(s - m)
    return p / (jnp.sum(p, -1, keepdims=True) + jnp.exp(sink - m))


def _swa_prompt_kernel(sinks_ref, q_ref, kp_ref, kc_ref, vp_ref, vc_ref, o_ref):
    n = pl.program_id(0)
    qi = lax.broadcasted_iota(jnp.int32, (SWA_BLOCK, 2 * SWA_BLOCK), 0)
    kj = lax.broadcasted_iota(jnp.int32, (SWA_BLOCK, 2 * SWA_BLOCK), 1)
    rel = SWA_BLOCK + qi - kj
    valid = (rel >= 0) & (rel <= WINDOW) & ((kj >= SWA_BLOCK) | (n > 0))
    for h in range(SWA_KV_HEADS):
        sl = slice(h * SWA_DH, (h + 1) * SWA_DH)
        k2 = jnp.concatenate([kp_ref[:, sl], kc_ref[:, sl]], 0).astype(BF)
        v2 = jnp.concatenate([vp_ref[:, sl], vc_ref[:, sl]], 0).astype(BF)
        for g in range(SWA_GROUP):
            hq = h * SWA_GROUP + g
            qsl = slice(hq * SWA_DH, (hq + 1) * SWA_DH)
            s = _bdot_nt(q_ref[:, qsl], k2) * (SWA_DH ** -0.5)
            s = jnp.where(valid, s, NEG)
            p = _sink_softmax(s, sinks_ref[hq])
            o_ref[:, qsl] = _bdot(p, v2)


def _swa_prompt(h2, sinks):
    t = h2.shape[0]
    nb = t // SWA_BLOCK
    prev = lambda col: (lambda n: (jnp.maximum(n - 1, 0), col))
    cur = lambda col: (lambda n: (n, col))
    ck, cv = COL_SK // 256, COL_SV // 256
    return pl.pallas_call(
        _swa_prompt_kernel,
        grid=(nb,),
        in_specs=[
            pl.BlockSpec(memory_space=pltpu.SMEM),
            pl.BlockSpec((SWA_BLOCK, 1024), cur(COL_SQ // 1024)),
            pl.BlockSpec((SWA_BLOCK, 256), prev(ck)),
            pl.BlockSpec((SWA_BLOCK, 256), cur(ck)),
            pl.BlockSpec((SWA_BLOCK, 256), prev(cv)),
            pl.BlockSpec((SWA_BLOCK, 256), cur(cv)),
        ],
        out_specs=pl.BlockSpec((SWA_BLOCK, 1024), lambda n: (n, 0)),
        out_shape=jax.ShapeDtypeStruct((t, 1024), F32),
        compiler_params=_params("parallel"),
        name="swa_prompt",
    )(sinks, h2, h2, h2, h2, h2)


def _swa_sample_kernel(sinks_ref, q_ref, kn_ref, vn_ref, kp_ref, vp_ref, o_ref, *, n_new):
    tb = q_ref.shape[0]
    w = kp_ref.shape[1]
    p_ = SAMPLE_PAD
    qi = lax.broadcasted_iota(jnp.int32, (tb, p_, w), 1)
    kj = lax.broadcasted_iota(jnp.int32, (tb, p_, w), 2)
    rel_prev = w + qi - kj
    valid_prev = (rel_prev >= 0) & (rel_prev <= WINDOW)
    qn = lax.broadcasted_iota(jnp.int32, (tb, p_, p_), 1)
    kn = lax.broadcasted_iota(jnp.int32, (tb, p_, p_), 2)
    valid_new = (qn - kn >= 0) & (qn - kn <= WINDOW) & (kn < n_new)
    bdot = lambda eq, a, b: jnp.einsum(eq, a.astype(BF), b.astype(BF), preferred_element_type=F32)
    for h in range(SWA_KV_HEADS):
        sl = slice(h * SWA_DH, (h + 1) * SWA_DH)
        kp, vp = kp_ref[:, :, sl], vp_ref[:, :, sl]
        kn_h, vn_h = kn_ref[:, :, sl], vn_ref[:, :, sl]
        for g in range(SWA_GROUP):
            hq = h * SWA_GROUP + g
            qsl = slice(hq * SWA_DH, (hq + 1) * SWA_DH)
            q = q_ref[:, :, qsl]
            sp = bdot('bqd,bkd->bqk', q, kp) * (SWA_DH ** -0.5)
            sn = bdot('bqd,bkd->bqk', q, kn_h) * (SWA_DH ** -0.5)
            sp = jnp.where(valid_prev, sp, NEG)
            sn = jnp.where(valid_new, sn, NEG)
            sink = sinks_ref[hq]
            m = jnp.maximum(jnp.maximum(jnp.max(sp, -1, keepdims=True),
                                        jnp.max(sn, -1, keepdims=True)), sink)
            pp = jnp.exp(sp - m)
            pn = jnp.exp(sn - m)
            den = jnp.sum(pp, -1, keepdims=True) + jnp.sum(pn, -1, keepdims=True) + jnp.exp(sink - m)
            o = bdot('bqk,bkd->bqd', pp / den, vp) + bdot('bqk,bkd->bqd', pn / den, vn_h)
            o_ref[:, :, qsl] = o


def _swa_sample(h3, prev_k, prev_v, sinks, n_new, tb=8):
    b = h3.shape[0]
    w = prev_k.shape[1]
    return pl.pallas_call(
        functools.partial(_swa_sample_kernel, n_new=n_new),
        grid=(b // tb,),
        in_specs=[
            pl.BlockSpec(memory_space=pltpu.SMEM),
            pl.BlockSpec((tb, SAMPLE_PAD, 1024), lambda i: (i, 0, COL_SQ // 1024)),
            pl.BlockSpec((tb, SAMPLE_PAD, 256), lambda i: (i, 0, COL_SK // 256)),
            pl.BlockSpec((tb, SAMPLE_PAD, 256), lambda i: (i, 0, COL_SV // 256)),
            pl.BlockSpec((tb, w, 256), lambda i: (i, 0, 0)),
            pl.BlockSpec((tb, w, 256), lambda i: (i, 0, 0)),
        ],
        out_specs=pl.BlockSpec((tb, SAMPLE_PAD, 1024), lambda i: (i, 0, 0)),
        out_shape=jax.ShapeDtypeStruct((b, SAMPLE_PAD, 1024), F32),
        compiler_params=_params("parallel"),
        name="swa_sample",
    )(sinks, h3, h3, h3, prev_k, prev_v)


def _mem_kernel(q_ref, mk_ref, mv_ref, o_ref):
    for h in range(MEM_HEADS):
        sl = slice(h * MEM_DH, (h + 1) * MEM_DH)
        s = _bdot_nt(q_ref[0, :, sl], mk_ref[0, :, sl]) * (MEM_DH ** -0.5)
        m = jnp.max(s, -1, keepdims=True)
        e = jnp.exp(s - m)
        p = e / jnp.sum(e, -1, keepdims=True)
        o_ref[0, :, sl] = _bdot(p, mv_ref[0, :, sl])


def _mem_attend(h3, mk, mv, tl):
    b, l, _ = h3.shape
    return pl.pallas_call(
        _mem_kernel,
        grid=(b, l // tl),
        in_specs=[
            pl.BlockSpec((1, tl, 1024), lambda i, j: (i, j, COL_MQ // 1024)),
            pl.BlockSpec((1, N_MEM, 1024), lambda i, j: (i, 0, 0)),
            pl.BlockSpec((1, N_MEM, 1024), lambda i, j: (i, 0, 0)),
        ],
        out_specs=pl.BlockSpec((1, tl, 1024), lambda i, j: (i, j, 0)),
        out_shape=jax.ShapeDtypeStruct((b, l, 1024), F32),
        compiler_params=_params("parallel", "parallel"),
        name="mem_attend",
    )(h3, mk, mv)


def _pack_bf16_pairs(h):
    bits = lax.bitcast_convert_type(h.astype(BF).astype(F32), jnp.uint32)
    n = h.shape[1] // 2
    return bits[:, :n] | (bits[:, n:] >> 16)


def _unpack_bf16_pairs(pk):
    hi = lax.bitcast_convert_type(pk & jnp.uint32(0xFFFF0000), F32)
    lo = lax.bitcast_convert_type(pk << 16, F32)
    return jnp.concatenate([hi, lo], 1).astype(BF)


def _merge_kernel(ro_ref, so_ref, mo_ref, g0_ref, g1_ref, g2_ref, x_ref, wb_ref, wo_ref, g_ref,
                  b_ref, o_ref, opk_ref, *, alpha):
    acc = None
    for n, (br, gl) in enumerate(((ro_ref, g0_ref), (so_ref, g1_ref), (mo_ref, g2_ref))):
        term = jax.nn.sigmoid(gl[...]) * jnp.dot(br[...].astype(BF), wb_ref[n],
                                                 preferred_element_type=F32)
        acc = term if acc is None else acc + term
    a = jnp.dot(acc.astype(BF), wo_ref[...], preferred_element_type=F32)
    h = _layer_norm(alpha * x_ref[...] + a, g_ref[...], b_ref[...])
    o_ref[...] = h
    opk_ref[...] = _pack_bf16_pairs(h)


def _merge(ro, so, mo, h2, x2, wb_bf, wo_bf, g, b, alpha, tm):
    t = x2.shape[0]
    tile = lambda col: pl.BlockSpec((tm, 1024), lambda i: (i, col))
    gl0 = COL_GL // 1024
    return pl.pallas_call(
        functools.partial(_merge_kernel, alpha=alpha),
        grid=(t // tm,),
        in_specs=[tile(0), tile(0), tile(0), tile(gl0), tile(gl0 + 1), tile(gl0 + 2), tile(0),
                  pl.BlockSpec((N_BRANCH, 1024, 1024), lambda i: (0, 0, 0)),
                  pl.BlockSpec((1024, 1024), lambda i: (0, 0)),
                  pl.BlockSpec((1, 1024), lambda i: (0, 0)),
                  pl.BlockSpec((1, 1024), lambda i: (0, 0))],
        out_specs=[tile(0), pl.BlockSpec((tm, 512), lambda i: (i, 0))],
        out_shape=[jax.ShapeDtypeStruct((t, 1024), F32),
                   jax.ShapeDtypeStruct((t, 512), jnp.uint32)],
        compiler_params=_params("parallel"),
        name="merge_ln1",
    )(ro, so, mo, h2, h2, h2, x2, wb_bf, wo_bf, g, b)


def _first_index_of_max(v, iota, big, axes):
    m = jnp.max(v, axis=axes, keepdims=True)
    idx = jnp.min(jnp.where(v == m, iota, big), axis=axes, keepdims=True)
    return m, idx


def _router_kernel(x_ref, wr_ref, bias_ref, eidx_ref, ew_ref):
    tt = x_ref.shape[0]
    logits = lax.dot_general(wr_ref[...], x_ref[...].astype(BF), (((1,), (1,)), ((), ())),
                             preferred_element_type=F32)
    s = jax.nn.sigmoid(logits).reshape(N_GROUPS, GROUP_SIZE, tt)
    sb = s + bias_ref[...].reshape(N_GROUPS, GROUP_SIZE, 1)
    ninf = -jnp.inf
    r_iota = lax.broadcasted_iota(jnp.int32, sb.shape, 1)
    m1, i1 = _first_index_of_max(sb, r_iota, GROUP_SIZE, 1)
    m2 = jnp.max(jnp.where(r_iota == i1, ninf, sb), axis=1, keepdims=True)
    gsc = (m1 + m2).reshape(N_GROUPS, tt)
    g_iota = lax.broadcasted_iota(jnp.int32, gsc.shape, 0)
    gmask = jnp.zeros(gsc.shape, jnp.bool_)
    for _ in range(TOPK_GROUPS):
        _, gi = _first_index_of_max(gsc, g_iota, N_GROUPS, 0)
        hit = g_iota == gi
        gmask = gmask | hit
        gsc = jnp.where(hit, ninf, gsc)
    cand = jnp.where(gmask.reshape(N_GROUPS, 1, tt), sb, ninf)
    e_iota = lax.broadcasted_iota(jnp.int32, sb.shape, 0) * GROUP_SIZE + r_iota
    idxs, ws = [], []
    for _ in range(TOP_K):
        _, ei = _first_index_of_max(cand, e_iota, N_EXPERTS, (0, 1))
        hit = e_iota == ei
        idxs.append(ei.reshape(1, tt))
        ws.append(jnp.sum(jnp.where(hit, s, 0.0), axis=(0, 1)).reshape(1, tt))
        cand = jnp.where(hit, ninf, cand)
    w = jnp.concatenate(ws, 0)
    eidx_ref[...] = jnp.concatenate(idxs, 0)
    ew_ref[...] = w / jnp.sum(w, 0, keepdims=True) * ROUTED_SCALE


def _router(x2, wr_t_bf, bias_col, tt):
    t = x2.shape[0]
    return pl.pallas_call(
        _router_kernel,
        grid=(t // tt,),
        in_specs=[pl.BlockSpec((tt, 1024), lambda i: (i, 0)),
                  pl.BlockSpec((N_EXPERTS, 1024), lambda i: (0, 0)),
                  pl.BlockSpec((N_EXPERTS, 1), lambda i: (0, 0))],
        out_specs=[pl.BlockSpec((TOP_K, tt), lambda i: (0, i)),
                   pl.BlockSpec((TOP_K, tt), lambda i: (0, i))],
        out_shape=[jax.ShapeDtypeStruct((TOP_K, t), jnp.int32),
                   jax.ShapeDtypeStruct((TOP_K, t), F32)],
        compiler_params=_params("parallel"),
        name="router",
    )(x2, wr_t_bf, bias_col)


IDX_BITS = 18
HALF = D_MODEL // 2


def _dispatch(eidx, ew, t):
    a = t * TOP_K
    assert a < (1 << IDX_BITS) and t < (1 << 15)
    nblk = -(-a // MOE_BLOCK) + N_EXPERTS
    skey = jnp.sort(eidx.reshape(-1) * (1 << IDX_BITS) + jnp.arange(a, dtype=jnp.int32))
    se = skey >> IDX_BITS
    si = skey & ((1 << IDX_BITS) - 1)
    grp_start = jnp.searchsorted(se, jnp.arange(N_EXPERTS + 1, dtype=jnp.int32), side='left')
    counts = grp_start[1:] - grp_start[:-1]
    padded = (counts + MOE_BLOCK - 1) // MOE_BLOCK * MOE_BLOCK
    pad_end = jnp.cumsum(padded)
    pad_start = pad_end - padded
    blk_exp = jnp.minimum(jnp.searchsorted(pad_end, jnp.arange(nblk) * MOE_BLOCK, side='right'),
                          N_EXPERTS - 1).astype(jnp.int32)
    off = (jnp.arange(nblk * MOE_BLOCK, dtype=jnp.int32).reshape(nblk, MOE_BLOCK)
           - pad_start[blk_exp][:, None])
    valid = off < counts[blk_exp][:, None]
    src = jnp.clip(grp_start[:-1][blk_exp][:, None] + off, 0, a - 1)
    row_si = si[src]
    row_tok = jnp.where(valid, row_si >> 3, t).astype(jnp.int32)
    row_w = jnp.where(valid, ew.reshape(-1)[row_si], 0.0)
    nused = (pad_end[-1] // MOE_BLOCK).astype(jnp.int32).reshape(1)
    tokp = (row_tok[:, 0::2] | (row_tok[:, 1::2] << 16)).reshape(-1)
    return tokp, row_w.reshape(-1, 1), blk_exp, nused


def _block_tokens(tokp_ref, b):
    toks = []
    for j in range(MOE_BLOCK // 2):
        word = tokp_ref[b * (MOE_BLOCK // 2) + j]
        toks += [word & 0xFFFF, word >> 16]
    return toks


def _up_kernel(be_ref, nused_ref, tokp_ref, x_ref, wg_ref, wu_ref, act_ref, xg_ref):
    b = pl.program_id(0)

    @pl.when(b < nused_ref[0])
    def _():
        for r, tok in enumerate(_block_tokens(tokp_ref, b)):
            xg_ref[r:r + 1, :] = x_ref[pl.ds(tok, 1), :]
        x = _unpack_bf16_pairs(xg_ref[...])
        g = jnp.dot(x, wg_ref[0].astype(BF), preferred_element_type=F32)
        u = jnp.dot(x, wu_ref[0].astype(BF), preferred_element_type=F32)
        act_ref[...] = ((g * jax.nn.sigmoid(g)) * u).astype(BF)

    @pl.when(b >= nused_ref[0])
    def _():
        act_ref[...] = jnp.zeros_like(act_ref)


def _experts_up(xpk, tokp, blk_exp, nused, w_gate, w_up):
    nblk = blk_exp.shape[0]
    wmap = lambda i, be, nu, tk: (be[i], 0, 0)
    return pl.pallas_call(
        _up_kernel,
        grid_spec=pltpu.PrefetchScalarGridSpec(
            num_scalar_prefetch=3,
            grid=(nblk,),
            in_specs=[pl.BlockSpec(memory_space=pltpu.VMEM),
                      pl.BlockSpec((1, D_MODEL, D_EXPERT), wmap),
                      pl.BlockSpec((1, D_MODEL, D_EXPERT), wmap)],
            out_specs=pl.BlockSpec((MOE_BLOCK, D_EXPERT), lambda i, be, nu, tk: (i, 0)),
            scratch_shapes=[pltpu.VMEM((MOE_BLOCK, HALF), jnp.uint32)],
        ),
        out_shape=jax.ShapeDtypeStruct((nblk * MOE_BLOCK, D_EXPERT), BF),
        compiler_params=_params("arbitrary"),
        name="experts_up",
    )(blk_exp, nused, tokp, xpk, w_gate, w_up)


SCATTER_GROUP = 8


def _down_kernel(be_ref, nused_ref, tokp_ref, act_ref, wd_ref, rw_ref, y_ref, yb_ref):
    b = pl.program_id(0)

    @pl.when(b == 0)
    def _():
        y_ref[...] = jnp.zeros_like(y_ref)

    @pl.when(b < nused_ref[0])
    def _():
        yb_ref[...] = jnp.dot(act_ref[...], wd_ref[0].astype(BF),
                              preferred_element_type=F32) * rw_ref[...]
        toks = _block_tokens(tokp_ref, b)
        for r0 in range(0, MOE_BLOCK, SCATTER_GROUP):
            rows = range(r0, r0 + SCATTER_GROUP)
            new = [y_ref[pl.ds(toks[r], 1), :] + yb_ref[r:r + 1, :] for r in rows]
            for r, v in zip(rows, new):
                y_ref[pl.ds(toks[r], 1), :] = v


def _experts_down(act, tokp, row_w, blk_exp, nused, w_down, half, t_rows):
    nblk = blk_exp.shape[0]
    blk = lambda i, be, nu, tk: (i, 0)
    return pl.pallas_call(
        _down_kernel,
        grid_spec=pltpu.PrefetchScalarGridSpec(
            num_scalar_prefetch=3,
            grid=(nblk,),
            in_specs=[pl.BlockSpec((MOE_BLOCK, D_EXPERT), blk),
                      pl.BlockSpec((1, D_EXPERT, HALF), lambda i, be, nu, tk: (be[i], 0, half)),
                      pl.BlockSpec((MOE_BLOCK, 1), blk)],
            out_specs=pl.BlockSpec(memory_space=pltpu.VMEM),
            scratch_shapes=[pltpu.VMEM((MOE_BLOCK, HALF), F32)],
        ),
        out_shape=jax.ShapeDtypeStruct((t_rows, HALF), F32),
        compiler_params=_params("arbitrary"),
        name="experts_down",
    )(blk_exp, nused, tokp, act, w_down, row_w)


def _final_kernel(h_ref, y0_ref, y1_ref, wg_ref, wu_ref, wd_ref, g_ref, b_ref, o_ref, *, alpha):
    h = h_ref[...]
    hb = h.astype(BF)
    g = jnp.dot(hb, wg_ref[...], preferred_element_type=F32)
    u = jnp.dot(hb, wu_ref[...], preferred_element_type=F32)
    shared = jnp.dot(((g * jax.nn.sigmoid(g)) * u).astype(BF), wd_ref[...],
                     preferred_element_type=F32)
    f = jnp.concatenate([y0_ref[...], y1_ref[...]], 1) + shared
    o_ref[...] = _layer_norm(alpha * h + f, g_ref[...], b_ref[...])


def _final(h2, y0, y1, wg_bf, wu_bf, wd_bf, g, b, alpha, tm):
    t = h2.shape[0]
    tile = pl.BlockSpec((tm, 1024), lambda i: (i, 0))
    htile = pl.BlockSpec((tm, HALF), lambda i: (i, 0))
    full = lambda shape: pl.BlockSpec(shape, lambda i: (0, 0))
    return pl.pallas_call(
        functools.partial(_final_kernel, alpha=alpha),
        grid=(t // tm,),
        in_specs=[tile, htile, htile, full((1024, 256)), full((1024, 256)), full((256, 1024)),
                  full((1, 1024)), full((1, 1024))],
        out_specs=tile,
        out_shape=jax.ShapeDtypeStruct((t, 1024), F32),
        compiler_params=_params("parallel"),
        name="shared_ln2",
    )(h2, y0, y1, wg_bf, wu_bf, wd_bf, g, b)


def _permute_w_in(w_in):
    rq, rk, rv, rg, sq, sk, sv, mq, gl = jnp.split(
        w_in, [512, 1024, 2048, 3072, 4096, 4352, 4608, 5632], axis=-1)
    return jnp.concatenate([rq, rk, rv, rg, sq, mq, gl, sk, sv], -1)


def kernel(x_prompt, x_sample, mem_prompt, cache_ret_state, cache_swa_k, cache_swa_v, cache_mem_k,
           cache_mem_v, w_in, swa_sinks, w_mem_kv, w_branch, w_o, ln1_g, ln1_b, w_router,
           router_bias, w_gate_e, w_up_e, w_down_e, w_sh_gate, w_sh_up, w_sh_down, ln2_g, ln2_b):
    depth = w_in.shape[0]
    assert depth == 1
    alpha = (2.0 * depth) ** 0.25
    bp, lp, d = x_prompt.shape
    bs, ls, _ = x_sample.shape
    l = 0

    w_in_bf = _permute_w_in(w_in[l]).astype(BF)
    sinks = swa_sinks[l]
    wb_bf = w_branch[l].astype(BF)
    wo_bf = w_o[l].astype(BF)
    g1, b1 = ln1_g[l].reshape(1, d), ln1_b[l].reshape(1, d)
    g2, b2 = ln2_g[l].reshape(1, d), ln2_b[l].reshape(1, d)

    tp = bp * lp
    xp2 = x_prompt.reshape(tp, d)
    hp2 = _proj(xp2, w_in_bf, 1024, 512)
    hp3 = hp2.reshape(bp, lp, D_IN)
    mkv = _proj(mem_prompt.reshape(bp * N_MEM, d), w_mem_kv[l].astype(BF), N_MEM, 512)
    mk_p, mv_p = mkv[:, :1024].reshape(bp, N_MEM, 1024), mkv[:, 1024:].reshape(bp, N_MEM, 1024)
    rs0 = jnp.zeros((bp, RET_HEADS, RET_DK, RET_DV), F32)
    ro_p, rs_p = _retention(hp3, jnp.arange(lp), rs0, RET_CHUNK, RET_CHUNK)
    so_p = _swa_prompt(hp2, sinks)
    mo_p = _mem_attend(hp3, mk_p, mv_p, 256)
    hmid_p, hpk_p = _merge(ro_p.reshape(tp, d), so_p, mo_p.reshape(tp, d), hp2, xp2, wb_bf, wo_bf,
                           g1, b1, alpha, 256)

    ts = bs * SAMPLE_PAD
    xs3 = jnp.pad(x_sample, ((0, 0), (0, SAMPLE_PAD - ls), (0, 0)))
    xs2 = xs3.reshape(ts, d)
    hs2 = _proj(xs2, w_in_bf, ts, 512)
    hs3 = hs2.reshape(bs, SAMPLE_PAD, D_IN)
    pos_s = PAST_LEN + jnp.arange(SAMPLE_PAD)
    ro_s, rs_s = _retention(hs3, pos_s, cache_ret_state[l], ls, SAMPLE_PAD)
    w_buf = cache_swa_k.shape[2]
    prev_k = cache_swa_k[l].reshape(bs, w_buf, SWA_KV_HEADS * SWA_DH)
    prev_v = cache_swa_v[l].reshape(bs, w_buf, SWA_KV_HEADS * SWA_DH)
    so_s = _swa_sample(hs3, prev_k, prev_v, sinks, ls)
    mo_s = _mem_attend(hs3, cache_mem_k[l].reshape(bs, N_MEM, 1024),
                       cache_mem_v[l].reshape(bs, N_MEM, 1024), SAMPLE_PAD)
    hmid_s, hpk_s = _merge(ro_s.reshape(ts, d), so_s.reshape(ts, d), mo_s.reshape(ts, d), hs2, xs2,
                           wb_bf, wo_bf, g1, b1, alpha, 256)
    real = lambda a_: a_.reshape(bs, SAMPLE_PAD, -1)[:, :ls].reshape(bs * ls, -1)

    hmid = jnp.concatenate([hmid_p, real(hmid_s)], 0)
    t = hmid.shape[0]
    t_rows = t + 8
    hpk = jnp.concatenate([hpk_p, real(hpk_s), jnp.zeros((8, HALF), jnp.uint32)], 0)
    eidx_t, ew_t = _router(hmid, w_router[l].T.astype(BF), router_bias[l].reshape(N_EXPERTS, 1), 512)
    tokp, row_w, blk_exp, nused = _dispatch(eidx_t.T, ew_t.T, t)
    act = _experts_up(hpk, tokp, blk_exp, nused, w_gate_e[l], w_up_e[l])
    y0, y1 = (_experts_down(act, tokp, row_w, blk_exp, nused, w_down_e[l], half, t_rows)
              for half in range(2))
    y = _final(hmid, y0, y1, w_sh_gate[l].astype(BF), w_sh_up[l].astype(BF),
               w_sh_down[l].astype(BF), g2, b2, alpha, 256)

    y_p = y[:tp].reshape(bp, lp, d)
    y_s = y[tp:].reshape(bs, ls, d)
    kv4 = lambda a, n: a.reshape(1, a.shape[0], n, SWA_KV_HEADS, SWA_DH)
    k_p = kv4(hp3[:, lp - WINDOW:, COL_SK:COL_SK + 256], WINDOW)
    v_p = kv4(hp3[:, lp - WINDOW:, COL_SV:COL_SV + 256], WINDOW)
    mem4 = lambda a: a.reshape(1, bp, N_MEM, MEM_HEADS, MEM_DH)
    k_s = kv4(jnp.concatenate([prev_k, hs3[:, :ls, COL_SK:COL_SK + 256]], 1)[:, -w_buf:], w_buf)
    v_s = kv4(jnp.concatenate([prev_v, hs3[:, :ls, COL_SV:COL_SV + 256]], 1)[:, -w_buf:], w_buf)
    return (y_p, y_s, rs_p[None], k_p, v_p, mem4(mk_p), mem4(mv_p), rs_s[None], k_s, v_s)
```

```python
import functools

import jax
import jax.numpy as jnp
from jax import lax
from jax.experimental import pallas as pl
from jax.experimental.pallas import tpu as pltpu

BF = jnp.bfloat16
F32 = jnp.float32

D_MODEL = 1024
RET_HEADS = 4
RET_DK = 128
RET_DV = 256
RET_CHUNK = 128
ROPE_BASE = 10000.0
SWA_HEADS = 16
SWA_KV_HEADS = 4
SWA_GROUP = SWA_HEADS // SWA_KV_HEADS
SWA_DH = 64
WINDOW = 128
SWA_BLOCK = 128
N_MEM = 256
MEM_HEADS = 4
MEM_DH = 256
N_BRANCH = 3
N_EXPERTS = 256
TOP_K = 8
N_GROUPS = 8
GROUP_SIZE = N_EXPERTS // N_GROUPS
TOPK_GROUPS = 4
D_EXPERT = 256
ROUTED_SCALE = 2.5
MOE_BLOCK = 128
ROUTER_TILE = 384
STEP_BLOCKS = 4
LN_EPS = 1e-5
NORM_EPS = 1e-6
NEG = -1e30
PAST_LEN = 16384
SAMPLE_PAD = 8

COL_RQ, COL_RK, COL_RV, COL_RG, COL_SQ, COL_MQ, COL_GL, COL_SK, COL_SV = (
    0, 512, 1024, 2048, 3072, 4096, 5120, 8192, 8448)
D_IN = 8704

VMEM_LIMIT = 56 * 1024 * 1024


def _params(*sem):
    return pltpu.CompilerParams(dimension_semantics=sem, vmem_limit_bytes=VMEM_LIMIT)


def _bdot(a, b):
    return jnp.dot(a.astype(BF), b.astype(BF), preferred_element_type=F32)


def _bdot_nt(a, b):
    return lax.dot_general(a.astype(BF), b.astype(BF), (((1,), (1,)), ((), ())),
                           preferred_element_type=F32)


def _bdot_tn(a, b):
    return lax.dot_general(a.astype(BF), b.astype(BF), (((0,), (0,)), ((), ())),
                           preferred_element_type=F32)


def _layer_norm(z, g, b):
    zc = z - jnp.mean(z, -1, keepdims=True)
    var = jnp.mean(zc * zc, -1, keepdims=True)
    return zc * lax.rsqrt(var + LN_EPS) * g + b


def _proj_kernel(x_ref, w_ref, o_ref, xb_ref):
    @pl.when(pl.program_id(1) == 0)
    def _():
        xb_ref[...] = x_ref[...].astype(BF)

    o_ref[...] = jnp.dot(xb_ref[...], w_ref[...], preferred_element_type=F32)


def _proj(x, w_bf, tm, tn):
    m, k = x.shape
    n = w_bf.shape[1]
    return pl.pallas_call(
        _proj_kernel,
        grid=(m // tm, n // tn),
        in_specs=[pl.BlockSpec((tm, k), lambda i, j: (i, 0)),
                  pl.BlockSpec((k, tn), lambda i, j: (0, j))],
        out_specs=pl.BlockSpec((tm, tn), lambda i, j: (i, j)),
        out_shape=jax.ShapeDtypeStruct((m, n), F32),
        scratch_shapes=[pltpu.VMEM((tm, k), BF)],
        compiler_params=_params("parallel", "arbitrary"),
        name="proj",
    )(x, w_bf)


def _ret_tables(c_real, c_pad):
    lg = jnp.log1p(-jnp.exp2(-5.0 - jnp.arange(RET_HEADS, dtype=F32)))
    idx = jnp.arange(c_pad, dtype=F32)
    real = idx < c_real
    rel = idx[:, None] - idx[None, :]
    intra = jnp.where(rel >= 0, jnp.exp(lg[:, None, None] * jnp.maximum(rel, 0.0)), 0.0)
    intra = jnp.where(real[None, :, None] & real[None, None, :], intra, 0.0)
    q_dec = jnp.where(real[None, :], jnp.exp((idx[None, :] + 1.0) * lg[:, None]), 0.0)
    k_dec = jnp.where(real[None, :], jnp.exp((c_real - 1.0 - idx)[None, :] * lg[:, None]), 0.0)
    c_dec = jnp.exp(c_real * lg)
    bc = lambda t: jnp.broadcast_to(t[:, :, None], (RET_HEADS, c_pad, RET_DK))
    return intra, bc(q_dec), bc(k_dec), c_dec


def _rope_tables(pos):
    half = RET_DK // 2
    inv_freq = 1.0 / (ROPE_BASE ** (jnp.arange(half, dtype=F32) / half))
    ang = pos.astype(F32)[:, None] * inv_freq[None, :]
    cos, sin = jnp.cos(ang), jnp.sin(ang)
    return jnp.concatenate([cos, cos], -1), jnp.concatenate([-sin, sin], -1)


def _ret_kernel(cdec_ref, rq_ref, rk_ref, rv_ref, rg_ref, cos_ref, sin_ref, intra_ref, qdec_ref,
                kdec_ref, s0_ref, o_ref, s_out_ref, s_scr, *, n_chunks):
    c = pl.program_id(1)

    @pl.when(c == 0)
    def _():
        s_scr[...] = s0_ref[0]

    cos2 = cos_ref[...]
    sin2 = sin_ref[...]

    def rot(x):
        return x * cos2 + pltpu.roll(x, RET_DK // 2, 1) * sin2

    for h in range(RET_HEADS):
        q = rot(rq_ref[0, :, h * RET_DK:(h + 1) * RET_DK])
        k = rot(rk_ref[0, :, h * RET_DK:(h + 1) * RET_DK]) * (RET_DK ** -0.5)
        v = rv_ref[0, :, h * RET_DV:(h + 1) * RET_DV].astype(BF)
        s_old = s_scr[h]
        a = _bdot_nt(q, k) * intra_ref[h]
        o = _bdot(a, v) + _bdot(q * qdec_ref[h], s_old)
        s_scr[h] = s_old * cdec_ref[h] + _bdot_tn(k * kdec_ref[h], v)
        o = o * lax.rsqrt(jnp.mean(o * o, -1, keepdims=True) + NORM_EPS)
        g = rg_ref[0, :, h * RET_DV:(h + 1) * RET_DV]
        o_ref[0, :, h * RET_DV:(h + 1) * RET_DV] = o * (g * jax.nn.sigmoid(g))

    @pl.when(c == n_chunks - 1)
    def _():
        s_out_ref[0] = s_scr[...]


def _retention(h3, pos, state0, c_real, c_pad):
    b, l, _ = h3.shape
    n_chunks = l // c_pad
    intra, qdec, kdec, cdec = _ret_tables(c_real, c_pad)
    cos2, sin2 = _rope_tables(pos)
    full3 = lambda shape: pl.BlockSpec(shape, lambda i, c: (0, 0, 0))
    return pl.pallas_call(
        functools.partial(_ret_kernel, n_chunks=n_chunks),
        grid=(b, n_chunks),
        in_specs=[
            pl.BlockSpec(memory_space=pltpu.SMEM),
            pl.BlockSpec((1, c_pad, 512), lambda i, c: (i, c, COL_RQ // 512)),
            pl.BlockSpec((1, c_pad, 512), lambda i, c: (i, c, COL_RK // 512)),
            pl.BlockSpec((1, c_pad, 1024), lambda i, c: (i, c, COL_RV // 1024)),
            pl.BlockSpec((1, c_pad, 1024), lambda i, c: (i, c, COL_RG // 1024)),
            pl.BlockSpec((c_pad, RET_DK), lambda i, c: (c, 0)),
            pl.BlockSpec((c_pad, RET_DK), lambda i, c: (c, 0)),
            full3((RET_HEADS, c_pad, c_pad)),
            full3((RET_HEADS, c_pad, RET_DK)),
            full3((RET_HEADS, c_pad, RET_DK)),
            pl.BlockSpec((1, RET_HEADS, RET_DK, RET_DV), lambda i, c: (i, 0, 0, 0)),
        ],
        out_specs=[
            pl.BlockSpec((1, c_pad, 1024), lambda i, c: (i, c, 0)),
            pl.BlockSpec((1, RET_HEADS, RET_DK, RET_DV), lambda i, c: (i, 0, 0, 0)),
        ],
        out_shape=[jax.ShapeDtypeStruct((b, l, 1024), F32),
                   jax.ShapeDtypeStruct((b, RET_HEADS, RET_DK, RET_DV), F32)],
        scratch_shapes=[pltpu.VMEM((RET_HEADS, RET_DK, RET_DV), F32)],
        compiler_params=_params("parallel", "arbitrary"),
        name="retention",
    )(cdec, h3, h3, h3, h3, cos2, sin2, intra, qdec, kdec, state0)


def _sink_softmax(s, sink):
    m = jnp.maximum(jnp.max(s, -1, keepdims=True), sink)
    p = jnp.exp(s - m)
    return p / (jnp.sum(p, -1, keepdims=True) + jnp.exp(sink - m))


def _swa_prompt_kernel(sinks_ref, q_ref, kp_ref, kc_ref, vp_ref, vc_ref, o_ref):
    n = pl.program_id(0)
    qi = lax.broadcasted_iota(jnp.int32, (SWA_BLOCK, 2 * SWA_BLOCK), 0)
    kj = lax.broadcasted_iota(jnp.int32, (SWA_BLOCK, 2 * SWA_BLOCK), 1)
    rel = SWA_BLOCK + qi - kj
    valid = (rel >= 0) & (rel <= WINDOW) & ((kj >= SWA_BLOCK) | (n > 0))
    for h in range(SWA_KV_HEADS):
        sl = slice(h * SWA_DH, (h + 1) * SWA_DH)
        k2 = jnp.concatenate([kp_ref[:, sl], kc_ref[:, sl]], 0).astype(BF)
        v2 = jnp.concatenate([vp_ref[:, sl], vc_ref[:, sl]], 0).astype(BF)
        for g in range(SWA_GROUP):
            hq = h * SWA_GROUP + g
            qsl = slice(hq * SWA_DH, (hq + 1) * SWA_DH)
            s = _bdot_nt(q_ref[:, qsl], k2) * (SWA_DH ** -0.5)
            s = jnp.where(valid, s, NEG)
            p = _sink_softmax(s, sinks_ref[hq])
            o_ref[:, qsl] = _bdot(p, v2)


def _swa_prompt(h2, sinks):
    t = h2.shape[0]
    nb = t // SWA_BLOCK
    prev = lambda col: (lambda n: (jnp.maximum(n - 1, 0), col))
    cur = lambda col: (lambda n: (n, col))
    ck, cv = COL_SK // 256, COL_SV // 256
    return pl.pallas_call(
        _swa_prompt_kernel,
        grid=(nb,),
        in_specs=[
            pl.BlockSpec(memory_space=pltpu.SMEM),
            pl.BlockSpec((SWA_BLOCK, 1024), cur(COL_SQ // 1024)),
            pl.BlockSpec((SWA_BLOCK, 256), prev(ck)),
            pl.BlockSpec((SWA_BLOCK, 256), cur(ck)),
            pl.BlockSpec((SWA_BLOCK, 256), prev(cv)),
            pl.BlockSpec((SWA_BLOCK, 256), cur(cv)),
        ],
        out_specs=pl.BlockSpec((SWA_BLOCK, 1024), lambda n: (n, 0)),
        out_shape=jax.ShapeDtypeStruct((t, 1024), F32),
        compiler_params=_params("parallel"),
        name="swa_prompt",
    )(sinks, h2, h2, h2, h2, h2)


def _swa_sample_kernel(sinks_ref, q_ref, kn_ref, vn_ref, kp_ref, vp_ref, o_ref, *, n_new):
    tb = q_ref.shape[0]
    w = kp_ref.shape[1]
    p_ = SAMPLE_PAD
    qi = lax.broadcasted_iota(jnp.int32, (tb, p_, w), 1)
    kj = lax.broadcasted_iota(jnp.int32, (tb, p_, w), 2)
    rel_prev = w + qi - kj
    valid_prev = (rel_prev >= 0) & (rel_prev <= WINDOW)
    qn = lax.broadcasted_iota(jnp.int32, (tb, p_, p_), 1)
    kn = lax.broadcasted_iota(jnp.int32, (tb, p_, p_), 2)
    valid_new = (qn - kn >= 0) & (qn - kn <= WINDOW) & (kn < n_new)
    bdot = lambda eq, a, b: jnp.einsum(eq, a.astype(BF), b.astype(BF), preferred_element_type=F32)
    for h in range(SWA_KV_HEADS):
        sl = slice(h * SWA_DH, (h + 1) * SWA_DH)
        kp, vp = kp_ref[:, :, sl], vp_ref[:, :, sl]
        kn_h, vn_h = kn_ref[:, :, sl], vn_ref[:, :, sl]
        for g in range(SWA_GROUP):
            hq = h * SWA_GROUP + g
            qsl = slice(hq * SWA_DH, (hq + 1) * SWA_DH)
            q = q_ref[:, :, qsl]
            sp = bdot('bqd,bkd->bqk', q, kp) * (SWA_DH ** -0.5)
            sn = bdot('bqd,bkd->bqk', q, kn_h) * (SWA_DH ** -0.5)
            sp = jnp.where(valid_prev, sp, NEG)
            sn = jnp.where(valid_new, sn, NEG)
            sink = sinks_ref[hq]
            m = jnp.maximum(jnp.maximum(jnp.max(sp, -1, keepdims=True),
                                        jnp.max(sn, -1, keepdims=True)), sink)
            pp = jnp.exp(sp - m)
            pn = jnp.exp(sn - m)
            den = jnp.sum(pp, -1, keepdims=True) + jnp.sum(pn, -1, keepdims=True) + jnp.exp(sink - m)
            o = bdot('bqk,bkd->bqd', pp / den, vp) + bdot('bqk,bkd->bqd', pn / den, vn_h)
            o_ref[:, :, qsl] = o


def _swa_sample(h3, prev_k, prev_v, sinks, n_new, tb=8):
    b = h3.shape[0]
    w = prev_k.shape[1]
    return pl.pallas_call(
        functools.partial(_swa_sample_kernel, n_new=n_new),
        grid=(b // tb,),
        in_specs=[
            pl.BlockSpec(memory_space=pltpu.SMEM),
            pl.BlockSpec((tb, SAMPLE_PAD, 1024), lambda i: (i, 0, COL_SQ // 1024)),
            pl.BlockSpec((tb, SAMPLE_PAD, 256), lambda i: (i, 0, COL_SK // 256)),
            pl.BlockSpec((tb, SAMPLE_PAD, 256), lambda i: (i, 0, COL_SV // 256)),
            pl.BlockSpec((tb, w, 256), lambda i: (i, 0, 0)),
            pl.BlockSpec((tb, w, 256), lambda i: (i, 0, 0)),
        ],
        out_specs=pl.BlockSpec((tb, SAMPLE_PAD, 1024), lambda i: (i, 0, 0)),
        out_shape=jax.ShapeDtypeStruct((b, SAMPLE_PAD, 1024), F32),
        compiler_params=_params("parallel"),
        name="swa_sample",
    )(sinks, h3, h3, h3, prev_k, prev_v)


def _mem_kernel(q_ref, mk_ref, mv_ref, o_ref):
    for h in range(MEM_HEADS):
        sl = slice(h * MEM_DH, (h + 1) * MEM_DH)
        s = _bdot_nt(q_ref[0, :, sl], mk_ref[0, :, sl]) * (MEM_DH ** -0.5)
        m = jnp.max(s, -1, keepdims=True)
        e = jnp.exp(s - m)
        p = e / jnp.sum(e, -1, keepdims=True)
        o_ref[0, :, sl] = _bdot(p, mv_ref[0, :, sl])


def _mem_attend(h3, mk, mv, tl):
    b, l, _ = h3.shape
    return pl.pallas_call(
        _mem_kernel,
        grid=(b, l // tl),
        in_specs=[
            pl.BlockSpec((1, tl, 1024), lambda i, j: (i, j, COL_MQ // 1024)),
            pl.BlockSpec((1, N_MEM, 1024), lambda i, j: (i, 0, 0)),
            pl.BlockSpec((1, N_MEM, 1024), lambda i, j: (i, 0, 0)),
        ],
        out_specs=pl.BlockSpec((1, tl, 1024), lambda i, j: (i, j, 0)),
        out_shape=jax.ShapeDtypeStruct((b, l, 1024), F32),
        compiler_params=_params("parallel", "parallel"),
        name="mem_attend",
    )(h3, mk, mv)


def _merge_kernel(ro_ref, so_ref, mo_ref, g0_ref, g1_ref, g2_ref, x_ref, wb_ref, wo_ref, g_ref,
                  b_ref, o_ref, *, alpha):
    acc = None
    for n, (br, gl) in enumerate(((ro_ref, g0_ref), (so_ref, g1_ref), (mo_ref, g2_ref))):
        term = jax.nn.sigmoid(gl[...]) * jnp.dot(br[...].astype(BF), wb_ref[n],
                                                 preferred_element_type=F32)
        acc = term if acc is None else acc + term
    a = jnp.dot(acc.astype(BF), wo_ref[...], preferred_element_type=F32)
    o_ref[...] = _layer_norm(alpha * x_ref[...] + a, g_ref[...], b_ref[...])


def _merge(ro, so, mo, h2, x2, wb_bf, wo_bf, g, b, alpha, tm):
    t = x2.shape[0]
    tile = lambda col: pl.BlockSpec((tm, 1024), lambda i: (i, col))
    gl0 = COL_GL // 1024
    return pl.pallas_call(
        functools.partial(_merge_kernel, alpha=alpha),
        grid=(t // tm,),
        in_specs=[tile(0), tile(0), tile(0), tile(gl0), tile(gl0 + 1), tile(gl0 + 2), tile(0),
                  pl.BlockSpec((N_BRANCH, 1024, 1024), lambda i: (0, 0, 0)),
                  pl.BlockSpec((1024, 1024), lambda i: (0, 0)),
                  pl.BlockSpec((1, 1024), lambda i: (0, 0)),
                  pl.BlockSpec((1, 1024), lambda i: (0, 0))],
        out_specs=tile(0),
        out_shape=jax.ShapeDtypeStruct((t, 1024), F32),
        compiler_params=_params("parallel"),
        name="merge_ln1",
    )(ro, so, mo, h2, h2, h2, x2, wb_bf, wo_bf, g, b)


def _first_index_of_max(v, iota, big, axes):
    m = jnp.max(v, axis=axes, keepdims=True)
    idx = jnp.min(jnp.where(v == m, iota, big), axis=axes, keepdims=True)
    return m, idx


def _router_kernel(x_ref, wr_ref, bias_ref, eidx_ref, ew_ref):
    tt = x_ref.shape[0]
    logits = lax.dot_general(wr_ref[...], x_ref[...].astype(BF), (((1,), (1,)), ((), ())),
                             preferred_element_type=F32)
    s = jax.nn.sigmoid(logits).reshape(N_GROUPS, GROUP_SIZE, tt)
    sb = s + bias_ref[...].reshape(N_GROUPS, GROUP_SIZE, 1)
    ninf = -jnp.inf
    r_iota = lax.broadcasted_iota(jnp.int32, sb.shape, 1)
    m1, i1 = _first_index_of_max(sb, r_iota, GROUP_SIZE, 1)
    m2 = jnp.max(jnp.where(r_iota == i1, ninf, sb), axis=1, keepdims=True)
    gsc = (m1 + m2).reshape(N_GROUPS, tt)
    g_iota = lax.broadcasted_iota(jnp.int32, gsc.shape, 0)
    gmask = jnp.zeros(gsc.shape, jnp.bool_)
    for _ in range(TOPK_GROUPS):
        _, gi = _first_index_of_max(gsc, g_iota, N_GROUPS, 0)
        hit = g_iota == gi
        gmask = gmask | hit
        gsc = jnp.where(hit, ninf, gsc)
    cand = jnp.where(gmask.reshape(N_GROUPS, 1, tt), sb, ninf)
    e_iota = lax.broadcasted_iota(jnp.int32, sb.shape, 0) * GROUP_SIZE + r_iota
    idxs, ws = [], []
    for _ in range(TOP_K):
        _, ei = _first_index_of_max(cand, e_iota, N_EXPERTS, (0, 1))
        hit = e_iota == ei
        idxs.append(ei.reshape(1, tt))
        ws.append(jnp.sum(jnp.where(hit, s, 0.0), axis=(0, 1)).reshape(1, tt))
        cand = jnp.where(hit, ninf, cand)
    w = jnp.concatenate(ws, 0)
    eidx_ref[...] = jnp.concatenate(idxs, 0)
    ew_ref[...] = w / jnp.sum(w, 0, keepdims=True) * ROUTED_SCALE


def _router(x2, wr_t_bf, bias_col, tt):
    t = x2.shape[0]
    return pl.pallas_call(
        _router_kernel,
        grid=(t // tt,),
        in_specs=[pl.BlockSpec((tt, 1024), lambda i: (i, 0)),
                  pl.BlockSpec((N_EXPERTS, 1024), lambda i: (0, 0)),
                  pl.BlockSpec((N_EXPERTS, 1), lambda i: (0, 0))],
        out_specs=[pl.BlockSpec((TOP_K, tt), lambda i: (0, i)),
                   pl.BlockSpec((TOP_K, tt), lambda i: (0, i))],
        out_shape=[jax.ShapeDtypeStruct((TOP_K, t), jnp.int32),
                   jax.ShapeDtypeStruct((TOP_K, t), F32)],
        compiler_params=_params("parallel"),
        name="router",
    )(x2, wr_t_bf, bias_col)


IDX_BITS = 18


def _dispatch(eidx, ew, t):
    a = t * TOP_K
    assert a < (1 << IDX_BITS)
    nblk = -(-a // MOE_BLOCK) + N_EXPERTS
    assert nblk % STEP_BLOCKS == 0
    flat_e = eidx.reshape(-1)
    skey = jnp.sort(flat_e * (1 << IDX_BITS) + jnp.arange(a, dtype=jnp.int32))
    si = skey & ((1 << IDX_BITS) - 1)
    experts = jnp.arange(N_EXPERTS, dtype=jnp.int32)
    counts = jnp.sum((flat_e[None, :] == experts[:, None]).astype(jnp.int32), axis=1)
    grp_start = jnp.cumsum(counts) - counts
    padded = (counts + MOE_BLOCK - 1) // MOE_BLOCK * MOE_BLOCK
    pad_end = jnp.cumsum(padded)
    pad_start = pad_end - padded
    blk_first = jnp.arange(nblk, dtype=jnp.int32) * MOE_BLOCK
    blk_exp = jnp.minimum(jnp.sum((pad_end[None, :] <= blk_first[:, None]).astype(jnp.int32), axis=1),
                          N_EXPERTS - 1)
    off = (jnp.arange(nblk * MOE_BLOCK, dtype=jnp.int32).reshape(nblk, MOE_BLOCK)
           - pad_start[blk_exp][:, None])
    valid = off < counts[blk_exp][:, None]
    src = jnp.clip(grp_start[blk_exp][:, None] + off, 0, a - 1)
    row_si = si[src]
    row_tok = jnp.where(valid, row_si >> 3, t).astype(jnp.int32)
    row_w = jnp.where(valid, ew.reshape(-1)[row_si], 0.0)
    nused = (pad_end[-1] // MOE_BLOCK).astype(jnp.int32).reshape(1)
    gidx, nxt = _group_tables(counts, blk_exp)
    step_rows = STEP_BLOCKS * MOE_BLOCK
    return (row_tok.reshape(-1), row_w.reshape(nblk // STEP_BLOCKS, 1, step_rows), blk_exp, nused,
            gidx, nxt)


def _group_tables(counts, blk_exp):
    nonempty = counts > 0
    gidx = (jnp.cumsum(nonempty.astype(jnp.int32)) - 1)[blk_exp]
    experts = jnp.arange(N_EXPERTS, dtype=jnp.int32)
    cand = jnp.where(nonempty, experts, N_EXPERTS)
    later = lax.cummin(cand, axis=0, reverse=True)
    nxt = jnp.concatenate([later[1:], jnp.full((1,), N_EXPERTS, jnp.int32)])
    nxt = jnp.where(nxt >= N_EXPERTS, -1, nxt)
    return gidx.astype(jnp.int32), nxt[blk_exp].astype(jnp.int32)


def _weight_copies(hbm_refs, buf_ref, sem_ref, e, slot):
    return [pltpu.make_async_copy(h.at[e], buf_ref.at[slot, k], sem_ref.at[slot, k])
            for k, h in enumerate(hbm_refs)]


def _stage_weights(b, be_ref, nused_ref, gidx_ref, nxt_ref, hbm_refs, buf_ref, sem_ref, cache_refs):
    first = ((b == 0) | (be_ref[b] != be_ref[jnp.maximum(b - 1, 0)])) & (b < nused_ref[0])

    @pl.when(first)
    def _():
        slot = gidx_ref[b] % 2

        @pl.when(b == 0)
        def _():
            for c in _weight_copies(hbm_refs, buf_ref, sem_ref, be_ref[0], 0):
                c.start()

        for c in _weight_copies(hbm_refs, buf_ref, sem_ref, be_ref[b], slot):
            c.wait()
        nxt = nxt_ref[b]

        @pl.when(nxt >= 0)
        def _():
            for c in _weight_copies(hbm_refs, buf_ref, sem_ref, nxt, 1 - slot):
                c.start()

        for k, cache in enumerate(cache_refs):
            cache[...] = buf_ref[slot, k].astype(BF)


def _up_kernel(be_ref, nused_ref, gidx_ref, nxt_ref, tok_ref, x_ref, wg_hbm, wu_hbm, act_ref,
               xg_ref, wbuf_ref, wgb_ref, wub_ref, sem_ref):
    last = x_ref.shape[0] - 1
    for j in range(STEP_BLOCKS):
        b = pl.program_id(0) * STEP_BLOCKS + j
        rows = slice(j * MOE_BLOCK, (j + 1) * MOE_BLOCK)
        _stage_weights(b, be_ref, nused_ref, gidx_ref, nxt_ref, (wg_hbm, wu_hbm), wbuf_ref, sem_ref,
                       (wgb_ref, wub_ref))

        @pl.when(b < nused_ref[0])
        def _():
            for r in range(MOE_BLOCK):
                tok = jnp.minimum(tok_ref[b * MOE_BLOCK + r], last)
                xg_ref[r:r + 1, :] = x_ref[pl.ds(tok, 1), :]
            x = xg_ref[...].astype(BF)
            g = jnp.dot(x, wgb_ref[...], preferred_element_type=F32)
            u = jnp.dot(x, wub_ref[...], preferred_element_type=F32)
            act_ref[rows, :] = ((g * jax.nn.sigmoid(g)) * u).astype(BF)

        @pl.when(b >= nused_ref[0])
        def _():
            act_ref[rows, :] = jnp.zeros((MOE_BLOCK, D_EXPERT), BF)


def _experts_up(x, row_tok, blk_exp, nused, gidx, nxt, w_gate, w_up):
    nblk = blk_exp.shape[0]
    step_rows = STEP_BLOCKS * MOE_BLOCK
    return pl.pallas_call(
        _up_kernel,
        grid_spec=pltpu.PrefetchScalarGridSpec(
            num_scalar_prefetch=5,
            grid=(nblk // STEP_BLOCKS,),
            in_specs=[pl.BlockSpec(memory_space=pltpu.VMEM),
                      pl.BlockSpec(memory_space=pl.ANY),
                      pl.BlockSpec(memory_space=pl.ANY)],
            out_specs=pl.BlockSpec((step_rows, D_EXPERT), lambda i, *_: (i, 0)),
            scratch_shapes=[pltpu.VMEM((MOE_BLOCK, D_MODEL), F32),
                            pltpu.VMEM((2, 2, D_MODEL, D_EXPERT), F32),
                            pltpu.VMEM((D_MODEL, D_EXPERT), BF),
                            pltpu.VMEM((D_MODEL, D_EXPERT), BF),
                            pltpu.SemaphoreType.DMA((2, 2))],
        ),
        out_shape=jax.ShapeDtypeStruct((nblk * MOE_BLOCK, D_EXPERT), BF),
        compiler_params=_params("arbitrary"),
        name="experts_up",
    )(blk_exp, nused, gidx, nxt, row_tok, x, w_gate, w_up)


SCATTER_GROUP = 8


def _row_to_column(row):
    n = row.shape[1]
    eye = lax.broadcasted_iota(jnp.int32, (n, n), 0) == lax.broadcasted_iota(jnp.int32, (n, n), 1)
    return jnp.sum(jnp.where(eye, jnp.broadcast_to(row, (n, n)), 0.0), axis=1, keepdims=True)


def _down_kernel(be_ref, nused_ref, gidx_ref, nxt_ref, tok_ref, act_ref, rw_ref, wd_hbm, y_ref,
                 yb_ref, wbuf_ref, wdb_ref, sem_ref):
    @pl.when(pl.program_id(0) == 0)
    def _():
        y_ref[...] = jnp.zeros_like(y_ref)

    for j in range(STEP_BLOCKS):
        b = pl.program_id(0) * STEP_BLOCKS + j
        rows = slice(j * MOE_BLOCK, (j + 1) * MOE_BLOCK)
        _stage_weights(b, be_ref, nused_ref, gidx_ref, nxt_ref, (wd_hbm,), wbuf_ref, sem_ref,
                       (wdb_ref,))

        @pl.when(b < nused_ref[0])
        def _():
            yb_ref[...] = jnp.dot(act_ref[rows, :], wdb_ref[...],
                                  preferred_element_type=F32) * _row_to_column(rw_ref[0, :, rows])
            for r0 in range(0, MOE_BLOCK, SCATTER_GROUP):
                rs = range(r0, r0 + SCATTER_GROUP)
                toks = [tok_ref[b * MOE_BLOCK + r] for r in rs]
                new = [y_ref[pl.ds(t, 1), :] + yb_ref[r:r + 1, :] for r, t in zip(rs, toks)]
                for t, v in zip(toks, new):
                    y_ref[pl.ds(t, 1), :] = v


def _experts_down(act, row_tok, row_w, blk_exp, nused, gidx, nxt, w_down, t_rows):
    nblk = blk_exp.shape[0]
    step_rows = STEP_BLOCKS * MOE_BLOCK
    return pl.pallas_call(
        _down_kernel,
        grid_spec=pltpu.PrefetchScalarGridSpec(
            num_scalar_prefetch=5,
            grid=(nblk // STEP_BLOCKS,),
            in_specs=[pl.BlockSpec((step_rows, D_EXPERT), lambda i, *_: (i, 0)),
                      pl.BlockSpec((1, 1, step_rows), lambda i, *_: (i, 0, 0)),
                      pl.BlockSpec(memory_space=pl.ANY)],
            out_specs=pl.BlockSpec(memory_space=pltpu.VMEM),
            scratch_shapes=[pltpu.VMEM((MOE_BLOCK, D_MODEL), F32),
                            pltpu.VMEM((2, 1, D_EXPERT, D_MODEL), F32),
                            pltpu.VMEM((D_EXPERT, D_MODEL), BF),
                            pltpu.SemaphoreType.DMA((2, 1))],
        ),
        out_shape=jax.ShapeDtypeStruct((t_rows, D_MODEL), F32),
        compiler_params=_params("arbitrary"),
        name="experts_down",
    )(blk_exp, nused, gidx, nxt, row_tok, act, row_w, w_down)


def _final_kernel(h_ref, yr_ref, wg_ref, wu_ref, wd_ref, g_ref, b_ref, o_ref, *, alpha):
    h = h_ref[...]
    hb = h.astype(BF)
    g = jnp.dot(hb, wg_ref[...], preferred_element_type=F32)
    u = jnp.dot(hb, wu_ref[...], preferred_element_type=F32)
    shared = jnp.dot(((g * jax.nn.sigmoid(g)) * u).astype(BF), wd_ref[...],
                     preferred_element_type=F32)
    f = yr_ref[...] + shared
    o_ref[...] = _layer_norm(alpha * h + f, g_ref[...], b_ref[...])


def _final(h2, yr, wg_bf, wu_bf, wd_bf, g, b, alpha, tm):
    t = h2.shape[0]
    tile = pl.BlockSpec((tm, 1024), lambda i: (i, 0))
    full = lambda shape: pl.BlockSpec(shape, lambda i: (0, 0))
    return pl.pallas_call(
        functools.partial(_final_kernel, alpha=alpha),
        grid=(t // tm,),
        in_specs=[tile, tile, full((1024, 256)), full((1024, 256)), full((256, 1024)),
                  full((1, 1024)), full((1, 1024))],
        out_specs=tile,
        out_shape=jax.ShapeDtypeStruct((t, 1024), F32),
        compiler_params=_params("parallel"),
        name="shared_ln2",
    )(h2, yr, wg_bf, wu_bf, wd_bf, g, b)


def _moe_ln2(h2, wr_t_bf, bias_col, w_gate, w_up, w_down, wsg_bf, wsu_bf, wsd_bf, g, b, alpha):
    t = h2.shape[0]
    eidx_t, ew_t = _router(h2, wr_t_bf, bias_col, ROUTER_TILE)
    row_tok, row_w, blk_exp, nused, gidx, nxt = _dispatch(eidx_t.T, ew_t.T, t)
    act = _experts_up(h2, row_tok, blk_exp, nused, gidx, nxt, w_gate, w_up)
    yr = _experts_down(act, row_tok, row_w, blk_exp, nused, gidx, nxt, w_down, t + 8)
    return _final(h2, yr, wsg_bf, wsu_bf, wsd_bf, g, b, alpha, 256)


def _permute_w_in(w_in):
    rq, rk, rv, rg, sq, sk, sv, mq, gl = jnp.split(
        w_in, [512, 1024, 2048, 3072, 4096, 4352, 4608, 5632], axis=-1)
    return jnp.concatenate([rq, rk, rv, rg, sq, mq, gl, sk, sv], -1)


def kernel(x_prompt, x_sample, mem_prompt, cache_ret_state, cache_swa_k, cache_swa_v, cache_mem_k,
           cache_mem_v, w_in, swa_sinks, w_mem_kv, w_branch, w_o, ln1_g, ln1_b, w_router,
           router_bias, w_gate_e, w_up_e, w_down_e, w_sh_gate, w_sh_up, w_sh_down, ln2_g, ln2_b):
    depth = w_in.shape[0]
    assert depth == 1
    alpha = (2.0 * depth) ** 0.25
    bp, lp, d = x_prompt.shape
    bs, ls, _ = x_sample.shape
    l = 0

    w_in_bf = _permute_w_in(w_in[l]).astype(BF)
    sinks = swa_sinks[l]
    wb_bf = w_branch[l].astype(BF)
    wo_bf = w_o[l].astype(BF)
    g1, b1 = ln1_g[l].reshape(1, d), ln1_b[l].reshape(1, d)
    g2, b2 = ln2_g[l].reshape(1, d), ln2_b[l].reshape(1, d)

    tp = bp * lp
    xp2 = x_prompt.reshape(tp, d)
    hp2 = _proj(xp2, w_in_bf, 1024, 512)
    hp3 = hp2.reshape(bp, lp, D_IN)
    mkv = _proj(mem_prompt.reshape(bp * N_MEM, d), w_mem_kv[l].astype(BF), N_MEM, 512)
    mk_p, mv_p = mkv[:, :1024].reshape(bp, N_MEM, 1024), mkv[:, 1024:].reshape(bp, N_MEM, 1024)
    rs0 = jnp.zeros((bp, RET_HEADS, RET_DK, RET_DV), F32)
    ro_p, rs_p = _retention(hp3, jnp.arange(lp), rs0, RET_CHUNK, RET_CHUNK)
    so_p = _swa_prompt(hp2, sinks)
    mo_p = _mem_attend(hp3, mk_p, mv_p, 256)
    hmid_p = _merge(ro_p.reshape(tp, d), so_p, mo_p.reshape(tp, d), hp2, xp2, wb_bf, wo_bf,
                    g1, b1, alpha, 256)

    ts = bs * SAMPLE_PAD
    xs3 = jnp.pad(x_sample, ((0, 0), (0, SAMPLE_PAD - ls), (0, 0)))
    xs2 = xs3.reshape(ts, d)
    hs2 = _proj(xs2, w_in_bf, ts, 512)
    hs3 = hs2.reshape(bs, SAMPLE_PAD, D_IN)
    pos_s = PAST_LEN + jnp.arange(SAMPLE_PAD)
    ro_s, rs_s = _retention(hs3, pos_s, cache_ret_state.reshape(bs, RET_HEADS, RET_DK, RET_DV), ls,
                            SAMPLE_PAD)
    w_buf = cache_swa_k.shape[2]
    prev_k = cache_swa_k.reshape(bs, w_buf, SWA_KV_HEADS * SWA_DH)
    prev_v = cache_swa_v.reshape(bs, w_buf, SWA_KV_HEADS * SWA_DH)
    so_s = _swa_sample(hs3, prev_k, prev_v, sinks, ls)
    mo_s = _mem_attend(hs3, cache_mem_k.reshape(bs, N_MEM, 1024),
                       cache_mem_v.reshape(bs, N_MEM, 1024), SAMPLE_PAD)
    hmid_s = _merge(ro_s.reshape(ts, d), so_s.reshape(ts, d), mo_s.reshape(ts, d), hs2, xs2,
                    wb_bf, wo_bf, g1, b1, alpha, 256)
    hmid_s = hmid_s.reshape(bs, SAMPLE_PAD, d)[:, :ls].reshape(bs * ls, d)

    t_all = tp + bs * ls
    assert t_all % (2 * ROUTER_TILE) == 0 and t_all // 2 <= tp
    th = t_all // 2
    moe = functools.partial(
        _moe_ln2, wr_t_bf=w_router[l].T.astype(BF), bias_col=router_bias[l].reshape(N_EXPERTS, 1),
        w_gate=w_gate_e[l], w_up=w_up_e[l], w_down=w_down_e[l], wsg_bf=w_sh_gate[l].astype(BF),
        wsu_bf=w_sh_up[l].astype(BF), wsd_bf=w_sh_down[l].astype(BF), g=g2, b=b2, alpha=alpha)
    y_a = moe(hmid_p[:th])
    y_b = moe(jnp.concatenate([hmid_p[th:], hmid_s], 0))

    y_p = jnp.concatenate([y_a, y_b[:tp - th]], 0).reshape(bp, lp, d)
    y_s = y_b[tp - th:].reshape(bs, ls, d)
    kv4 = lambda a, n: a.reshape(1, a.shape[0], n, SWA_KV_HEADS, SWA_DH)
    k_p = kv4(hp3[:, lp - WINDOW:, COL_SK:COL_SK + 256], WINDOW)
    v_p = kv4(hp3[:, lp - WINDOW:, COL_SV:COL_SV + 256], WINDOW)
    mem4 = lambda a: a.reshape(1, bp, N_MEM, MEM_HEADS, MEM_DH)
    k_s = kv4(jnp.concatenate([prev_k, hs3[:, :ls, COL_SK:COL_SK + 256]], 1)[:, -w_buf:], w_buf)
    v_s = kv4(jnp.concatenate([prev_v, hs3[:, :ls, COL_SV:COL_SV + 256]], 1)[:, -w_buf:], w_buf)
    return (y_p, y_s, rs_p[None], k_p, v_p, mem4(mk_p), mem4(mv_p), rs_s[None], k_s, v_s)
```

```python
import functools

import jax
import jax.numpy as jnp
from jax import lax
from jax.experimental import pallas as pl
from jax.experimental.pallas import tpu as pltpu

BF = jnp.bfloat16
F32 = jnp.float32

D_MODEL = 1024
RET_HEADS = 4
RET_DK = 128
RET_DV = 256
RET_CHUNK = 128
ROPE_BASE = 10000.0
SWA_HEADS = 16
SWA_KV_HEADS = 4
SWA_GROUP = SWA_HEADS // SWA_KV_HEADS
SWA_DH = 64
WINDOW = 128
SWA_BLOCK = 128
N_MEM = 256
MEM_HEADS = 4
MEM_DH = 256
N_BRANCH = 3
N_EXPERTS = 256
TOP_K = 8
N_GROUPS = 8
GROUP_SIZE = N_EXPERTS // N_GROUPS
TOPK_GROUPS = 4
D_EXPERT = 256
ROUTED_SCALE = 2.5
MOE_BLOCK = 128
ROUTER_TILE = 384
ROW_SUB = D_MODEL // 128
STEP_BLOCKS = 4
LN_EPS = 1e-5
NORM_EPS = 1e-6
NEG = -1e30
PAST_LEN = 16384
SAMPLE_PAD = 8

COL_RQ, COL_RK, COL_RV, COL_RG, COL_SQ, COL_MQ, COL_GL, COL_SK, COL_SV = (
    0, 512, 1024, 2048, 3072, 4096, 5120, 8192, 8448)
D_IN = 8704

VMEM_LIMIT = 56 * 1024 * 1024


def _params(*sem):
    return pltpu.CompilerParams(dimension_semantics=sem, vmem_limit_bytes=VMEM_LIMIT)


def _bdot(a, b):
    return jnp.dot(a.astype(BF), b.astype(BF), preferred_element_type=F32)


def _bdot_nt(a, b):
    return lax.dot_general(a.astype(BF), b.astype(BF), (((1,), (1,)), ((), ())),
                           preferred_element_type=F32)


def _bdot_tn(a, b):
    return lax.dot_general(a.astype(BF), b.astype(BF), (((0,), (0,)), ((), ())),
                           preferred_element_type=F32)


def _layer_norm(z, g, b):
    zc = z - jnp.mean(z, -1, keepdims=True)
    var = jnp.mean(zc * zc, -1, keepdims=True)
    return zc * lax.rsqrt(var + LN_EPS) * g + b


def _proj_kernel(x_ref, w_ref, o_ref, xb_ref):
    @pl.when(pl.program_id(1) == 0)
    def _():
        xb_ref[...] = x_ref[...].astype(BF)

    o_ref[...] = jnp.dot(xb_ref[...], w_ref[...], preferred_element_type=F32)


def _proj(x, w_bf, tm, tn):
    m, k = x.shape
    n = w_bf.shape[1]
    return pl.pallas_call(
        _proj_kernel,
        grid=(m // tm, n // tn),
        in_specs=[pl.BlockSpec((tm, k), lambda i, j: (i, 0)),
                  pl.BlockSpec((k, tn), lambda i, j: (0, j))],
        out_specs=pl.BlockSpec((tm, tn), lambda i, j: (i, j)),
        out_shape=jax.ShapeDtypeStruct((m, n), F32),
        scratch_shapes=[pltpu.VMEM((tm, k), BF)],
        compiler_params=_params("parallel", "arbitrary"),
        name="proj",
    )(x, w_bf)


def _ret_tables(c_real, c_pad):
    lg = jnp.log1p(-jnp.exp2(-5.0 - jnp.arange(RET_HEADS, dtype=F32)))
    idx = jnp.arange(c_pad, dtype=F32)
    real = idx < c_real
    rel = idx[:, None] - idx[None, :]
    intra = jnp.where(rel >= 0, jnp.exp(lg[:, None, None] * jnp.maximum(rel, 0.0)), 0.0)
    intra = jnp.where(real[None, :, None] & real[None, None, :], intra, 0.0)
    q_dec = jnp.where(real[None, :], jnp.exp((idx[None, :] + 1.0) * lg[:, None]), 0.0)
    k_dec = jnp.where(real[None, :], jnp.exp((c_real - 1.0 - idx)[None, :] * lg[:, None]), 0.0)
    c_dec = jnp.exp(c_real * lg)
    bc = lambda t: jnp.broadcast_to(t[:, :, None], (RET_HEADS, c_pad, RET_DK))
    return intra, bc(q_dec), bc(k_dec), c_dec


def _rope_tables(pos):
    half = RET_DK // 2
    inv_freq = 1.0 / (ROPE_BASE ** (jnp.arange(half, dtype=F32) / half))
    ang = pos.astype(F32)[:, None] * inv_freq[None, :]
    cos, sin = jnp.cos(ang), jnp.sin(ang)
    return jnp.concatenate([cos, cos], -1), jnp.concatenate([-sin, sin], -1)


def _ret_kernel(cdec_ref, rq_ref, rk_ref, rv_ref, rg_ref, cos_ref, sin_ref, intra_ref, qdec_ref,
                kdec_ref, s0_ref, o_ref, s_out_ref, s_scr, *, n_chunks):
    c = pl.program_id(1)

    @pl.when(c == 0)
    def _():
        s_scr[...] = s0_ref[0]

    cos2 = cos_ref[...]
    sin2 = sin_ref[...]

    def rot(x):
        return x * cos2 + pltpu.roll(x, RET_DK // 2, 1) * sin2

    for h in range(RET_HEADS):
        q = rot(rq_ref[0, :, h * RET_DK:(h + 1) * RET_DK])
        k = rot(rk_ref[0, :, h * RET_DK:(h + 1) * RET_DK]) * (RET_DK ** -0.5)
        v = rv_ref[0, :, h * RET_DV:(h + 1) * RET_DV].astype(BF)
        s_old = s_scr[h]
        a = _bdot_nt(q, k) * intra_ref[h]
        o = _bdot(a, v) + _bdot(q * qdec_ref[h], s_old)
        s_scr[h] = s_old * cdec_ref[h] + _bdot_tn(k * kdec_ref[h], v)
        o = o * lax.rsqrt(jnp.mean(o * o, -1, keepdims=True) + NORM_EPS)
        g = rg_ref[0, :, h * RET_DV:(h + 1) * RET_DV]
        o_ref[0, :, h * RET_DV:(h + 1) * RET_DV] = o * (g * jax.nn.sigmoid(g))

    @pl.when(c == n_chunks - 1)
    def _():
        s_out_ref[0] = s_scr[...]


def _retention(h3, pos, state0, c_real, c_pad):
    b, l, _ = h3.shape
    n_chunks = l // c_pad
    intra, qdec, kdec, cdec = _ret_tables(c_real, c_pad)
    cos2, sin2 = _rope_tables(pos)
    full3 = lambda shape: pl.BlockSpec(shape, lambda i, c: (0, 0, 0))
    return pl.pallas_call(
        functools.partial(_ret_kernel, n_chunks=n_chunks),
        grid=(b, n_chunks),
        in_specs=[
            pl.BlockSpec(memory_space=pltpu.SMEM),
            pl.BlockSpec((1, c_pad, 512), lambda i, c: (i, c, COL_RQ // 512)),
            pl.BlockSpec((1, c_pad, 512), lambda i, c: (i, c, COL_RK // 512)),
            pl.BlockSpec((1, c_pad, 1024), lambda i, c: (i, c, COL_RV // 1024)),
            pl.BlockSpec((1, c_pad, 1024), lambda i, c: (i, c, COL_RG // 1024)),
            pl.BlockSpec((c_pad, RET_DK), lambda i, c: (c, 0)),
            pl.BlockSpec((c_pad, RET_DK), lambda i, c: (c, 0)),
            full3((RET_HEADS, c_pad, c_pad)),
            full3((RET_HEADS, c_pad, RET_DK)),
            full3((RET_HEADS, c_pad, RET_DK)),
            pl.BlockSpec((1, RET_HEADS, RET_DK, RET_DV), lambda i, c: (i, 0, 0, 0)),
        ],
        out_specs=[
            pl.BlockSpec((1, c_pad, 1024), lambda i, c: (i, c, 0)),
            pl.BlockSpec((1, RET_HEADS, RET_DK, RET_DV), lambda i, c: (i, 0, 0, 0)),
        ],
        out_shape=[jax.ShapeDtypeStruct((b, l, 1024), F32),
                   jax.ShapeDtypeStruct((b, RET_HEADS, RET_DK, RET_DV), F32)],
        scratch_shapes=[pltpu.VMEM((RET_HEADS, RET_DK, RET_DV), F32)],
        compiler_params=_params("parallel", "arbitrary"),
        name="retention",
    )(cdec, h3, h3, h3, h3, cos2, sin2, intra, qdec, kdec, state0)


def _sink_softmax(s, sink):
    m = jnp.maximum(jnp.max(s, -1, keepdims=True), sink)
    p = jnp.exp(s - m)
    return p / (jnp.sum(p, -1, keepdims=True) + jnp.exp(sink - m))


def _swa_prompt_kernel(sinks_ref, q_ref, kp_ref, kc_ref, vp_ref, vc_ref, o_ref):
    n = pl.program_id(0)
    qi = lax.broadcasted_iota(jnp.int32, (SWA_BLOCK, 2 * SWA_BLOCK), 0)
    kj = lax.broadcasted_iota(jnp.int32, (SWA_BLOCK, 2 * SWA_BLOCK), 1)
    rel = SWA_BLOCK + qi - kj
    valid = (rel >= 0) & (rel <= WINDOW) & ((kj >= SWA_BLOCK) | (n > 0))
    for h in range(SWA_KV_HEADS):
        sl = slice(h * SWA_DH, (h + 1) * SWA_DH)
        k2 = jnp.concatenate([kp_ref[:, sl], kc_ref[:, sl]], 0).astype(BF)
        v2 = jnp.concatenate([vp_ref[:, sl], vc_ref[:, sl]], 0).astype(BF)
        for g in range(SWA_GROUP):
            hq = h * SWA_GROUP + g
            qsl = slice(hq * SWA_DH, (hq + 1) * SWA_DH)
            s = _bdot_nt(q_ref[:, qsl], k2) * (SWA_DH ** -0.5)
            s = jnp.where(valid, s, NEG)
            p = _sink_softmax(s, sinks_ref[hq])
            o_ref[:, qsl] = _bdot(p, v2)


def _swa_prompt(h2, sinks):
    t = h2.shape[0]
    nb = t // SWA_BLOCK
    prev = lambda col: (lambda n: (jnp.maximum(n - 1, 0), col))
    cur = lambda col: (lambda n: (n, col))
    ck, cv = COL_SK // 256, COL_SV // 256
    return pl.pallas_call(
        _swa_prompt_kernel,
        grid=(nb,),
        in_specs=[
            pl.BlockSpec(memory_space=pltpu.SMEM),
            pl.BlockSpec((SWA_BLOCK, 1024), cur(COL_SQ // 1024)),
            pl.BlockSpec((SWA_BLOCK, 256), prev(ck)),
            pl.BlockSpec((SWA_BLOCK, 256), cur(ck)),
            pl.BlockSpec((SWA_BLOCK, 256), prev(cv)),
            pl.BlockSpec((SWA_BLOCK, 256), cur(cv)),
        ],
        out_specs=pl.BlockSpec((SWA_BLOCK, 1024), lambda n: (n, 0)),
        out_shape=jax.ShapeDtypeStruct((t, 1024), F32),
        compiler_params=_params("parallel"),
        name="swa_prompt",
    )(sinks, h2, h2, h2, h2, h2)


def _swa_sample_kernel(sinks_ref, q_ref, kn_ref, vn_ref, kp_ref, vp_ref, o_ref, *, n_new):
    tb = q_ref.shape[0]
    w = kp_ref.shape[1]
    p_ = SAMPLE_PAD
    qi = lax.broadcasted_iota(jnp.int32, (tb, p_, w), 1)
    kj = lax.broadcasted_iota(jnp.int32, (tb, p_, w), 2)
    rel_prev = w + qi - kj
    valid_prev = (rel_prev >= 0) & (rel_prev <= WINDOW)
    qn = lax.broadcasted_iota(jnp.int32, (tb, p_, p_), 1)
    kn = lax.broadcasted_iota(jnp.int32, (tb, p_, p_), 2)
    valid_new = (qn - kn >= 0) & (qn - kn <= WINDOW) & (kn < n_new)
    bdot = lambda eq, a, b: jnp.einsum(eq, a.astype(BF), b.astype(BF), preferred_element_type=F32)
    for h in range(SWA_KV_HEADS):
        sl = slice(h * SWA_DH, (h + 1) * SWA_DH)
        kp, vp = kp_ref[:, :, sl], vp_ref[:, :, sl]
        kn_h, vn_h = kn_ref[:, :, sl], vn_ref[:, :, sl]
        for g in range(SWA_GROUP):
            hq = h * SWA_GROUP + g
            qsl = slice(hq * SWA_DH, (hq + 1) * SWA_DH)
            q = q_ref[:, :, qsl]
            sp = bdot('bqd,bkd->bqk', q, kp) * (SWA_DH ** -0.5)
            sn = bdot('bqd,bkd->bqk', q, kn_h) * (SWA_DH ** -0.5)
            sp = jnp.where(valid_prev, sp, NEG)
            sn = jnp.where(valid_new, sn, NEG)
            sink = sinks_ref[hq]
            m = jnp.maximum(jnp.maximum(jnp.max(sp, -1, keepdims=True),
                                        jnp.max(sn, -1, keepdims=True)), sink)
            pp = jnp.exp(sp - m)
            pn = jnp.exp(sn - m)
            den = jnp.sum(pp, -1, keepdims=True) + jnp.sum(pn, -1, keepdims=True) + jnp.exp(sink - m)
            o = bdot('bqk,bkd->bqd', pp / den, vp) + bdot('bqk,bkd->bqd', pn / den, vn_h)
            o_ref[:, :, qsl] = o


def _swa_sample(h3, prev_k, prev_v, sinks, n_new, tb=8):
    b = h3.shape[0]
    w = prev_k.shape[1]
    return pl.pallas_call(
        functools.partial(_swa_sample_kernel, n_new=n_new),
        grid=(b // tb,),
        in_specs=[
            pl.BlockSpec(memory_space=pltpu.SMEM),
            pl.BlockSpec((tb, SAMPLE_PAD, 1024), lambda i: (i, 0, COL_SQ // 1024)),
            pl.BlockSpec((tb, SAMPLE_PAD, 256), lambda i: (i, 0, COL_SK // 256)),
            pl.BlockSpec((tb, SAMPLE_PAD, 256), lambda i: (i, 0, COL_SV // 256)),
            pl.BlockSpec((tb, w, 256), lambda i: (i, 0, 0)),
            pl.BlockSpec((tb, w, 256), lambda i: (i, 0, 0)),
        ],
        out_specs=pl.BlockSpec((tb, SAMPLE_PAD, 1024), lambda i: (i, 0, 0)),
        out_shape=jax.ShapeDtypeStruct((b, SAMPLE_PAD, 1024), F32),
        compiler_params=_params("parallel"),
        name="swa_sample",
    )(sinks, h3, h3, h3, prev_k, prev_v)


def _mem_kernel(q_ref, mk_ref, mv_ref, o_ref):
    for h in range(MEM_HEADS):
        sl = slice(h * MEM_DH, (h + 1) * MEM_DH)
        s = _bdot_nt(q_ref[0, :, sl], mk_ref[0, :, sl]) * (MEM_DH ** -0.5)
        m = jnp.max(s, -1, keepdims=True)
        e = jnp.exp(s - m)
        p = e / jnp.sum(e, -1, keepdims=True)
        o_ref[0, :, sl] = _bdot(p, mv_ref[0, :, sl])


def _mem_attend(h3, mk, mv, tl):
    b, l, _ = h3.shape
    return pl.pallas_call(
        _mem_kernel,
        grid=(b, l // tl),
        in_specs=[
            pl.BlockSpec((1, tl, 1024), lambda i, j: (i, j, COL_MQ // 1024)),
            pl.BlockSpec((1, N_MEM, 1024), lambda i, j: (i, 0, 0)),
            pl.BlockSpec((1, N_MEM, 1024), lambda i, j: (i, 0, 0)),
        ],
        out_specs=pl.BlockSpec((1, tl, 1024), lambda i, j: (i, j, 0)),
        out_shape=jax.ShapeDtypeStruct((b, l, 1024), F32),
        compiler_params=_params("parallel", "parallel"),
        name="mem_attend",
    )(h3, mk, mv)


def _merge_kernel(ro_ref, so_ref, mo_ref, g0_ref, g1_ref, g2_ref, x_ref, wb_ref, wo_ref, g_ref,
                  b_ref, o_ref, *, alpha):
    acc = None
    for n, (br, gl) in enumerate(((ro_ref, g0_ref), (so_ref, g1_ref), (mo_ref, g2_ref))):
        term = jax.nn.sigmoid(gl[...]) * jnp.dot(br[...].astype(BF), wb_ref[n],
                                                 preferred_element_type=F32)
        acc = term if acc is None else acc + term
    a = jnp.dot(acc.astype(BF), wo_ref[...], preferred_element_type=F32)
    o_ref[...] = _layer_norm(alpha * x_ref[...] + a, g_ref[...], b_ref[...])


def _merge(ro, so, mo, h2, x2, wb_bf, wo_bf, g, b, alpha, tm):
    t = x2.shape[0]
    tile = lambda col: pl.BlockSpec((tm, 1024), lambda i: (i, col))
    gl0 = COL_GL // 1024
    return pl.pallas_call(
        functools.partial(_merge_kernel, alpha=alpha),
        grid=(t // tm,),
        in_specs=[tile(0), tile(0), tile(0), tile(gl0), tile(gl0 + 1), tile(gl0 + 2), tile(0),
                  pl.BlockSpec((N_BRANCH, 1024, 1024), lambda i: (0, 0, 0)),
                  pl.BlockSpec((1024, 1024), lambda i: (0, 0)),
                  pl.BlockSpec((1, 1024), lambda i: (0, 0)),
                  pl.BlockSpec((1, 1024), lambda i: (0, 0))],
        out_specs=tile(0),
        out_shape=jax.ShapeDtypeStruct((t, 1024), F32),
        compiler_params=_params("parallel"),
        name="merge_ln1",
    )(ro, so, mo, h2, h2, h2, x2, wb_bf, wo_bf, g, b)


def _first_index_of_max(v, iota, big, axes):
    m = jnp.max(v, axis=axes, keepdims=True)
    idx = jnp.min(jnp.where(v == m, iota, big), axis=axes, keepdims=True)
    return m, idx


def _router_kernel(x_ref, wr_ref, bias_ref, eidx_ref, ew_ref):
    tt = x_ref.shape[0]
    logits = lax.dot_general(wr_ref[...], x_ref[...].astype(BF), (((1,), (1,)), ((), ())),
                             preferred_element_type=F32)
    s = jax.nn.sigmoid(logits).reshape(N_GROUPS, GROUP_SIZE, tt)
    sb = s + bias_ref[...].reshape(N_GROUPS, GROUP_SIZE, 1)
    ninf = -jnp.inf
    r_iota = lax.broadcasted_iota(jnp.int32, sb.shape, 1)
    m1, i1 = _first_index_of_max(sb, r_iota, GROUP_SIZE, 1)
    m2 = jnp.max(jnp.where(r_iota == i1, ninf, sb), axis=1, keepdims=True)
    gsc = (m1 + m2).reshape(N_GROUPS, tt)
    g_iota = lax.broadcasted_iota(jnp.int32, gsc.shape, 0)
    gmask = jnp.zeros(gsc.shape, jnp.bool_)
    for _ in range(TOPK_GROUPS):
        _, gi = _first_index_of_max(gsc, g_iota, N_GROUPS, 0)
        hit = g_iota == gi
        gmask = gmask | hit
        gsc = jnp.where(hit, ninf, gsc)
    cand = jnp.where(gmask.reshape(N_GROUPS, 1, tt), sb, ninf)
    e_iota = lax.broadcasted_iota(jnp.int32, sb.shape, 0) * GROUP_SIZE + r_iota
    idxs, ws = [], []
    for _ in range(TOP_K):
        _, ei = _first_index_of_max(cand, e_iota, N_EXPERTS, (0, 1))
        hit = e_iota == ei
        idxs.append(ei.reshape(1, tt))
        ws.append(jnp.sum(jnp.where(hit, s, 0.0), axis=(0, 1)).reshape(1, tt))
        cand = jnp.where(hit, ninf, cand)
    w = jnp.concatenate(ws, 0)
    eidx_ref[...] = jnp.concatenate(idxs, 0)
    ew_ref[...] = w / jnp.sum(w, 0, keepdims=True) * ROUTED_SCALE


def _router(x2, wr_t_bf, bias_col, tt):
    t = x2.shape[0]
    return pl.pallas_call(
        _router_kernel,
        grid=(t // tt,),
        in_specs=[pl.BlockSpec((tt, 1024), lambda i: (i, 0)),
                  pl.BlockSpec((N_EXPERTS, 1024), lambda i: (0, 0)),
                  pl.BlockSpec((N_EXPERTS, 1), lambda i: (0, 0))],
        out_specs=[pl.BlockSpec((TOP_K, tt), lambda i: (0, i)),
                   pl.BlockSpec((TOP_K, tt), lambda i: (0, i))],
        out_shape=[jax.ShapeDtypeStruct((TOP_K, t), jnp.int32),
                   jax.ShapeDtypeStruct((TOP_K, t), F32)],
        compiler_params=_params("parallel"),
        name="router",
    )(x2, wr_t_bf, bias_col)


IDX_BITS = 18


def _dispatch(eidx, ew, t):
    a = t * TOP_K
    assert a < (1 << IDX_BITS)
    nblk = -(-a // MOE_BLOCK) + N_EXPERTS
    assert nblk % STEP_BLOCKS == 0 and STEP_BLOCKS % 2 == 0
    flat_e = eidx.reshape(-1)
    skey = jnp.sort(flat_e * (1 << IDX_BITS) + jnp.arange(a, dtype=jnp.int32))
    si = skey & ((1 << IDX_BITS) - 1)
    experts = jnp.arange(N_EXPERTS, dtype=jnp.int32)
    counts = jnp.sum((flat_e[None, :] == experts[:, None]).astype(jnp.int32), axis=1)
    grp_start = jnp.cumsum(counts) - counts
    padded = (counts + MOE_BLOCK - 1) // MOE_BLOCK * MOE_BLOCK
    pad_end = jnp.cumsum(padded)
    pad_start = pad_end - padded
    blk_first = jnp.arange(nblk, dtype=jnp.int32) * MOE_BLOCK
    blk_exp = jnp.minimum(jnp.sum((pad_end[None, :] <= blk_first[:, None]).astype(jnp.int32), axis=1),
                          N_EXPERTS - 1)
    off = (jnp.arange(nblk * MOE_BLOCK, dtype=jnp.int32).reshape(nblk, MOE_BLOCK)
           - pad_start[blk_exp][:, None])
    valid = off < counts[blk_exp][:, None]
    src = jnp.clip(grp_start[blk_exp][:, None] + off, 0, a - 1)
    row_si = si[src]
    row_tok = jnp.where(valid, row_si >> 3, t).astype(jnp.int32)
    row_w = jnp.where(valid, ew.reshape(-1)[row_si], 0.0)
    nused = (pad_end[-1] // MOE_BLOCK).astype(jnp.int32).reshape(1)
    gidx, nxt = _group_tables(counts, blk_exp)
    step_rows = STEP_BLOCKS * MOE_BLOCK
    return (row_tok.reshape(-1), row_w.reshape(nblk // STEP_BLOCKS, 1, step_rows), blk_exp, nused,
            gidx, nxt)


def _group_tables(counts, blk_exp):
    nonempty = counts > 0
    gidx = (jnp.cumsum(nonempty.astype(jnp.int32)) - 1)[blk_exp]
    experts = jnp.arange(N_EXPERTS, dtype=jnp.int32)
    cand = jnp.where(nonempty, experts, N_EXPERTS)
    later = lax.cummin(cand, axis=0, reverse=True)
    nxt = jnp.concatenate([later[1:], jnp.full((1,), N_EXPERTS, jnp.int32)])
    nxt = jnp.where(nxt >= N_EXPERTS, -1, nxt)
    return gidx.astype(jnp.int32), nxt[blk_exp].astype(jnp.int32)


def _weight_copies(hbm_refs, buf_ref, sem_ref, e, slot):
    return [pltpu.make_async_copy(h.at[e], buf_ref.at[slot, k], sem_ref.at[slot, k])
            for k, h in enumerate(hbm_refs)]


def _stage_weights(b, be_ref, nused_ref, gidx_ref, nxt_ref, hbm_refs, buf_ref, sem_ref, cache_refs):
    first = ((b == 0) | (be_ref[b] != be_ref[jnp.maximum(b - 1, 0)])) & (b < nused_ref[0])

    @pl.when(first)
    def _():
        slot = gidx_ref[b] % 2

        @pl.when(b == 0)
        def _():
            for c in _weight_copies(hbm_refs, buf_ref, sem_ref, be_ref[0], 0):
                c.start()

        for c in _weight_copies(hbm_refs, buf_ref, sem_ref, be_ref[b], slot):
            c.wait()
        nxt = nxt_ref[b]

        @pl.when(nxt >= 0)
        def _():
            for c in _weight_copies(hbm_refs, buf_ref, sem_ref, nxt, 1 - slot):
                c.start()

        for k, cache in enumerate(cache_refs):
            cache[...] = buf_ref[slot, k].astype(BF)


def _load_tiled_rows(ref, idx, m):
    return jnp.concatenate([ref[(*idx, pl.ds(s, m, stride=ROW_SUB), slice(None))]
                            for s in range(ROW_SUB)], axis=1)


def _store_tiled_rows(ref, idx, val):
    m = val.shape[0]
    for s in range(ROW_SUB):
        ref[(*idx, pl.ds(s, m, stride=ROW_SUB), slice(None))] = val[:, s * 128:(s + 1) * 128]


def _gather_rows(tok_ref, b, x_ref, xg_ref, slot):
    for r in range(MOE_BLOCK):
        xg_ref[slot, r * ROW_SUB:(r + 1) * ROW_SUB, :] = x_ref[tok_ref[b * MOE_BLOCK + r]]


def _up_kernel(be_ref, nused_ref, gidx_ref, nxt_ref, tok_ref, x_ref, wg_hbm, wu_hbm, act_ref,
               xg_ref, wbuf_ref, wgb_ref, wub_ref, sem_ref):
    i = pl.program_id(0)
    nblk = pl.num_programs(0) * STEP_BLOCKS
    b0 = i * STEP_BLOCKS

    @pl.when(i == 0)
    def _():
        _gather_rows(tok_ref, 0, x_ref, xg_ref, 0)

    @pl.when(b0 < nused_ref[0])
    def _():
        for j in range(STEP_BLOCKS):
            b = b0 + j
            _stage_weights(b, be_ref, nused_ref, gidx_ref, nxt_ref, (wg_hbm, wu_hbm), wbuf_ref,
                           sem_ref, (wgb_ref, wub_ref))
            x = _load_tiled_rows(xg_ref, (j % 2,), MOE_BLOCK).astype(BF)
            g = jnp.dot(x, wgb_ref[...], preferred_element_type=F32)
            u = jnp.dot(x, wub_ref[...], preferred_element_type=F32)
            act_ref[j * MOE_BLOCK:(j + 1) * MOE_BLOCK, :] = ((g * jax.nn.sigmoid(g)) * u).astype(BF)
            _gather_rows(tok_ref, jnp.minimum(b + 1, nblk - 1), x_ref, xg_ref, (j + 1) % 2)

    @pl.when(b0 >= nused_ref[0])
    def _():
        act_ref[...] = jnp.zeros_like(act_ref)


def _experts_up(x, row_tok, blk_exp, nused, gidx, nxt, w_gate, w_up):
    nblk = blk_exp.shape[0]
    step_rows = STEP_BLOCKS * MOE_BLOCK
    return pl.pallas_call(
        _up_kernel,
        grid_spec=pltpu.PrefetchScalarGridSpec(
            num_scalar_prefetch=5,
            grid=(nblk // STEP_BLOCKS,),
            in_specs=[pl.BlockSpec(memory_space=pltpu.VMEM),
                      pl.BlockSpec(memory_space=pl.ANY),
                      pl.BlockSpec(memory_space=pl.ANY)],
            out_specs=pl.BlockSpec((step_rows, D_EXPERT), lambda i, *_: (i, 0)),
            scratch_shapes=[pltpu.VMEM((2, MOE_BLOCK * ROW_SUB, 128), F32),
                            pltpu.VMEM((2, 2, D_MODEL, D_EXPERT), F32),
                            pltpu.VMEM((D_MODEL, D_EXPERT), BF),
                            pltpu.VMEM((D_MODEL, D_EXPERT), BF),
                            pltpu.SemaphoreType.DMA((2, 2))],
        ),
        out_shape=jax.ShapeDtypeStruct((nblk * MOE_BLOCK, D_EXPERT), BF),
        compiler_params=_params("arbitrary"),
        name="experts_up",
    )(blk_exp, nused, gidx, nxt, row_tok, x, w_gate, w_up)


SCATTER_GROUP = 8


def _row_to_column(row):
    n = row.shape[1]
    eye = lax.broadcasted_iota(jnp.int32, (n, n), 0) == lax.broadcasted_iota(jnp.int32, (n, n), 1)
    return jnp.sum(jnp.where(eye, jnp.broadcast_to(row, (n, n)), 0.0), axis=1, keepdims=True)


def _scatter_add_rows(tok_ref, b, y_ref, yb_ref, slot):
    for r0 in range(0, MOE_BLOCK, SCATTER_GROUP):
        rs = range(r0, r0 + SCATTER_GROUP)
        toks = [tok_ref[b * MOE_BLOCK + r] for r in rs]
        new = [y_ref[t] + yb_ref[slot, r * ROW_SUB:(r + 1) * ROW_SUB, :] for r, t in zip(rs, toks)]
        for t, v in zip(toks, new):
            y_ref[t] = v


def _down_kernel(be_ref, nused_ref, gidx_ref, nxt_ref, tok_ref, act_ref, rw_ref, wd_hbm, y_ref,
                 yb_ref, wbuf_ref, wdb_ref, sem_ref):
    i = pl.program_id(0)
    nblk = pl.num_programs(0) * STEP_BLOCKS
    b0 = i * STEP_BLOCKS

    @pl.when(i == 0)
    def _():
        y_ref[...] = jnp.zeros_like(y_ref)
        yb_ref[1] = jnp.zeros(yb_ref.shape[1:], F32)

    @pl.when(b0 <= nused_ref[0])
    def _():
        for j in range(STEP_BLOCKS):
            b = b0 + j
            rows = slice(j * MOE_BLOCK, (j + 1) * MOE_BLOCK)
            _stage_weights(b, be_ref, nused_ref, gidx_ref, nxt_ref, (wd_hbm,), wbuf_ref, sem_ref,
                           (wdb_ref,))
            yb = jnp.dot(act_ref[rows, :], wdb_ref[...],
                         preferred_element_type=F32) * _row_to_column(rw_ref[0, :, rows])
            _store_tiled_rows(yb_ref, (j % 2,), yb)
            _scatter_add_rows(tok_ref, jnp.maximum(b - 1, 0), y_ref, yb_ref, (j + 1) % 2)

    @pl.when((i == pl.num_programs(0) - 1) & (nused_ref[0] >= nblk))
    def _():
        _scatter_add_rows(tok_ref, nblk - 1, y_ref, yb_ref, (STEP_BLOCKS - 1) % 2)


def _experts_down(act, row_tok, row_w, blk_exp, nused, gidx, nxt, w_down, t_rows):
    nblk = blk_exp.shape[0]
    step_rows = STEP_BLOCKS * MOE_BLOCK
    return pl.pallas_call(
        _down_kernel,
        grid_spec=pltpu.PrefetchScalarGridSpec(
            num_scalar_prefetch=5,
            grid=(nblk // STEP_BLOCKS,),
            in_specs=[pl.BlockSpec((step_rows, D_EXPERT), lambda i, *_: (i, 0)),
                      pl.BlockSpec((1, 1, step_rows), lambda i, *_: (i, 0, 0)),
                      pl.BlockSpec(memory_space=pl.ANY)],
            out_specs=pl.BlockSpec(memory_space=pltpu.VMEM),
            scratch_shapes=[pltpu.VMEM((2, MOE_BLOCK * ROW_SUB, 128), F32),
                            pltpu.VMEM((2, 1, D_EXPERT, D_MODEL), F32),
                            pltpu.VMEM((D_EXPERT, D_MODEL), BF),
                            pltpu.SemaphoreType.DMA((2, 1))],
        ),
        out_shape=jax.ShapeDtypeStruct((t_rows, ROW_SUB, 128), F32),
        compiler_params=_params("arbitrary"),
        name="experts_down",
    )(blk_exp, nused, gidx, nxt, row_tok, act, row_w, w_down)


def _final_kernel(h_ref, yr_ref, wg_ref, wu_ref, wd_ref, g_ref, b_ref, o_ref, *, alpha):
    h = h_ref[...]
    hb = h.astype(BF)
    g = jnp.dot(hb, wg_ref[...], preferred_element_type=F32)
    u = jnp.dot(hb, wu_ref[...], preferred_element_type=F32)
    shared = jnp.dot(((g * jax.nn.sigmoid(g)) * u).astype(BF), wd_ref[...],
                     preferred_element_type=F32)
    f = _load_tiled_rows(yr_ref, (), h.shape[0]) + shared
    o_ref[...] = _layer_norm(alpha * h + f, g_ref[...], b_ref[...])


def _final(h2, yr, wg_bf, wu_bf, wd_bf, g, b, alpha, tm):
    t = h2.shape[0]
    tile = pl.BlockSpec((tm, 1024), lambda i: (i, 0))
    full = lambda shape: pl.BlockSpec(shape, lambda i: (0, 0))
    return pl.pallas_call(
        functools.partial(_final_kernel, alpha=alpha),
        grid=(t // tm,),
        in_specs=[tile, pl.BlockSpec((tm * ROW_SUB, 128), lambda i: (i, 0)),
                  full((1024, 256)), full((1024, 256)), full((256, 1024)),
                  full((1, 1024)), full((1, 1024))],
        out_specs=tile,
        out_shape=jax.ShapeDtypeStruct((t, 1024), F32),
        compiler_params=_params("parallel"),
        name="shared_ln2",
    )(h2, yr, wg_bf, wu_bf, wd_bf, g, b)


def _moe_ln2(h2, wr_t_bf, bias_col, w_gate, w_up, w_down, wsg_bf, wsu_bf, wsd_bf, g, b, alpha):
    t = h2.shape[0]
    eidx_t, ew_t = _router(h2, wr_t_bf, bias_col, ROUTER_TILE)
    row_tok, row_w, blk_exp, nused, gidx, nxt = _dispatch(eidx_t.T, ew_t.T, t)
    act = _experts_up(h2.reshape(t, ROW_SUB, 128), jnp.minimum(row_tok, t - 1), blk_exp, nused, gidx,
                      nxt, w_gate, w_up)
    yr = _experts_down(act, row_tok, row_w, blk_exp, nused, gidx, nxt, w_down, t + 8)
    return _final(h2, yr.reshape((t + 8) * ROW_SUB, 128), wsg_bf, wsu_bf, wsd_bf, g, b, alpha, 256)


def _permute_w_in(w_in):
    rq, rk, rv, rg, sq, sk, sv, mq, gl = jnp.split(
        w_in, [512, 1024, 2048, 3072, 4096, 4352, 4608, 5632], axis=-1)
    return jnp.concatenate([rq, rk, rv, rg, sq, mq, gl, sk, sv], -1)


def kernel(x_prompt, x_sample, mem_prompt, cache_ret_state, cache_swa_k, cache_swa_v, cache_mem_k,
           cache_mem_v, w_in, swa_sinks, w_mem_kv, w_branch, w_o, ln1_g, ln1_b, w_router,
           router_bias, w_gate_e, w_up_e, w_down_e, w_sh_gate, w_sh_up, w_sh_down, ln2_g, ln2_b):
    depth = w_in.shape[0]
    assert depth == 1
    alpha = (2.0 * depth) ** 0.25
    bp, lp, d = x_prompt.shape
    bs, ls, _ = x_sample.shape
    l = 0

    w_in_bf = _permute_w_in(w_in[l]).astype(BF)
    sinks = swa_sinks[l]
    wb_bf = w_branch[l].astype(BF)
    wo_bf = w_o[l].astype(BF)
    g1, b1 = ln1_g[l].reshape(1, d), ln1_b[l].reshape(1, d)
    g2, b2 = ln2_g[l].reshape(1, d), ln2_b[l].reshape(1, d)

    tp = bp * lp
    xp2 = x_prompt.reshape(tp, d)
    hp2 = _proj(xp2, w_in_bf, 1024, 512)
    hp3 = hp2.reshape(bp, lp, D_IN)
    mkv = _proj(mem_prompt.reshape(bp * N_MEM, d), w_mem_kv[l].astype(BF), N_MEM, 512)
    mk_p, mv_p = mkv[:, :1024].reshape(bp, N_MEM, 1024), mkv[:, 1024:].reshape(bp, N_MEM, 1024)
    rs0 = jnp.zeros((bp, RET_HEADS, RET_DK, RET_DV), F32)
    ro_p, rs_p = _retention(hp3, jnp.arange(lp), rs0, RET_CHUNK, RET_CHUNK)
    so_p = _swa_prompt(hp2, sinks)
    mo_p = _mem_attend(hp3, mk_p, mv_p, 256)
    hmid_p = _merge(ro_p.reshape(tp, d), so_p, mo_p.reshape(tp, d), hp2, xp2, wb_bf, wo_bf,
                    g1, b1, alpha, 256)

    ts = bs * SAMPLE_PAD
    xs3 = jnp.pad(x_sample, ((0, 0), (0, SAMPLE_PAD - ls), (0, 0)))
    xs2 = xs3.reshape(ts, d)
    hs2 = _proj(xs2, w_in_bf, ts, 512)
    hs3 = hs2.reshape(bs, SAMPLE_PAD, D_IN)
    pos_s = PAST_LEN + jnp.arange(SAMPLE_PAD)
    ro_s, rs_s = _retention(hs3, pos_s, cache_ret_state.reshape(bs, RET_HEADS, RET_DK, RET_DV), ls,
                            SAMPLE_PAD)
    w_buf = cache_swa_k.shape[2]
    prev_k = cache_swa_k.reshape(bs, w_buf, SWA_KV_HEADS * SWA_DH)
    prev_v = cache_swa_v.reshape(bs, w_buf, SWA_KV_HEADS * SWA_DH)
    so_s = _swa_sample(hs3, prev_k, prev_v, sinks, ls)
    mo_s = _mem_attend(hs3, cache_mem_k.reshape(bs, N_MEM, 1024),
                       cache_mem_v.reshape(bs, N_MEM, 1024), SAMPLE_PAD)
    hmid_s = _merge(ro_s.reshape(ts, d), so_s.reshape(ts, d), mo_s.reshape(ts, d), hs2, xs2,
                    wb_bf, wo_bf, g1, b1, alpha, 256)
    hmid_s = hmid_s.reshape(bs, SAMPLE_PAD, d)[:, :ls].reshape(bs * ls, d)

    t_all = tp + bs * ls
    assert t_all % (2 * ROUTER_TILE) == 0 and t_all // 2 <= tp
    th = t_all // 2
    moe = functools.partial(
        _moe_ln2, wr_t_bf=w_router[l].T.astype(BF), bias_col=router_bias[l].reshape(N_EXPERTS, 1),
        w_gate=w_gate_e[l], w_up=w_up_e[l], w_down=w_down_e[l], wsg_bf=w_sh_gate[l].astype(BF),
        wsu_bf=w_sh_up[l].astype(BF), wsd_bf=w_sh_down[l].astype(BF), g=g2, b=b2, alpha=alpha)
    y_a = moe(hmid_p[:th])
    y_b = moe(jnp.concatenate([hmid_p[th:], hmid_s], 0))

    y_p = jnp.concatenate([y_a, y_b[:tp - th]], 0).reshape(bp, lp, d)
    y_s = y_b[tp - th:].reshape(bs, ls, d)
    kv4 = lambda a, n: a.reshape(1, a.shape[0], n, SWA_KV_HEADS, SWA_DH)
    k_p = kv4(hp3[:, lp - WINDOW:, COL_SK:COL_SK + 256], WINDOW)
    v_p = kv4(hp3[:, lp - WINDOW:, COL_SV:COL_SV + 256], WINDOW)
    mem4 = lambda a: a.reshape(1, bp, N_MEM, MEM_HEADS, MEM_DH)
    k_s = kv4(jnp.concatenate([prev_k, hs3[:, :ls, COL_SK:COL_SK + 256]], 1)[:, -w_buf:], w_buf)
    v_s = kv4(jnp.concatenate([prev_v, hs3[:, :ls, COL_SV:COL_SV + 256]], 1)[:, -w_buf:], w_buf)
    return (y_p, y_s, rs_p[None], k_p, v_p, mem4(mk_p), mem4(mv_p), rs_s[None], k_s, v_s)
```

```python
import functools

import jax
import jax.numpy as jnp
from jax import lax
from jax.experimental import pallas as pl
from jax.experimental.pallas import tpu as pltpu

BF = jnp.bfloat16
F32 = jnp.float32

D_MODEL = 1024
RET_HEADS = 4
RET_DK = 128
RET_DV = 256
RET_CHUNK = 128
ROPE_BASE = 10000.0
SWA_HEADS = 16
SWA_KV_HEADS = 4
SWA_GROUP = SWA_HEADS // SWA_KV_HEADS
SWA_DH = 64
WINDOW = 128
SWA_BLOCK = 128
N_MEM = 256
MEM_HEADS = 4
MEM_DH = 256
N_BRANCH = 3
N_EXPERTS = 256
TOP_K = 8
N_GROUPS = 8
GROUP_SIZE = N_EXPERTS // N_GROUPS
TOPK_GROUPS = 4
D_EXPERT = 256
ROUTED_SCALE = 2.5
MOE_BLOCK = 128
ROUTER_TILE = 384
ROW_SUB = D_MODEL // 128
STEP_BLOCKS = 4
LN_EPS = 1e-5
NORM_EPS = 1e-6
NEG = -1e30
PAST_LEN = 16384
SAMPLE_PAD = 8

COL_RQ, COL_RK, COL_RV, COL_RG, COL_SQ, COL_MQ, COL_GL, COL_SK, COL_SV = (
    0, 512, 1024, 2048, 3072, 4096, 5120, 8192, 8448)
D_IN = 8704

VMEM_LIMIT = 56 * 1024 * 1024


def _params(*sem):
    return pltpu.CompilerParams(dimension_semantics=sem, vmem_limit_bytes=VMEM_LIMIT)


def _bdot(a, b):
    return jnp.dot(a.astype(BF), b.astype(BF), preferred_element_type=F32)


def _bdot_nt(a, b):
    return lax.dot_general(a.astype(BF), b.astype(BF), (((1,), (1,)), ((), ())),
                           preferred_element_type=F32)


def _bdot_tn(a, b):
    return lax.dot_general(a.astype(BF), b.astype(BF), (((0,), (0,)), ((), ())),
                           preferred_element_type=F32)


def _layer_norm(z, g, b):
    zc = z - jnp.mean(z, -1, keepdims=True)
    var = jnp.mean(zc * zc, -1, keepdims=True)
    return zc * lax.rsqrt(var + LN_EPS) * g + b


def _load_tiled_rows(ref, idx, m):
    return jnp.concatenate([ref[(*idx, pl.ds(s, m, stride=ROW_SUB), slice(None))]
                            for s in range(ROW_SUB)], axis=1)


def _store_tiled_rows(ref, idx, val):
    m = val.shape[0]
    for s in range(ROW_SUB):
        ref[(*idx, pl.ds(s, m, stride=ROW_SUB), slice(None))] = val[:, s * 128:(s + 1) * 128]


def _proj_kernel(x_ref, w_ref, o_ref, xb_ref):
    @pl.when(pl.program_id(1) == 0)
    def _():
        xb_ref[...] = x_ref[...].astype(BF)

    o_ref[...] = jnp.dot(xb_ref[...], w_ref[...], preferred_element_type=F32)


def _proj(x, w_bf, tm, tn):
    m, k = x.shape
    n = w_bf.shape[1]
    return pl.pallas_call(
        _proj_kernel,
        grid=(m // tm, n // tn),
        in_specs=[pl.BlockSpec((tm, k), lambda i, j: (i, 0)),
                  pl.BlockSpec((k, tn), lambda i, j: (0, j))],
        out_specs=pl.BlockSpec((tm, tn), lambda i, j: (i, j)),
        out_shape=jax.ShapeDtypeStruct((m, n), F32),
        scratch_shapes=[pltpu.VMEM((tm, k), BF)],
        compiler_params=_params("parallel", "arbitrary"),
        name="proj",
    )(x, w_bf)


def _ret_tables(c_real, c_pad):
    lg = jnp.log1p(-jnp.exp2(-5.0 - jnp.arange(RET_HEADS, dtype=F32)))
    idx = jnp.arange(c_pad, dtype=F32)
    real = idx < c_real
    rel = idx[:, None] - idx[None, :]
    intra = jnp.where(rel >= 0, jnp.exp(lg[:, None, None] * jnp.maximum(rel, 0.0)), 0.0)
    intra = jnp.where(real[None, :, None] & real[None, None, :], intra, 0.0)
    q_dec = jnp.where(real[None, :], jnp.exp((idx[None, :] + 1.0) * lg[:, None]), 0.0)
    k_dec = jnp.where(real[None, :], jnp.exp((c_real - 1.0 - idx)[None, :] * lg[:, None]), 0.0)
    c_dec = jnp.exp(c_real * lg)
    bc = lambda t: jnp.broadcast_to(t[:, :, None], (RET_HEADS, c_pad, RET_DK))
    return intra, bc(q_dec), bc(k_dec), c_dec


def _rope_tables(pos):
    half = RET_DK // 2
    inv_freq = 1.0 / (ROPE_BASE ** (jnp.arange(half, dtype=F32) / half))
    ang = pos.astype(F32)[:, None] * inv_freq[None, :]
    cos, sin = jnp.cos(ang), jnp.sin(ang)
    return jnp.concatenate([cos, cos], -1), jnp.concatenate([-sin, sin], -1)


def _ret_kernel(cdec_ref, rq_ref, rk_ref, rv_ref, rg_ref, cos_ref, sin_ref, intra_ref, qdec_ref,
                kdec_ref, s0_ref, o_ref, s_out_ref, s_scr, *, n_chunks):
    c = pl.program_id(1)

    @pl.when(c == 0)
    def _():
        s_scr[...] = s0_ref[0]

    cos2 = cos_ref[...]
    sin2 = sin_ref[...]

    def rot(x):
        return x * cos2 + pltpu.roll(x, RET_DK // 2, 1) * sin2

    for h in range(RET_HEADS):
        q = rot(rq_ref[0, :, h * RET_DK:(h + 1) * RET_DK])
        k = rot(rk_ref[0, :, h * RET_DK:(h + 1) * RET_DK]) * (RET_DK ** -0.5)
        v = rv_ref[0, :, h * RET_DV:(h + 1) * RET_DV].astype(BF)
        s_old = s_scr[h]
        a = _bdot_nt(q, k) * intra_ref[h]
        o = _bdot(a, v) + _bdot(q * qdec_ref[h], s_old)
        s_scr[h] = s_old * cdec_ref[h] + _bdot_tn(k * kdec_ref[h], v)
        o = o * lax.rsqrt(jnp.mean(o * o, -1, keepdims=True) + NORM_EPS)
        g = rg_ref[0, :, h * RET_DV:(h + 1) * RET_DV]
        o_ref[0, :, h * RET_DV:(h + 1) * RET_DV] = o * (g * jax.nn.sigmoid(g))

    @pl.when(c == n_chunks - 1)
    def _():
        s_out_ref[0] = s_scr[...]


def _retention(h3, pos, state0, c_real, c_pad):
    b, l, _ = h3.shape
    n_chunks = l // c_pad
    intra, qdec, kdec, cdec = _ret_tables(c_real, c_pad)
    cos2, sin2 = _rope_tables(pos)
    full3 = lambda shape: pl.BlockSpec(shape, lambda i, c: (0, 0, 0))
    return pl.pallas_call(
        functools.partial(_ret_kernel, n_chunks=n_chunks),
        grid=(b, n_chunks),
        in_specs=[
            pl.BlockSpec(memory_space=pltpu.SMEM),
            pl.BlockSpec((1, c_pad, 512), lambda i, c: (i, c, COL_RQ // 512)),
            pl.BlockSpec((1, c_pad, 512), lambda i, c: (i, c, COL_RK // 512)),
            pl.BlockSpec((1, c_pad, 1024), lambda i, c: (i, c, COL_RV // 1024)),
            pl.BlockSpec((1, c_pad, 1024), lambda i, c: (i, c, COL_RG // 1024)),
            pl.BlockSpec((c_pad, RET_DK), lambda i, c: (c, 0)),
            pl.BlockSpec((c_pad, RET_DK), lambda i, c: (c, 0)),
            full3((RET_HEADS, c_pad, c_pad)),
            full3((RET_HEADS, c_pad, RET_DK)),
            full3((RET_HEADS, c_pad, RET_DK)),
            pl.BlockSpec((1, RET_HEADS, RET_DK, RET_DV), lambda i, c: (i, 0, 0, 0)),
        ],
        out_specs=[
            pl.BlockSpec((1, c_pad, 1024), lambda i, c: (i, c, 0)),
            pl.BlockSpec((1, RET_HEADS, RET_DK, RET_DV), lambda i, c: (i, 0, 0, 0)),
        ],
        out_shape=[jax.ShapeDtypeStruct((b, l, 1024), F32),
                   jax.ShapeDtypeStruct((b, RET_HEADS, RET_DK, RET_DV), F32)],
        scratch_shapes=[pltpu.VMEM((RET_HEADS, RET_DK, RET_DV), F32)],
        compiler_params=_params("parallel", "arbitrary"),
        name="retention",
    )(cdec, h3, h3, h3, h3, cos2, sin2, intra, qdec, kdec, state0)


def _sink_softmax(s, sink):
    m = jnp.maximum(jnp.max(s, -1, keepdims=True), sink)
    p = jnp.exp(s - m)
    return p / (jnp.sum(p, -1, keepdims=True) + jnp.exp(sink - m))


def _swa_prompt_kernel(sinks_ref, q_ref, kp_ref, kc_ref, vp_ref, vc_ref, o_ref):
    n = pl.program_id(0)
    qi = lax.broadcasted_iota(jnp.int32, (SWA_BLOCK, 2 * SWA_BLOCK), 0)
    kj = lax.broadcasted_iota(jnp.int32, (SWA_BLOCK, 2 * SWA_BLOCK), 1)
    rel = SWA_BLOCK + qi - kj
    valid = (rel >= 0) & (rel <= WINDOW) & ((kj >= SWA_BLOCK) | (n > 0))
    for h in range(SWA_KV_HEADS):
        sl = slice(h * SWA_DH, (h + 1) * SWA_DH)
        k2 = jnp.concatenate([kp_ref[:, sl], kc_ref[:, sl]], 0).astype(BF)
        v2 = jnp.concatenate([vp_ref[:, sl], vc_ref[:, sl]], 0).astype(BF)
        for g in range(SWA_GROUP):
            hq = h * SWA_GROUP + g
            qsl = slice(hq * SWA_DH, (hq + 1) * SWA_DH)
            s = _bdot_nt(q_ref[:, qsl], k2) * (SWA_DH ** -0.5)
            s = jnp.where(valid, s, NEG)
            p = _sink_softmax(s, sinks_ref[hq])
            o_ref[:, qsl] = _bdot(p, v2)


def _swa_prompt(h2, sinks):
    t = h2.shape[0]
    nb = t // SWA_BLOCK
    prev = lambda col: (lambda n: (jnp.maximum(n - 1, 0), col))
    cur = lambda col: (lambda n: (n, col))
    ck, cv = COL_SK // 256, COL_SV // 256
    return pl.pallas_call(
        _swa_prompt_kernel,
        grid=(nb,),
        in_specs=[
            pl.BlockSpec(memory_space=pltpu.SMEM),
            pl.BlockSpec((SWA_BLOCK, 1024), cur(COL_SQ // 1024)),
            pl.BlockSpec((SWA_BLOCK, 256), prev(ck)),
            pl.BlockSpec((SWA_BLOCK, 256), cur(ck)),
            pl.BlockSpec((SWA_BLOCK, 256), prev(cv)),
            pl.BlockSpec((SWA_BLOCK, 256), cur(cv)),
        ],
        out_specs=pl.BlockSpec((SWA_BLOCK, 1024), lambda n: (n, 0)),
        out_shape=jax.ShapeDtypeStruct((t, 1024), F32),
        compiler_params=_params("parallel"),
        name="swa_prompt",
    )(sinks, h2, h2, h2, h2, h2)


def _swa_sample_kernel(sinks_ref, q_ref, kn_ref, vn_ref, kp_ref, vp_ref, o_ref, *, n_new):
    tb = q_ref.shape[0]
    w = kp_ref.shape[1]
    p_ = SAMPLE_PAD
    qi = lax.broadcasted_iota(jnp.int32, (tb, p_, w), 1)
    kj = lax.broadcasted_iota(jnp.int32, (tb, p_, w), 2)
    rel_prev = w + qi - kj
    valid_prev = (rel_prev >= 0) & (rel_prev <= WINDOW)
    qn = lax.broadcasted_iota(jnp.int32, (tb, p_, p_), 1)
    kn = lax.broadcasted_iota(jnp.int32, (tb, p_, p_), 2)
    valid_new = (qn - kn >= 0) & (qn - kn <= WINDOW) & (kn < n_new)
    bdot = lambda eq, a, b: jnp.einsum(eq, a.astype(BF), b.astype(BF), preferred_element_type=F32)
    for h in range(SWA_KV_HEADS):
        sl = slice(h * SWA_DH, (h + 1) * SWA_DH)
        kp, vp = kp_ref[:, :, sl], vp_ref[:, :, sl]
        kn_h, vn_h = kn_ref[:, :, sl], vn_ref[:, :, sl]
        for g in range(SWA_GROUP):
            hq = h * SWA_GROUP + g
            qsl = slice(hq * SWA_DH, (hq + 1) * SWA_DH)
            q = q_ref[:, :, qsl]
            sp = bdot('bqd,bkd->bqk', q, kp) * (SWA_DH ** -0.5)
            sn = bdot('bqd,bkd->bqk', q, kn_h) * (SWA_DH ** -0.5)
            sp = jnp.where(valid_prev, sp, NEG)
            sn = jnp.where(valid_new, sn, NEG)
            sink = sinks_ref[hq]
            m = jnp.maximum(jnp.maximum(jnp.max(sp, -1, keepdims=True),
                                        jnp.max(sn, -1, keepdims=True)), sink)
            pp = jnp.exp(sp - m)
            pn = jnp.exp(sn - m)
            den = jnp.sum(pp, -1, keepdims=True) + jnp.sum(pn, -1, keepdims=True) + jnp.exp(sink - m)
            o = bdot('bqk,bkd->bqd', pp / den, vp) + bdot('bqk,bkd->bqd', pn / den, vn_h)
            o_ref[:, :, qsl] = o


def _swa_sample(h3, prev_k, prev_v, sinks, n_new, tb=8):
    b = h3.shape[0]
    w = prev_k.shape[1]
    return pl.pallas_call(
        functools.partial(_swa_sample_kernel, n_new=n_new),
        grid=(b // tb,),
        in_specs=[
            pl.BlockSpec(memory_space=pltpu.SMEM),
            pl.BlockSpec((tb, SAMPLE_PAD, 1024), lambda i: (i, 0, COL_SQ // 1024)),
            pl.BlockSpec((tb, SAMPLE_PAD, 256), lambda i: (i, 0, COL_SK // 256)),
            pl.BlockSpec((tb, SAMPLE_PAD, 256), lambda i: (i, 0, COL_SV // 256)),
            pl.BlockSpec((tb, w, 256), lambda i: (i, 0, 0)),
            pl.BlockSpec((tb, w, 256), lambda i: (i, 0, 0)),
        ],
        out_specs=pl.BlockSpec((tb, SAMPLE_PAD, 1024), lambda i: (i, 0, 0)),
        out_shape=jax.ShapeDtypeStruct((b, SAMPLE_PAD, 1024), F32),
        compiler_params=_params("parallel"),
        name="swa_sample",
    )(sinks, h3, h3, h3, prev_k, prev_v)


def _mem_kernel(q_ref, mk_ref, mv_ref, o_ref):
    for h in range(MEM_HEADS):
        sl = slice(h * MEM_DH, (h + 1) * MEM_DH)
        s = _bdot_nt(q_ref[0, :, sl], mk_ref[0, :, sl]) * (MEM_DH ** -0.5)
        m = jnp.max(s, -1, keepdims=True)
        e = jnp.exp(s - m)
        p = e / jnp.sum(e, -1, keepdims=True)
        o_ref[0, :, sl] = _bdot(p, mv_ref[0, :, sl])


def _mem_attend(h3, mk, mv, tl):
    b, l, _ = h3.shape
    return pl.pallas_call(
        _mem_kernel,
        grid=(b, l // tl),
        in_specs=[
            pl.BlockSpec((1, tl, 1024), lambda i, j: (i, j, COL_MQ // 1024)),
            pl.BlockSpec((1, N_MEM, 1024), lambda i, j: (i, 0, 0)),
            pl.BlockSpec((1, N_MEM, 1024), lambda i, j: (i, 0, 0)),
        ],
        out_specs=pl.BlockSpec((1, tl, 1024), lambda i, j: (i, j, 0)),
        out_shape=jax.ShapeDtypeStruct((b, l, 1024), F32),
        compiler_params=_params("parallel", "parallel"),
        name="mem_attend",
    )(h3, mk, mv)


def _merge_kernel(ro_ref, so_ref, mo_ref, g0_ref, g1_ref, g2_ref, x_ref, wb_ref, wo_ref, g_ref,
                  b_ref, o_ref, *, alpha):
    acc = None
    for n, (br, gl) in enumerate(((ro_ref, g0_ref), (so_ref, g1_ref), (mo_ref, g2_ref))):
        term = jax.nn.sigmoid(gl[...]) * jnp.dot(br[...].astype(BF), wb_ref[n],
                                                 preferred_element_type=F32)
        acc = term if acc is None else acc + term
    a = jnp.dot(acc.astype(BF), wo_ref[...], preferred_element_type=F32)
    _store_tiled_rows(o_ref, (), _layer_norm(alpha * x_ref[...] + a, g_ref[...], b_ref[...]))


def _merge(ro, so, mo, h2, x2, wb_bf, wo_bf, g, b, alpha, tm):
    t = x2.shape[0]
    tile = lambda col: pl.BlockSpec((tm, 1024), lambda i: (i, col))
    gl0 = COL_GL // 1024
    return pl.pallas_call(
        functools.partial(_merge_kernel, alpha=alpha),
        grid=(t // tm,),
        in_specs=[tile(0), tile(0), tile(0), tile(gl0), tile(gl0 + 1), tile(gl0 + 2), tile(0),
                  pl.BlockSpec((N_BRANCH, 1024, 1024), lambda i: (0, 0, 0)),
                  pl.BlockSpec((1024, 1024), lambda i: (0, 0)),
                  pl.BlockSpec((1, 1024), lambda i: (0, 0)),
                  pl.BlockSpec((1, 1024), lambda i: (0, 0))],
        out_specs=pl.BlockSpec((tm * ROW_SUB, 128), lambda i: (i, 0)),
        out_shape=jax.ShapeDtypeStruct((t * ROW_SUB, 128), F32),
        compiler_params=_params("parallel"),
        name="merge_ln1",
    )(ro, so, mo, h2, h2, h2, x2, wb_bf, wo_bf, g, b)


def _first_index_of_max(v, iota, big, axes):
    m = jnp.max(v, axis=axes, keepdims=True)
    idx = jnp.min(jnp.where(v == m, iota, big), axis=axes, keepdims=True)
    return m, idx


def _router_kernel(x_ref, wr_ref, bias_ref, eidx_ref, ew_ref):
    tt = x_ref.shape[0] // ROW_SUB
    x = _load_tiled_rows(x_ref, (), tt).astype(BF)
    logits = lax.dot_general(wr_ref[...], x, (((1,), (1,)), ((), ())),
                             preferred_element_type=F32)
    s = jax.nn.sigmoid(logits).reshape(N_GROUPS, GROUP_SIZE, tt)
    sb = s + bias_ref[...].reshape(N_GROUPS, GROUP_SIZE, 1)
    ninf = -jnp.inf
    r_iota = lax.broadcasted_iota(jnp.int32, sb.shape, 1)
    m1, i1 = _first_index_of_max(sb, r_iota, GROUP_SIZE, 1)
    m2 = jnp.max(jnp.where(r_iota == i1, ninf, sb), axis=1, keepdims=True)
    gsc = (m1 + m2).reshape(N_GROUPS, tt)
    g_iota = lax.broadcasted_iota(jnp.int32, gsc.shape, 0)
    gmask = jnp.zeros(gsc.shape, jnp.bool_)
    for _ in range(TOPK_GROUPS):
        _, gi = _first_index_of_max(gsc, g_iota, N_GROUPS, 0)
        hit = g_iota == gi
        gmask = gmask | hit
        gsc = jnp.where(hit, ninf, gsc)
    cand = jnp.where(gmask.reshape(N_GROUPS, 1, tt), sb, ninf)
    e_iota = lax.broadcasted_iota(jnp.int32, sb.shape, 0) * GROUP_SIZE + r_iota
    idxs, ws = [], []
    for _ in range(TOP_K):
        _, ei = _first_index_of_max(cand, e_iota, N_EXPERTS, (0, 1))
        hit = e_iota == ei
        idxs.append(ei.reshape(1, tt))
        ws.append(jnp.sum(jnp.where(hit, s, 0.0), axis=(0, 1)).reshape(1, tt))
        cand = jnp.where(hit, ninf, cand)
    w = jnp.concatenate(ws, 0)
    eidx_ref[...] = jnp.concatenate(idxs, 0)
    ew_ref[...] = w / jnp.sum(w, 0, keepdims=True) * ROUTED_SCALE


def _router(xt, wr_t_bf, bias_col, tt):
    t = xt.shape[0] // ROW_SUB
    return pl.pallas_call(
        _router_kernel,
        grid=(t // tt,),
        in_specs=[pl.BlockSpec((tt * ROW_SUB, 128), lambda i: (i, 0)),
                  pl.BlockSpec((N_EXPERTS, 1024), lambda i: (0, 0)),
                  pl.BlockSpec((N_EXPERTS, 1), lambda i: (0, 0))],
        out_specs=[pl.BlockSpec((TOP_K, tt), lambda i: (0, i)),
                   pl.BlockSpec((TOP_K, tt), lambda i: (0, i))],
        out_shape=[jax.ShapeDtypeStruct((TOP_K, t), jnp.int32),
                   jax.ShapeDtypeStruct((TOP_K, t), F32)],
        compiler_params=_params("parallel"),
        name="router",
    )(xt, wr_t_bf, bias_col)


IDX_BITS = 18


def _dispatch(eidx, ew, t):
    a = t * TOP_K
    assert a < (1 << IDX_BITS)
    nblk = -(-a // MOE_BLOCK) + N_EXPERTS
    assert nblk % STEP_BLOCKS == 0 and STEP_BLOCKS % 2 == 0
    flat_e = eidx.reshape(-1)
    skey, sw = lax.sort((flat_e * (1 << IDX_BITS) + jnp.arange(a, dtype=jnp.int32), ew.reshape(-1)),
                        num_keys=1)
    si = skey & ((1 << IDX_BITS) - 1)
    experts = jnp.arange(N_EXPERTS, dtype=jnp.int32)
    counts = jnp.sum((flat_e[None, :] == experts[:, None]).astype(jnp.int32), axis=1)
    grp_start = jnp.cumsum(counts) - counts
    padded = (counts + MOE_BLOCK - 1) // MOE_BLOCK * MOE_BLOCK
    pad_end = jnp.cumsum(padded)
    pad_start = pad_end - padded
    blk_first = jnp.arange(nblk, dtype=jnp.int32) * MOE_BLOCK
    blk_exp = jnp.minimum(jnp.sum((pad_end[None, :] <= blk_first[:, None]).astype(jnp.int32), axis=1),
                          N_EXPERTS - 1)
    off = (jnp.arange(nblk * MOE_BLOCK, dtype=jnp.int32).reshape(nblk, MOE_BLOCK)
           - pad_start[blk_exp][:, None])
    valid = off < counts[blk_exp][:, None]
    src = jnp.clip(grp_start[blk_exp][:, None] + off, 0, a - 1)
    picked = jnp.stack([(si >> 3).astype(F32), sw], axis=1)[src]
    row_tok = jnp.where(valid, picked[..., 0].astype(jnp.int32), t)
    row_w = jnp.where(valid, picked[..., 1], 0.0)
    nused = (pad_end[-1] // MOE_BLOCK).astype(jnp.int32).reshape(1)
    gidx, nxt = _group_tables(counts, blk_exp)
    step_rows = STEP_BLOCKS * MOE_BLOCK
    return (row_tok.reshape(-1), row_w.reshape(nblk // STEP_BLOCKS, 1, step_rows), blk_exp, nused,
            gidx, nxt)


def _group_tables(counts, blk_exp):
    nonempty = counts > 0
    gidx = (jnp.cumsum(nonempty.astype(jnp.int32)) - 1)[blk_exp]
    experts = jnp.arange(N_EXPERTS, dtype=jnp.int32)
    cand = jnp.where(nonempty, experts, N_EXPERTS)
    later = lax.cummin(cand, axis=0, reverse=True)
    nxt = jnp.concatenate([later[1:], jnp.full((1,), N_EXPERTS, jnp.int32)])
    nxt = jnp.where(nxt >= N_EXPERTS, -1, nxt)
    return gidx.astype(jnp.int32), nxt[blk_exp].astype(jnp.int32)


def _weight_copies(hbm_refs, buf_ref, sem_ref, e, slot):
    return [pltpu.make_async_copy(h.at[e], buf_ref.at[slot, k], sem_ref.at[slot, k])
            for k, h in enumerate(hbm_refs)]


def _stage_weights(b, be_ref, nused_ref, gidx_ref, nxt_ref, hbm_refs, buf_ref, sem_ref, cache_refs):
    first = ((b == 0) | (be_ref[b] != be_ref[jnp.maximum(b - 1, 0)])) & (b < nused_ref[0])

    @pl.when(first)
    def _():
        slot = gidx_ref[b] % 2

        @pl.when(b == 0)
        def _():
            for c in _weight_copies(hbm_refs, buf_ref, sem_ref, be_ref[0], 0):
                c.start()

        for c in _weight_copies(hbm_refs, buf_ref, sem_ref, be_ref[b], slot):
            c.wait()
        nxt = nxt_ref[b]

        @pl.when(nxt >= 0)
        def _():
            for c in _weight_copies(hbm_refs, buf_ref, sem_ref, nxt, 1 - slot):
                c.start()

        for k, cache in enumerate(cache_refs):
            cache[...] = buf_ref[slot, k].astype(BF)


def _gather_rows(tok_ref, b, x_ref, xg_ref, slot):
    for r in range(MOE_BLOCK):
        xg_ref[slot, r * ROW_SUB:(r + 1) * ROW_SUB, :] = x_ref[tok_ref[b * MOE_BLOCK + r]]


def _up_kernel(be_ref, nused_ref, gidx_ref, nxt_ref, tok_ref, x_ref, wg_hbm, wu_hbm, act_ref,
               xg_ref, wbuf_ref, wgb_ref, wub_ref, sem_ref):
    i = pl.program_id(0)
    nblk = pl.num_programs(0) * STEP_BLOCKS
    b0 = i * STEP_BLOCKS

    @pl.when(i == 0)
    def _():
        _gather_rows(tok_ref, 0, x_ref, xg_ref, 0)

    @pl.when(b0 < nused_ref[0])
    def _():
        for j in range(STEP_BLOCKS):
            b = b0 + j
            _stage_weights(b, be_ref, nused_ref, gidx_ref, nxt_ref, (wg_hbm, wu_hbm), wbuf_ref,
                           sem_ref, (wgb_ref, wub_ref))
            _gather_rows(tok_ref, jnp.minimum(b + 1, nblk - 1), x_ref, xg_ref, (j + 1) % 2)
            x = _load_tiled_rows(xg_ref, (j % 2,), MOE_BLOCK).astype(BF)
            g = jnp.dot(x, wgb_ref[...], preferred_element_type=F32)
            u = jnp.dot(x, wub_ref[...], preferred_element_type=F32)
            act_ref[j * MOE_BLOCK:(j + 1) * MOE_BLOCK, :] = ((g * jax.nn.sigmoid(g)) * u).astype(BF)

    @pl.when(b0 >= nused_ref[0])
    def _():
        act_ref[...] = jnp.zeros_like(act_ref)


def _experts_up(x, row_tok, blk_exp, nused, gidx, nxt, w_gate, w_up):
    nblk = blk_exp.shape[0]
    step_rows = STEP_BLOCKS * MOE_BLOCK
    return pl.pallas_call(
        _up_kernel,
        grid_spec=pltpu.PrefetchScalarGridSpec(
            num_scalar_prefetch=5,
            grid=(nblk // STEP_BLOCKS,),
            in_specs=[pl.BlockSpec(memory_space=pltpu.VMEM),
                      pl.BlockSpec(memory_space=pl.ANY),
                      pl.BlockSpec(memory_space=pl.ANY)],
            out_specs=pl.BlockSpec((step_rows, D_EXPERT), lambda i, *_: (i, 0)),
            scratch_shapes=[pltpu.VMEM((2, MOE_BLOCK * ROW_SUB, 128), F32),
                            pltpu.VMEM((2, 2, D_MODEL, D_EXPERT), F32),
                            pltpu.VMEM((D_MODEL, D_EXPERT), BF),
                            pltpu.VMEM((D_MODEL, D_EXPERT), BF),
                            pltpu.SemaphoreType.DMA((2, 2))],
        ),
        out_shape=jax.ShapeDtypeStruct((nblk * MOE_BLOCK, D_EXPERT), BF),
        compiler_params=_params("arbitrary"),
        name="experts_up",
    )(blk_exp, nused, gidx, nxt, row_tok, x, w_gate, w_up)


SCATTER_GROUP = 8


def _row_to_column(row):
    n = row.shape[1]
    eye = lax.broadcasted_iota(jnp.int32, (n, n), 0) == lax.broadcasted_iota(jnp.int32, (n, n), 1)
    return jnp.sum(jnp.where(eye, jnp.broadcast_to(row, (n, n)), 0.0), axis=1, keepdims=True)


def _scatter_add_rows(tok_ref, b, y_ref, yb_ref, slot):
    for r0 in range(0, MOE_BLOCK, SCATTER_GROUP):
        rs = range(r0, r0 + SCATTER_GROUP)
        toks = [tok_ref[b * MOE_BLOCK + r] for r in rs]
        new = [y_ref[t] + yb_ref[slot, r * ROW_SUB:(r + 1) * ROW_SUB, :] for r, t in zip(rs, toks)]
        for t, v in zip(toks, new):
            y_ref[t] = v


def _down_kernel(be_ref, nused_ref, gidx_ref, nxt_ref, tok_ref, act_ref, rw_ref, wd_hbm, y_ref,
                 yb_ref, wbuf_ref, wdb_ref, sem_ref):
    i = pl.program_id(0)
    nblk = pl.num_programs(0) * STEP_BLOCKS
    b0 = i * STEP_BLOCKS

    @pl.when(i == 0)
    def _():
        y_ref[...] = jnp.zeros_like(y_ref)
        yb_ref[1] = jnp.zeros(yb_ref.shape[1:], F32)

    @pl.when(b0 <= nused_ref[0])
    def _():
        for j in range(STEP_BLOCKS):
            b = b0 + j
            rows = slice(j * MOE_BLOCK, (j + 1) * MOE_BLOCK)
            _stage_weights(b, be_ref, nused_ref, gidx_ref, nxt_ref, (wd_hbm,), wbuf_ref, sem_ref,
                           (wdb_ref,))
            yb = jnp.dot(act_ref[rows, :], wdb_ref[...],
                         preferred_element_type=F32) * _row_to_column(rw_ref[0, :, rows])
            _store_tiled_rows(yb_ref, (j % 2,), yb)
            _scatter_add_rows(tok_ref, jnp.maximum(b - 1, 0), y_ref, yb_ref, (j + 1) % 2)

    @pl.when((i == pl.num_programs(0) - 1) & (nused_ref[0] >= nblk))
    def _():
        _scatter_add_rows(tok_ref, nblk - 1, y_ref, yb_ref, (STEP_BLOCKS - 1) % 2)


def _experts_down(act, row_tok, row_w, blk_exp, nused, gidx, nxt, w_down, t_rows):
    nblk = blk_exp.shape[0]
    step_rows = STEP_BLOCKS * MOE_BLOCK
    return pl.pallas_call(
        _down_kernel,
        grid_spec=pltpu.PrefetchScalarGridSpec(
            num_scalar_prefetch=5,
            grid=(nblk // STEP_BLOCKS,),
            in_specs=[pl.BlockSpec((step_rows, D_EXPERT), lambda i, *_: (i, 0)),
                      pl.BlockSpec((1, 1, step_rows), lambda i, *_: (i, 0, 0)),
                      pl.BlockSpec(memory_space=pl.ANY)],
            out_specs=pl.BlockSpec(memory_space=pltpu.VMEM),
            scratch_shapes=[pltpu.VMEM((2, MOE_BLOCK * ROW_SUB, 128), F32),
                            pltpu.VMEM((2, 1, D_EXPERT, D_MODEL), F32),
                            pltpu.VMEM((D_EXPERT, D_MODEL), BF),
                            pltpu.SemaphoreType.DMA((2, 1))],
        ),
        out_shape=jax.ShapeDtypeStruct((t_rows, ROW_SUB, 128), F32),
        compiler_params=_params("arbitrary"),
        name="experts_down",
    )(blk_exp, nused, gidx, nxt, row_tok, act, row_w, w_down)


def _final_kernel(h_ref, yr_ref, wg_ref, wu_ref, wd_ref, g_ref, b_ref, o_ref, *, alpha):
    h = _load_tiled_rows(h_ref, (), o_ref.shape[0])
    hb = h.astype(BF)
    g = jnp.dot(hb, wg_ref[...], preferred_element_type=F32)
    u = jnp.dot(hb, wu_ref[...], preferred_element_type=F32)
    shared = jnp.dot(((g * jax.nn.sigmoid(g)) * u).astype(BF), wd_ref[...],
                     preferred_element_type=F32)
    f = _load_tiled_rows(yr_ref, (), h.shape[0]) + shared
    o_ref[...] = _layer_norm(alpha * h + f, g_ref[...], b_ref[...])


def _final(ht, yr, wg_bf, wu_bf, wd_bf, g, b, alpha, tm):
    t = ht.shape[0] // ROW_SUB
    tiled = pl.BlockSpec((tm * ROW_SUB, 128), lambda i: (i, 0))
    full = lambda shape: pl.BlockSpec(shape, lambda i: (0, 0))
    return pl.pallas_call(
        functools.partial(_final_kernel, alpha=alpha),
        grid=(t // tm,),
        in_specs=[tiled, tiled, full((1024, 256)), full((1024, 256)), full((256, 1024)),
                  full((1, 1024)), full((1, 1024))],
        out_specs=pl.BlockSpec((tm, 1024), lambda i: (i, 0)),
        out_shape=jax.ShapeDtypeStruct((t, 1024), F32),
        compiler_params=_params("parallel"),
        name="shared_ln2",
    )(ht, yr, wg_bf, wu_bf, wd_bf, g, b)


def _moe_ln2(ht, wr_t_bf, bias_col, w_gate, w_up, w_down, wsg_bf, wsu_bf, wsd_bf, g, b, alpha):
    t = ht.shape[0] // ROW_SUB
    eidx_t, ew_t = _router(ht, wr_t_bf, bias_col, ROUTER_TILE)
    row_tok, row_w, blk_exp, nused, gidx, nxt = _dispatch(eidx_t.T, ew_t.T, t)
    act = _experts_up(ht.reshape(t, ROW_SUB, 128), jnp.minimum(row_tok, t - 1), blk_exp, nused, gidx,
                      nxt, w_gate, w_up)
    yr = _experts_down(act, row_tok, row_w, blk_exp, nused, gidx, nxt, w_down, t + 8)
    return _final(ht, yr.reshape((t + 8) * ROW_SUB, 128), wsg_bf, wsu_bf, wsd_bf, g, b, alpha, 256)


def _permute_w_in(w_in):
    rq, rk, rv, rg, sq, sk, sv, mq, gl = jnp.split(
        w_in, [512, 1024, 2048, 3072, 4096, 4352, 4608, 5632], axis=-1)
    return jnp.concatenate([rq, rk, rv, rg, sq, mq, gl, sk, sv], -1)


def kernel(x_prompt, x_sample, mem_prompt, cache_ret_state, cache_swa_k, cache_swa_v, cache_mem_k,
           cache_mem_v, w_in, swa_sinks, w_mem_kv, w_branch, w_o, ln1_g, ln1_b, w_router,
           router_bias, w_gate_e, w_up_e, w_down_e, w_sh_gate, w_sh_up, w_sh_down, ln2_g, ln2_b):
    depth = w_in.shape[0]
    assert depth == 1
    alpha = (2.0 * depth) ** 0.25
    bp, lp, d = x_prompt.shape
    bs, ls, _ = x_sample.shape
    l = 0

    w_in_bf = _permute_w_in(w_in[l]).astype(BF)
    sinks = swa_sinks[l]
    wb_bf = w_branch[l].astype(BF)
    wo_bf = w_o[l].astype(BF)
    g1, b1 = ln1_g[l].reshape(1, d), ln1_b[l].reshape(1, d)
    g2, b2 = ln2_g[l].reshape(1, d), ln2_b[l].reshape(1, d)

    tp = bp * lp
    xp2 = x_prompt.reshape(tp, d)
    hp2 = _proj(xp2, w_in_bf, 1024, 512)
    hp3 = hp2.reshape(bp, lp, D_IN)
    mkv = _proj(mem_prompt.reshape(bp * N_MEM, d), w_mem_kv[l].astype(BF), N_MEM, 512)
    mk_p, mv_p = mkv[:, :1024].reshape(bp, N_MEM, 1024), mkv[:, 1024:].reshape(bp, N_MEM, 1024)
    rs0 = jnp.zeros((bp, RET_HEADS, RET_DK, RET_DV), F32)
    ro_p, rs_p = _retention(hp3, jnp.arange(lp), rs0, RET_CHUNK, RET_CHUNK)
    so_p = _swa_prompt(hp2, sinks)
    mo_p = _mem_attend(hp3, mk_p, mv_p, 256)
    hmid_p = _merge(ro_p.reshape(tp, d), so_p, mo_p.reshape(tp, d), hp2, xp2, wb_bf, wo_bf,
                    g1, b1, alpha, 256)

    ts = bs * SAMPLE_PAD
    xs3 = jnp.pad(x_sample, ((0, 0), (0, SAMPLE_PAD - ls), (0, 0)))
    xs2 = xs3.reshape(ts, d)
    hs2 = _proj(xs2, w_in_bf, ts, 512)
    hs3 = hs2.reshape(bs, SAMPLE_PAD, D_IN)
    pos_s = PAST_LEN + jnp.arange(SAMPLE_PAD)
    ro_s, rs_s = _retention(hs3, pos_s, cache_ret_state.reshape(bs, RET_HEADS, RET_DK, RET_DV), ls,
                            SAMPLE_PAD)
    w_buf = cache_swa_k.shape[2]
    prev_k = cache_swa_k.reshape(bs, w_buf, SWA_KV_HEADS * SWA_DH)
    prev_v = cache_swa_v.reshape(bs, w_buf, SWA_KV_HEADS * SWA_DH)
    so_s = _swa_sample(hs3, prev_k, prev_v, sinks, ls)
    mo_s = _mem_attend(hs3, cache_mem_k.reshape(bs, N_MEM, 1024),
                       cache_mem_v.reshape(bs, N_MEM, 1024), SAMPLE_PAD)
    hmid_s = _merge(ro_s.reshape(ts, d), so_s.reshape(ts, d), mo_s.reshape(ts, d), hs2, xs2,
                    wb_bf, wo_bf, g1, b1, alpha, 256)
    hmid_s = hmid_s.reshape(bs, SAMPLE_PAD, ROW_SUB, 128)[:, :ls].reshape(bs * ls * ROW_SUB, 128)

    t_all = tp + bs * ls
    assert t_all % (2 * ROUTER_TILE) == 0 and t_all // 2 <= tp
    th = t_all // 2
    moe = functools.partial(
        _moe_ln2, wr_t_bf=w_router[l].T.astype(BF), bias_col=router_bias[l].reshape(N_EXPERTS, 1),
        w_gate=w_gate_e[l], w_up=w_up_e[l], w_down=w_down_e[l], wsg_bf=w_sh_gate[l].astype(BF),
        wsu_bf=w_sh_up[l].astype(BF), wsd_bf=w_sh_down[l].astype(BF), g=g2, b=b2, alpha=alpha)
    y_a = moe(hmid_p[:th * ROW_SUB])
    y_b = moe(jnp.concatenate([hmid_p[th * ROW_SUB:], hmid_s], 0))

    y_p = jnp.concatenate([y_a, y_b[:tp - th]], 0).reshape(bp, lp, d)
    y_s = y_b[tp - th:].reshape(bs, ls, d)
    kv4 = lambda a, n: a.reshape(1, a.shape[0], n, SWA_KV_HEADS, SWA_DH)
    k_p = kv4(hp3[:, lp - WINDOW:, COL_SK:COL_SK + 256], WINDOW)
    v_p = kv4(hp3[:, lp - WINDOW:, COL_SV:COL_SV + 256], WINDOW)
    mem4 = lambda a: a.reshape(1, bp, N_MEM, MEM_HEADS, MEM_DH)
    k_s = kv4(jnp.concatenate([prev_k, hs3[:, :ls, COL_SK:COL_SK + 256]], 1)[:, -w_buf:], w_buf)
    v_s = kv4(jnp.concatenate([prev_v, hs3[:, :ls, COL_SV:COL_SV + 256]], 1)[:, -w_buf:], w_buf)
    return (y_p, y_s, rs_p[None], k_p, v_p, mem4(mk_p), mem4(mv_p), rs_s[None], k_s, v_s)
```

```python
import functools

import jax
import jax.numpy as jnp
from jax import lax
from jax.experimental import pallas as pl
from jax.experimental.pallas import tpu as pltpu

BF = jnp.bfloat16
F32 = jnp.float32

D_MODEL = 1024
RET_HEADS = 4
RET_DK = 128
RET_DV = 256
RET_CHUNK = 128
ROPE_BASE = 10000.0
SWA_HEADS = 16
SWA_KV_HEADS = 4
SWA_GROUP = SWA_HEADS // SWA_KV_HEADS
SWA_DH = 64
WINDOW = 128
SWA_BLOCK = 128
N_MEM = 256
MEM_HEADS = 4
MEM_DH = 256
N_BRANCH = 3
N_EXPERTS = 256
TOP_K = 8
N_GROUPS = 8
GROUP_SIZE = N_EXPERTS // N_GROUPS
TOPK_GROUPS = 4
D_EXPERT = 256
ROUTED_SCALE = 2.5
MOE_BLOCK = 128
ROUTER_TILE = 384
ROW_SUB = D_MODEL // 128
STEP_BLOCKS = 4
LN_EPS = 1e-5
NORM_EPS = 1e-6
NEG = -1e30
PAST_LEN = 16384
SAMPLE_PAD = 8

COL_RQ, COL_RK, COL_RV, COL_RG, COL_SQ, COL_MQ, COL_GL, COL_SK, COL_SV = (
    0, 512, 1024, 2048, 3072, 4096, 5120, 8192, 8448)
D_IN = 8704

VMEM_LIMIT = 56 * 1024 * 1024


def _params(*sem):
    return pltpu.CompilerParams(dimension_semantics=sem, vmem_limit_bytes=VMEM_LIMIT)


def _bdot(a, b):
    return jnp.dot(a.astype(BF), b.astype(BF), preferred_element_type=F32)


def _bdot_nt(a, b):
    return lax.dot_general(a.astype(BF), b.astype(BF), (((1,), (1,)), ((), ())),
                           preferred_element_type=F32)


def _bdot_tn(a, b):
    return lax.dot_general(a.astype(BF), b.astype(BF), (((0,), (0,)), ((), ())),
                           preferred_element_type=F32)


def _layer_norm(z, g, b):
    zc = z - jnp.mean(z, -1, keepdims=True)
    var = jnp.mean(zc * zc, -1, keepdims=True)
    return zc * lax.rsqrt(var + LN_EPS) * g + b


def _load_tiled_rows(ref, idx, m):
    return jnp.concatenate([ref[(*idx, pl.ds(s, m, stride=ROW_SUB), slice(None))]
                            for s in range(ROW_SUB)], axis=1)


def _store_tiled_rows(ref, idx, val):
    m = val.shape[0]
    for s in range(ROW_SUB):
        ref[(*idx, pl.ds(s, m, stride=ROW_SUB), slice(None))] = val[:, s * 128:(s + 1) * 128]


def _proj_kernel(x_ref, w_ref, o_ref, xb_ref):
    @pl.when(pl.program_id(1) == 0)
    def _():
        xb_ref[...] = x_ref[...].astype(BF)

    o_ref[...] = jnp.dot(xb_ref[...], w_ref[...], preferred_element_type=F32)


def _proj(x, w_bf, tm, tn):
    m, k = x.shape
    n = w_bf.shape[1]
    return pl.pallas_call(
        _proj_kernel,
        grid=(m // tm, n // tn),
        in_specs=[pl.BlockSpec((tm, k), lambda i, j: (i, 0)),
                  pl.BlockSpec((k, tn), lambda i, j: (0, j))],
        out_specs=pl.BlockSpec((tm, tn), lambda i, j: (i, j)),
        out_shape=jax.ShapeDtypeStruct((m, n), F32),
        scratch_shapes=[pltpu.VMEM((tm, k), BF)],
        compiler_params=_params("parallel", "arbitrary"),
        name="proj",
    )(x, w_bf)


def _ret_tables(c_real, c_pad):
    lg = jnp.log1p(-jnp.exp2(-5.0 - jnp.arange(RET_HEADS, dtype=F32)))
    idx = jnp.arange(c_pad, dtype=F32)
    real = idx < c_real
    rel = idx[:, None] - idx[None, :]
    intra = jnp.where(rel >= 0, jnp.exp(lg[:, None, None] * jnp.maximum(rel, 0.0)), 0.0)
    intra = jnp.where(real[None, :, None] & real[None, None, :], intra, 0.0)
    q_dec = jnp.where(real[None, :], jnp.exp((idx[None, :] + 1.0) * lg[:, None]), 0.0)
    k_dec = jnp.where(real[None, :], jnp.exp((c_real - 1.0 - idx)[None, :] * lg[:, None]), 0.0)
    c_dec = jnp.exp(c_real * lg)
    bc = lambda t: jnp.broadcast_to(t[:, :, None], (RET_HEADS, c_pad, RET_DK))
    return intra, bc(q_dec), bc(k_dec), c_dec


def _rope_tables(pos):
    half = RET_DK // 2
    inv_freq = 1.0 / (ROPE_BASE ** (jnp.arange(half, dtype=F32) / half))
    ang = pos.astype(F32)[:, None] * inv_freq[None, :]
    cos, sin = jnp.cos(ang), jnp.sin(ang)
    return jnp.concatenate([cos, cos], -1), jnp.concatenate([-sin, sin], -1)


def _ret_kernel(cdec_ref, rq_ref, rk_ref, rv_ref, rg_ref, cos_ref, sin_ref, intra_ref, qdec_ref,
                kdec_ref, s0_ref, o_ref, s_out_ref, s_scr, *, n_chunks):
    c = pl.program_id(1)

    @pl.when(c == 0)
    def _():
        s_scr[...] = s0_ref[...]

    cos2 = cos_ref[...]
    sin2 = sin_ref[...]

    def rot(x):
        return x * cos2 + pltpu.roll(x, RET_DK // 2, 1) * sin2

    for bi in range(rq_ref.shape[0]):
        for h in range(RET_HEADS):
            q = rot(rq_ref[bi, :, h * RET_DK:(h + 1) * RET_DK])
            k = rot(rk_ref[bi, :, h * RET_DK:(h + 1) * RET_DK]) * (RET_DK ** -0.5)
            v = rv_ref[bi, :, h * RET_DV:(h + 1) * RET_DV].astype(BF)
            s_old = s_scr[bi, h]
            a = _bdot_nt(q, k) * intra_ref[h]
            o = _bdot(a, v) + _bdot(q * qdec_ref[h], s_old)
            s_scr[bi, h] = s_old * cdec_ref[h] + _bdot_tn(k * kdec_ref[h], v)
            o = o * lax.rsqrt(jnp.mean(o * o, -1, keepdims=True) + NORM_EPS)
            g = rg_ref[bi, :, h * RET_DV:(h + 1) * RET_DV]
            o_ref[bi, :, h * RET_DV:(h + 1) * RET_DV] = o * (g * jax.nn.sigmoid(g))

    @pl.when(c == n_chunks - 1)
    def _():
        s_out_ref[...] = s_scr[...]


def _retention(h3, pos, state0, c_real, c_pad, tb):
    b, l, _ = h3.shape
    n_chunks = l // c_pad
    intra, qdec, kdec, cdec = _ret_tables(c_real, c_pad)
    cos2, sin2 = _rope_tables(pos)
    full3 = lambda shape: pl.BlockSpec(shape, lambda i, c: (0, 0, 0))
    return pl.pallas_call(
        functools.partial(_ret_kernel, n_chunks=n_chunks),
        grid=(b // tb, n_chunks),
        in_specs=[
            pl.BlockSpec(memory_space=pltpu.SMEM),
            pl.BlockSpec((tb, c_pad, 512), lambda i, c: (i, c, COL_RQ // 512)),
            pl.BlockSpec((tb, c_pad, 512), lambda i, c: (i, c, COL_RK // 512)),
            pl.BlockSpec((tb, c_pad, 1024), lambda i, c: (i, c, COL_RV // 1024)),
            pl.BlockSpec((tb, c_pad, 1024), lambda i, c: (i, c, COL_RG // 1024)),
            pl.BlockSpec((c_pad, RET_DK), lambda i, c: (c, 0)),
            pl.BlockSpec((c_pad, RET_DK), lambda i, c: (c, 0)),
            full3((RET_HEADS, c_pad, c_pad)),
            full3((RET_HEADS, c_pad, RET_DK)),
            full3((RET_HEADS, c_pad, RET_DK)),
            pl.BlockSpec((tb, RET_HEADS, RET_DK, RET_DV), lambda i, c: (i, 0, 0, 0)),
        ],
        out_specs=[
            pl.BlockSpec((tb, c_pad, 1024), lambda i, c: (i, c, 0)),
            pl.BlockSpec((tb, RET_HEADS, RET_DK, RET_DV), lambda i, c: (i, 0, 0, 0)),
        ],
        out_shape=[jax.ShapeDtypeStruct((b, l, 1024), F32),
                   jax.ShapeDtypeStruct((b, RET_HEADS, RET_DK, RET_DV), F32)],
        scratch_shapes=[pltpu.VMEM((tb, RET_HEADS, RET_DK, RET_DV), F32)],
        compiler_params=_params("parallel", "arbitrary"),
        name="retention",
    )(cdec, h3, h3, h3, h3, cos2, sin2, intra, qdec, kdec, state0)


def _sink_softmax(s, sink):
    m = jnp.maximum(jnp.max(s, -1, keepdims=True), sink)
    p = jnp.exp(s - m)
    return p / (jnp.sum(p, -1, keepdims=True) + jnp.exp(sink - m))


def _swa_prompt_kernel(sinks_ref, q_ref, kp_ref, kc_ref, vp_ref, vc_ref, o_ref):
    n = pl.program_id(0)
    qi = lax.broadcasted_iota(jnp.int32, (SWA_BLOCK, 2 * SWA_BLOCK), 0)
    kj = lax.broadcasted_iota(jnp.int32, (SWA_BLOCK, 2 * SWA_BLOCK), 1)
    rel = SWA_BLOCK + qi - kj
    valid = (rel >= 0) & (rel <= WINDOW) & ((kj >= SWA_BLOCK) | (n > 0))
    for h in range(SWA_KV_HEADS):
        sl = slice(h * SWA_DH, (h + 1) * SWA_DH)
        k2 = jnp.concatenate([kp_ref[:, sl], kc_ref[:, sl]], 0).astype(BF)
        v2 = jnp.concatenate([vp_ref[:, sl], vc_ref[:, sl]], 0).astype(BF)
        for g in range(SWA_GROUP):
            hq = h * SWA_GROUP + g
            qsl = slice(hq * SWA_DH, (hq + 1) * SWA_DH)
            s = _bdot_nt(q_ref[:, qsl], k2) * (SWA_DH ** -0.5)
            s = jnp.where(valid, s, NEG)
            p = _sink_softmax(s, sinks_ref[hq])
            o_ref[:, qsl] = _bdot(p, v2)


def _swa_prompt(h2, sinks):
    t = h2.shape[0]
    nb = t // SWA_BLOCK
    prev = lambda col: (lambda n: (jnp.maximum(n - 1, 0), col))
    cur = lambda col: (lambda n: (n, col))
    ck, cv = COL_SK // 256, COL_SV // 256
    return pl.pallas_call(
        _swa_prompt_kernel,
        grid=(nb,),
        in_specs=[
            pl.BlockSpec(memory_space=pltpu.SMEM),
            pl.BlockSpec((SWA_BLOCK, 1024), cur(COL_SQ // 1024)),
            pl.BlockSpec((SWA_BLOCK, 256), prev(ck)),
            pl.BlockSpec((SWA_BLOCK, 256), cur(ck)),
            pl.BlockSpec((SWA_BLOCK, 256), prev(cv)),
            pl.BlockSpec((SWA_BLOCK, 256), cur(cv)),
        ],
        out_specs=pl.BlockSpec((SWA_BLOCK, 1024), lambda n: (n, 0)),
        out_shape=jax.ShapeDtypeStruct((t, 1024), F32),
        compiler_params=_params("parallel"),
        name="swa_prompt",
    )(sinks, h2, h2, h2, h2, h2)


def _swa_sample_kernel(sinks_ref, q_ref, kn_ref, vn_ref, kp_ref, vp_ref, o_ref, *, n_new):
    tb = q_ref.shape[0]
    w = kp_ref.shape[1]
    p_ = SAMPLE_PAD
    rows = SWA_GROUP * p_
    qi = lax.broadcasted_iota(jnp.int32, (tb, rows, w), 1) % p_
    kj = lax.broadcasted_iota(jnp.int32, (tb, rows, w), 2)
    rel_prev = w + qi - kj
    valid_prev = (rel_prev >= 0) & (rel_prev <= WINDOW)
    qn = lax.broadcasted_iota(jnp.int32, (tb, rows, p_), 1) % p_
    kn = lax.broadcasted_iota(jnp.int32, (tb, rows, p_), 2)
    valid_new = (qn - kn >= 0) & (qn - kn <= WINDOW) & (kn < n_new)
    row_head = lax.broadcasted_iota(jnp.int32, (1, rows, 1), 1) // p_
    bdot = lambda eq, a, b: jnp.einsum(eq, a.astype(BF), b.astype(BF), preferred_element_type=F32)
    for h in range(SWA_KV_HEADS):
        sl = slice(h * SWA_DH, (h + 1) * SWA_DH)
        kp, vp = kp_ref[:, :, sl], vp_ref[:, :, sl]
        kn_h, vn_h = kn_ref[:, :, sl], vn_ref[:, :, sl]
        heads = [h * SWA_GROUP + g for g in range(SWA_GROUP)]
        q = jnp.concatenate([q_ref[:, :, hq * SWA_DH:(hq + 1) * SWA_DH] for hq in heads], axis=1)
        sink = jnp.zeros((1, rows, 1), F32)
        for g, hq in enumerate(heads):
            sink = jnp.where(row_head == g, sinks_ref[hq], sink)
        sp = bdot('bqd,bkd->bqk', q, kp) * (SWA_DH ** -0.5)
        sn = bdot('bqd,bkd->bqk', q, kn_h) * (SWA_DH ** -0.5)
        sp = jnp.where(valid_prev, sp, NEG)
        sn = jnp.where(valid_new, sn, NEG)
        m = jnp.maximum(jnp.maximum(jnp.max(sp, -1, keepdims=True),
                                    jnp.max(sn, -1, keepdims=True)), sink)
        pp = jnp.exp(sp - m)
        pn = jnp.exp(sn - m)
        den = jnp.sum(pp, -1, keepdims=True) + jnp.sum(pn, -1, keepdims=True) + jnp.exp(sink - m)
        o = bdot('bqk,bkd->bqd', pp / den, vp) + bdot('bqk,bkd->bqd', pn / den, vn_h)
        for g, hq in enumerate(heads):
            o_ref[:, :, hq * SWA_DH:(hq + 1) * SWA_DH] = o[:, g * p_:(g + 1) * p_, :]


def _swa_sample(h3, prev_k, prev_v, sinks, n_new, tb=8):
    b = h3.shape[0]
    w = prev_k.shape[1]
    return pl.pallas_call(
        functools.partial(_swa_sample_kernel, n_new=n_new),
        grid=(b // tb,),
        in_specs=[
            pl.BlockSpec(memory_space=pltpu.SMEM),
            pl.BlockSpec((tb, SAMPLE_PAD, 1024), lambda i: (i, 0, COL_SQ // 1024)),
            pl.BlockSpec((tb, SAMPLE_PAD, 256), lambda i: (i, 0, COL_SK // 256)),
            pl.BlockSpec((tb, SAMPLE_PAD, 256), lambda i: (i, 0, COL_SV // 256)),
            pl.BlockSpec((tb, w, 256), lambda i: (i, 0, 0)),
            pl.BlockSpec((tb, w, 256), lambda i: (i, 0, 0)),
        ],
        out_specs=pl.BlockSpec((tb, SAMPLE_PAD, 1024), lambda i: (i, 0, 0)),
        out_shape=jax.ShapeDtypeStruct((b, SAMPLE_PAD, 1024), F32),
        compiler_params=_params("parallel"),
        name="swa_sample",
    )(sinks, h3, h3, h3, prev_k, prev_v)


def _mem_kernel(q_ref, mk_ref, mv_ref, o_ref):
    for h in range(MEM_HEADS):
        sl = slice(h * MEM_DH, (h + 1) * MEM_DH)
        s = _bdot_nt(q_ref[0, :, sl], mk_ref[0, :, sl]) * (MEM_DH ** -0.5)
        m = jnp.max(s, -1, keepdims=True)
        e = jnp.exp(s - m)
        p = e / jnp.sum(e, -1, keepdims=True)
        o_ref[0, :, sl] = _bdot(p, mv_ref[0, :, sl])


def _mem_attend(h3, mk, mv, tl):
    b, l, _ = h3.shape
    return pl.pallas_call(
        _mem_kernel,
        grid=(b, l // tl),
        in_specs=[
            pl.BlockSpec((1, tl, 1024), lambda i, j: (i, j, COL_MQ // 1024)),
            pl.BlockSpec((1, N_MEM, 1024), lambda i, j: (i, 0, 0)),
            pl.BlockSpec((1, N_MEM, 1024), lambda i, j: (i, 0, 0)),
        ],
        out_specs=pl.BlockSpec((1, tl, 1024), lambda i, j: (i, j, 0)),
        out_shape=jax.ShapeDtypeStruct((b, l, 1024), F32),
        compiler_params=_params("parallel", "parallel"),
        name="mem_attend",
    )(h3, mk, mv)


def _merge_kernel(ro_ref, so_ref, mo_ref, g0_ref, g1_ref, g2_ref, x_ref, wb_ref, wo_ref, g_ref,
                  b_ref, o_ref, *, alpha):
    acc = None
    for n, (br, gl) in enumerate(((ro_ref, g0_ref), (so_ref, g1_ref), (mo_ref, g2_ref))):
        term = jax.nn.sigmoid(gl[...]) * jnp.dot(br[...].astype(BF), wb_ref[n],
                                                 preferred_element_type=F32)
        acc = term if acc is None else acc + term
    a = jnp.dot(acc.astype(BF), wo_ref[...], preferred_element_type=F32)
    _store_tiled_rows(o_ref, (), _layer_norm(alpha * x_ref[...] + a, g_ref[...], b_ref[...]))


def _merge(ro, so, mo, h2, x2, wb_bf, wo_bf, g, b, alpha, tm):
    t = x2.shape[0]
    tile = lambda col: pl.BlockSpec((tm, 1024), lambda i: (i, col))
    gl0 = COL_GL // 1024
    return pl.pallas_call(
        functools.partial(_merge_kernel, alpha=alpha),
        grid=(t // tm,),
        in_specs=[tile(0), tile(0), tile(0), tile(gl0), tile(gl0 + 1), tile(gl0 + 2), tile(0),
                  pl.BlockSpec((N_BRANCH, 1024, 1024), lambda i: (0, 0, 0)),
                  pl.BlockSpec((1024, 1024), lambda i: (0, 0)),
                  pl.BlockSpec((1, 1024), lambda i: (0, 0)),
                  pl.BlockSpec((1, 1024), lambda i: (0, 0))],
        out_specs=pl.BlockSpec((tm * ROW_SUB, 128), lambda i: (i, 0)),
        out_shape=jax.ShapeDtypeStruct((t * ROW_SUB, 128), F32),
        compiler_params=_params("parallel"),
        name="merge_ln1",
    )(ro, so, mo, h2, h2, h2, x2, wb_bf, wo_bf, g, b)


def _first_index_of_max(v, iota, big, axes):
    m = jnp.max(v, axis=axes, keepdims=True)
    idx = jnp.min(jnp.where(v == m, iota, big), axis=axes, keepdims=True)
    return m, idx


def _router_kernel(x_ref, wr_ref, bias_ref, eidx_ref, ew_ref):
    tt = x_ref.shape[0] // ROW_SUB
    x = _load_tiled_rows(x_ref, (), tt).astype(BF)
    logits = lax.dot_general(wr_ref[...], x, (((1,), (1,)), ((), ())),
                             preferred_element_type=F32)
    s = jax.nn.sigmoid(logits).reshape(N_GROUPS, GROUP_SIZE, tt)
    sb = s + bias_ref[...].reshape(N_GROUPS, GROUP_SIZE, 1)
    ninf = -jnp.inf
    r_iota = lax.broadcasted_iota(jnp.int32, sb.shape, 1)
    m1, i1 = _first_index_of_max(sb, r_iota, GROUP_SIZE, 1)
    m2 = jnp.max(jnp.where(r_iota == i1, ninf, sb), axis=1, keepdims=True)
    gsc = (m1 + m2).reshape(N_GROUPS, tt)
    g_iota = lax.broadcasted_iota(jnp.int32, gsc.shape, 0)
    gmask = jnp.zeros(gsc.shape, jnp.bool_)
    for _ in range(TOPK_GROUPS):
        _, gi = _first_index_of_max(gsc, g_iota, N_GROUPS, 0)
        hit = g_iota == gi
        gmask = gmask | hit
        gsc = jnp.where(hit, ninf, gsc)
    cand = jnp.where(gmask.reshape(N_GROUPS, 1, tt), sb, ninf)
    e_iota = lax.broadcasted_iota(jnp.int32, sb.shape, 0) * GROUP_SIZE + r_iota
    idxs, ws = [], []
    for _ in range(TOP_K):
        _, ei = _first_index_of_max(cand, e_iota, N_EXPERTS, (0, 1))
        hit = e_iota == ei
        idxs.append(ei.reshape(1, tt))
        ws.append(jnp.sum(jnp.where(hit, s, 0.0), axis=(0, 1)).reshape(1, tt))
        cand = jnp.where(hit, ninf, cand)
    w = jnp.concatenate(ws, 0)
    eidx_ref[...] = jnp.concatenate(idxs, 0)
    ew_ref[...] = w / jnp.sum(w, 0, keepdims=True) * ROUTED_SCALE


def _router(xt, wr_t_bf, bias_col, tt):
    t = xt.shape[0] // ROW_SUB
    return pl.pallas_call(
        _router_kernel,
        grid=(t // tt,),
        in_specs=[pl.BlockSpec((tt * ROW_SUB, 128), lambda i: (i, 0)),
                  pl.BlockSpec((N_EXPERTS, 1024), lambda i: (0, 0)),
                  pl.BlockSpec((N_EXPERTS, 1), lambda i: (0, 0))],
        out_specs=[pl.BlockSpec((TOP_K, tt), lambda i: (0, i)),
                   pl.BlockSpec((TOP_K, tt), lambda i: (0, i))],
        out_shape=[jax.ShapeDtypeStruct((TOP_K, t), jnp.int32),
                   jax.ShapeDtypeStruct((TOP_K, t), F32)],
        compiler_params=_params("parallel"),
        name="router",
    )(xt, wr_t_bf, bias_col)


IDX_BITS = 18


def _dispatch(eidx, ew, t):
    a = t * TOP_K
    assert a < (1 << IDX_BITS)
    nblk = -(-a // MOE_BLOCK) + N_EXPERTS
    assert nblk % STEP_BLOCKS == 0 and STEP_BLOCKS % 2 == 0
    flat_e = eidx.reshape(-1)
    skey = jnp.sort(flat_e * (1 << IDX_BITS) + jnp.arange(a, dtype=jnp.int32))
    si = skey & ((1 << IDX_BITS) - 1)
    experts = jnp.arange(N_EXPERTS, dtype=jnp.int32)
    counts = jnp.sum((flat_e[None, :] == experts[:, None]).astype(jnp.int32), axis=1)
    grp_start = jnp.cumsum(counts) - counts
    padded = (counts + MOE_BLOCK - 1) // MOE_BLOCK * MOE_BLOCK
    pad_end = jnp.cumsum(padded)
    pad_start = pad_end - padded
    blk_first = jnp.arange(nblk, dtype=jnp.int32) * MOE_BLOCK
    blk_exp = jnp.minimum(jnp.sum((pad_end[None, :] <= blk_first[:, None]).astype(jnp.int32), axis=1),
                          N_EXPERTS - 1)
    off = (jnp.arange(nblk * MOE_BLOCK, dtype=jnp.int32).reshape(nblk, MOE_BLOCK)
           - pad_start[blk_exp][:, None])
    valid = off < counts[blk_exp][:, None]
    src = jnp.clip(grp_start[blk_exp][:, None] + off, 0, a - 1)
    row_si = si[src]
    row_tok = jnp.where(valid, row_si >> 3, t).astype(jnp.int32)
    row_w = jnp.where(valid, ew.reshape(-1)[row_si], 0.0)
    nused = (pad_end[-1] // MOE_BLOCK).astype(jnp.int32).reshape(1)
    gidx, nxt = _group_tables(counts, blk_exp)
    step_rows = STEP_BLOCKS * MOE_BLOCK
    return (row_tok.reshape(-1), row_w.reshape(nblk // STEP_BLOCKS, 1, step_rows), blk_exp, nused,
            gidx, nxt)


def _group_tables(counts, blk_exp):
    nonempty = counts > 0
    gidx = (jnp.cumsum(nonempty.astype(jnp.int32)) - 1)[blk_exp]
    experts = jnp.arange(N_EXPERTS, dtype=jnp.int32)
    cand = jnp.where(nonempty, experts, N_EXPERTS)
    later = lax.cummin(cand, axis=0, reverse=True)
    nxt = jnp.concatenate([later[1:], jnp.full((1,), N_EXPERTS, jnp.int32)])
    nxt = jnp.where(nxt >= N_EXPERTS, -1, nxt)
    return gidx.astype(jnp.int32), nxt[blk_exp].astype(jnp.int32)


def _weight_copies(hbm_refs, buf_ref, sem_ref, e, slot):
    return [pltpu.make_async_copy(h.at[e], buf_ref.at[slot, k], sem_ref.at[slot, k])
            for k, h in enumerate(hbm_refs)]


def _stage_weights(b, be_ref, nused_ref, gidx_ref, nxt_ref, hbm_refs, buf_ref, sem_ref, cache_refs):
    first = ((b == 0) | (be_ref[b] != be_ref[jnp.maximum(b - 1, 0)])) & (b < nused_ref[0])

    @pl.when(first)
    def _():
        slot = gidx_ref[b] % 2

        @pl.when(b == 0)
        def _():
            for c in _weight_copies(hbm_refs, buf_ref, sem_ref, be_ref[0], 0):
                c.start()

        for c in _weight_copies(hbm_refs, buf_ref, sem_ref, be_ref[b], slot):
            c.wait()
        nxt = nxt_ref[b]

        @pl.when(nxt >= 0)
        def _():
            for c in _weight_copies(hbm_refs, buf_ref, sem_ref, nxt, 1 - slot):
                c.start()

        for k, cache in enumerate(cache_refs):
            cache[...] = buf_ref[slot, k].astype(BF)


def _gather_rows(tok_ref, b, x_ref, xg_ref, slot):
    for r in range(MOE_BLOCK):
        xg_ref[slot, r * ROW_SUB:(r + 1) * ROW_SUB, :] = x_ref[tok_ref[b * MOE_BLOCK + r]]


def _up_kernel(be_ref, nused_ref, gidx_ref, nxt_ref, tok_ref, x_ref, wg_hbm, wu_hbm, act_ref,
               xg_ref, wbuf_ref, wgb_ref, wub_ref, sem_ref):
    i = pl.program_id(0)
    nblk = pl.num_programs(0) * STEP_BLOCKS
    b0 = i * STEP_BLOCKS

    @pl.when(i == 0)
    def _():
        _gather_rows(tok_ref, 0, x_ref, xg_ref, 0)

    @pl.when(b0 < nused_ref[0])
    def _():
        for j in range(STEP_BLOCKS):
            b = b0 + j
            _stage_weights(b, be_ref, nused_ref, gidx_ref, nxt_ref, (wg_hbm, wu_hbm), wbuf_ref,
                           sem_ref, (wgb_ref, wub_ref))
            _gather_rows(tok_ref, jnp.minimum(b + 1, nblk - 1), x_ref, xg_ref, (j + 1) % 2)
            x = _load_tiled_rows(xg_ref, (j % 2,), MOE_BLOCK).astype(BF)
            g = jnp.dot(x, wgb_ref[...], preferred_element_type=F32)
            u = jnp.dot(x, wub_ref[...], preferred_element_type=F32)
            act_ref[j * MOE_BLOCK:(j + 1) * MOE_BLOCK, :] = ((g * jax.nn.sigmoid(g)) * u).astype(BF)

    @pl.when(b0 >= nused_ref[0])
    def _():
        act_ref[...] = jnp.zeros_like(act_ref)


def _experts_up(x, row_tok, blk_exp, nused, gidx, nxt, w_gate, w_up):
    nblk = blk_exp.shape[0]
    step_rows = STEP_BLOCKS * MOE_BLOCK
    return pl.pallas_call(
        _up_kernel,
        grid_spec=pltpu.PrefetchScalarGridSpec(
            num_scalar_prefetch=5,
            grid=(nblk // STEP_BLOCKS,),
            in_specs=[pl.BlockSpec(memory_space=pltpu.VMEM),
                      pl.BlockSpec(memory_space=pl.ANY),
                      pl.BlockSpec(memory_space=pl.ANY)],
            out_specs=pl.BlockSpec((step_rows, D_EXPERT), lambda i, *_: (i, 0)),
            scratch_shapes=[pltpu.VMEM((2, MOE_BLOCK * ROW_SUB, 128), F32),
                            pltpu.VMEM((2, 2, D_MODEL, D_EXPERT), F32),
                            pltpu.VMEM((D_MODEL, D_EXPERT), BF),
                            pltpu.VMEM((D_MODEL, D_EXPERT), BF),
                            pltpu.SemaphoreType.DMA((2, 2))],
        ),
        out_shape=jax.ShapeDtypeStruct((nblk * MOE_BLOCK, D_EXPERT), BF),
        compiler_params=_params("arbitrary"),
        name="experts_up",
    )(blk_exp, nused, gidx, nxt, row_tok, x, w_gate, w_up)


SCATTER_GROUP = 8


def _row_to_column(row):
    n = row.shape[1]
    eye = lax.broadcasted_iota(jnp.int32, (n, n), 0) == lax.broadcasted_iota(jnp.int32, (n, n), 1)
    return jnp.sum(jnp.where(eye, jnp.broadcast_to(row, (n, n)), 0.0), axis=1, keepdims=True)


def _scatter_add_rows(tok_ref, b, y_ref, yb_ref, slot):
    for r0 in range(0, MOE_BLOCK, SCATTER_GROUP):
        rs = range(r0, r0 + SCATTER_GROUP)
        toks = [tok_ref[b * MOE_BLOCK + r] for r in rs]
        new = [y_ref[t] + yb_ref[slot, r * ROW_SUB:(r + 1) * ROW_SUB, :] for r, t in zip(rs, toks)]
        for t, v in zip(toks, new):
            y_ref[t] = v


def _down_kernel(be_ref, nused_ref, gidx_ref, nxt_ref, tok_ref, act_ref, rw_ref, wd_hbm, y_ref,
                 yb_ref, wbuf_ref, wdb_ref, sem_ref):
    i = pl.program_id(0)
    nblk = pl.num_programs(0) * STEP_BLOCKS
    b0 = i * STEP_BLOCKS

    @pl.when(i == 0)
    def _():
        y_ref[...] = jnp.zeros_like(y_ref)
        yb_ref[1] = jnp.zeros(yb_ref.shape[1:], F32)

    @pl.when(b0 <= nused_ref[0])
    def _():
        for j in range(STEP_BLOCKS):
            b = b0 + j
            rows = slice(j * MOE_BLOCK, (j + 1) * MOE_BLOCK)
            _stage_weights(b, be_ref, nused_ref, gidx_ref, nxt_ref, (wd_hbm,), wbuf_ref, sem_ref,
                           (wdb_ref,))
            yb = jnp.dot(act_ref[rows, :], wdb_ref[...],
                         preferred_element_type=F32) * _row_to_column(rw_ref[0, :, rows])
            _store_tiled_rows(yb_ref, (j % 2,), yb)
            _scatter_add_rows(tok_ref, jnp.maximum(b - 1, 0), y_ref, yb_ref, (j + 1) % 2)

    @pl.when((i == pl.num_programs(0) - 1) & (nused_ref[0] >= nblk))
    def _():
        _scatter_add_rows(tok_ref, nblk - 1, y_ref, yb_ref, (STEP_BLOCKS - 1) % 2)


def _experts_down(act, row_tok, row_w, blk_exp, nused, gidx, nxt, w_down, t_rows):
    nblk = blk_exp.shape[0]
    step_rows = STEP_BLOCKS * MOE_BLOCK
    return pl.pallas_call(
        _down_kernel,
        grid_spec=pltpu.PrefetchScalarGridSpec(
            num_scalar_prefetch=5,
            grid=(nblk // STEP_BLOCKS,),
            in_specs=[pl.BlockSpec((step_rows, D_EXPERT), lambda i, *_: (i, 0)),
                      pl.BlockSpec((1, 1, step_rows), lambda i, *_: (i, 0, 0)),
                      pl.BlockSpec(memory_space=pl.ANY)],
            out_specs=pl.BlockSpec(memory_space=pltpu.VMEM),
            scratch_shapes=[pltpu.VMEM((2, MOE_BLOCK * ROW_SUB, 128), F32),
                            pltpu.VMEM((2, 1, D_EXPERT, D_MODEL), F32),
                            pltpu.VMEM((D_EXPERT, D_MODEL), BF),
                            pltpu.SemaphoreType.DMA((2, 1))],
        ),
        out_shape=jax.ShapeDtypeStruct((t_rows, ROW_SUB, 128), F32),
        compiler_params=_params("arbitrary"),
        name="experts_down",
    )(blk_exp, nused, gidx, nxt, row_tok, act, row_w, w_down)


def _final_kernel(h_ref, yr_ref, wg_ref, wu_ref, wd_ref, g_ref, b_ref, o_ref, *, alpha):
    h = _load_tiled_rows(h_ref, (), o_ref.shape[0])
    hb = h.astype(BF)
    g = jnp.dot(hb, wg_ref[...], preferred_element_type=F32)
    u = jnp.dot(hb, wu_ref[...], preferred_element_type=F32)
    shared = jnp.dot(((g * jax.nn.sigmoid(g)) * u).astype(BF), wd_ref[...],
                     preferred_element_type=F32)
    f = _load_tiled_rows(yr_ref, (), h.shape[0]) + shared
    o_ref[...] = _layer_norm(alpha * h + f, g_ref[...], b_ref[...])


def _final(ht, yr, wg_bf, wu_bf, wd_bf, g, b, alpha, tm):
    t = ht.shape[0] // ROW_SUB
    tiled = pl.BlockSpec((tm * ROW_SUB, 128), lambda i: (i, 0))
    full = lambda shape: pl.BlockSpec(shape, lambda i: (0, 0))
    return pl.pallas_call(
        functools.partial(_final_kernel, alpha=alpha),
        grid=(t // tm,),
        in_specs=[tiled, tiled, full((1024, 256)), full((1024, 256)), full((256, 1024)),
                  full((1, 1024)), full((1, 1024))],
        out_specs=pl.BlockSpec((tm, 1024), lambda i: (i, 0)),
        out_shape=jax.ShapeDtypeStruct((t, 1024), F32),
        compiler_params=_params("parallel"),
        name="shared_ln2",
    )(ht, yr, wg_bf, wu_bf, wd_bf, g, b)


def _moe_ln2(ht, tables, w_gate, w_up, w_down, wsg_bf, wsu_bf, wsd_bf, g, b, alpha):
    t = ht.shape[0] // ROW_SUB
    row_tok, row_w, blk_exp, nused, gidx, nxt = tables
    act = _experts_up(ht.reshape(t, ROW_SUB, 128), jnp.minimum(row_tok, t - 1), blk_exp, nused, gidx,
                      nxt, w_gate, w_up)
    yr = _experts_down(act, row_tok, row_w, blk_exp, nused, gidx, nxt, w_down, t + 8)
    return _final(ht, yr.reshape((t + 8) * ROW_SUB, 128), wsg_bf, wsu_bf, wsd_bf, g, b, alpha, 256)


def _permute_w_in(w_in):
    rq, rk, rv, rg, sq, sk, sv, mq, gl = jnp.split(
        w_in, [512, 1024, 2048, 3072, 4096, 4352, 4608, 5632], axis=-1)
    return jnp.concatenate([rq, rk, rv, rg, sq, mq, gl, sk, sv], -1)


def kernel(x_prompt, x_sample, mem_prompt, cache_ret_state, cache_swa_k, cache_swa_v, cache_mem_k,
           cache_mem_v, w_in, swa_sinks, w_mem_kv, w_branch, w_o, ln1_g, ln1_b, w_router,
           router_bias, w_gate_e, w_up_e, w_down_e, w_sh_gate, w_sh_up, w_sh_down, ln2_g, ln2_b):
    depth = w_in.shape[0]
    assert depth == 1
    alpha = (2.0 * depth) ** 0.25
    bp, lp, d = x_prompt.shape
    bs, ls, _ = x_sample.shape
    l = 0

    w_in_bf = _permute_w_in(w_in[l]).astype(BF)
    sinks = swa_sinks[l]
    wb_bf = w_branch[l].astype(BF)
    wo_bf = w_o[l].astype(BF)
    g1, b1 = ln1_g[l].reshape(1, d), ln1_b[l].reshape(1, d)
    g2, b2 = ln2_g[l].reshape(1, d), ln2_b[l].reshape(1, d)

    tp = bp * lp
    xp2 = x_prompt.reshape(tp, d)
    hp2 = _proj(xp2, w_in_bf, 1024, 512)
    hp3 = hp2.reshape(bp, lp, D_IN)
    mkv = _proj(mem_prompt.reshape(bp * N_MEM, d), w_mem_kv[l].astype(BF), N_MEM, 512)
    mk_p, mv_p = mkv[:, :1024].reshape(bp, N_MEM, 1024), mkv[:, 1024:].reshape(bp, N_MEM, 1024)
    rs0 = jnp.zeros((bp, RET_HEADS, RET_DK, RET_DV), F32)
    ro_p, rs_p = _retention(hp3, jnp.arange(lp), rs0, RET_CHUNK, RET_CHUNK, 1)
    so_p = _swa_prompt(hp2, sinks)
    mo_p = _mem_attend(hp3, mk_p, mv_p, 256)
    hmid_p = _merge(ro_p.reshape(tp, d), so_p, mo_p.reshape(tp, d), hp2, xp2, wb_bf, wo_bf,
                    g1, b1, alpha, 256)

    ts = bs * SAMPLE_PAD
    xs3 = jnp.pad(x_sample, ((0, 0), (0, SAMPLE_PAD - ls), (0, 0)))
    xs2 = xs3.reshape(ts, d)
    hs2 = _proj(xs2, w_in_bf, ts, 512)
    hs3 = hs2.reshape(bs, SAMPLE_PAD, D_IN)
    pos_s = PAST_LEN + jnp.arange(SAMPLE_PAD)
    ro_s, rs_s = _retention(hs3, pos_s, cache_ret_state.reshape(bs, RET_HEADS, RET_DK, RET_DV), ls,
                            SAMPLE_PAD, 4)
    w_buf = cache_swa_k.shape[2]
    prev_k = cache_swa_k.reshape(bs, w_buf, SWA_KV_HEADS * SWA_DH)
    prev_v = cache_swa_v.reshape(bs, w_buf, SWA_KV_HEADS * SWA_DH)
    so_s = _swa_sample(hs3, prev_k, prev_v, sinks, ls)
    mo_s = _mem_attend(hs3, cache_mem_k.reshape(bs, N_MEM, 1024),
                       cache_mem_v.reshape(bs, N_MEM, 1024), SAMPLE_PAD)
    hmid_s = _merge(ro_s.reshape(ts, d), so_s.reshape(ts, d), mo_s.reshape(ts, d), hs2, xs2,
                    wb_bf, wo_bf, g1, b1, alpha, 256)
    hmid_s = hmid_s.reshape(bs, SAMPLE_PAD, ROW_SUB, 128)[:, :ls].reshape(bs * ls * ROW_SUB, 128)

    t_all = tp + bs * ls
    assert t_all % (2 * ROUTER_TILE) == 0 and t_all // 2 <= tp
    th = t_all // 2
    groups = (hmid_p[:th * ROW_SUB], jnp.concatenate([hmid_p[th * ROW_SUB:], hmid_s], 0))
    wr_t_bf, bias_col = w_router[l].T.astype(BF), router_bias[l].reshape(N_EXPERTS, 1)
    routed = [_router(ht, wr_t_bf, bias_col, ROUTER_TILE) for ht in groups]
    tables = jax.vmap(functools.partial(_dispatch, t=th))(
        jnp.stack([e.T for e, _ in routed]), jnp.stack([w.T for _, w in routed]))
    y_a, y_b = (
        _moe_ln2(ht, jax.tree.map(lambda a: a[i], tables), w_gate_e[l], w_up_e[l], w_down_e[l],
                 w_sh_gate[l].astype(BF), w_sh_up[l].astype(BF), w_sh_down[l].astype(BF), g2, b2, alpha)
        for i, ht in enumerate(groups))

    y_p = jnp.concatenate([y_a, y_b[:tp - th]], 0).reshape(bp, lp, d)
    y_s = y_b[tp - th:].reshape(bs, ls, d)
    kv4 = lambda a, n: a.reshape(1, a.shape[0], n, SWA_KV_HEADS, SWA_DH)
    k_p = kv4(hp3[:, lp - WINDOW:, COL_SK:COL_SK + 256], WINDOW)
    v_p = kv4(hp3[:, lp - WINDOW:, COL_SV:COL_SV + 256], WINDOW)
    mem4 = lambda a: a.reshape(1, bp, N_MEM, MEM_HEADS, MEM_DH)
    k_s = kv4(jnp.concatenate([prev_k, hs3[:, :ls, COL_SK:COL_SK + 256]], 1)[:, -w_buf:], w_buf)
    v_s = kv4(jnp.concatenate([prev_v, hs3[:, :ls, COL_SV:COL_SV + 256]], 1)[:, -w_buf:], w_buf)
    return (y_p, y_s, rs_p[None], k_p, v_p, mem4(mk_p), mem4(mv_p), rs_s[None], k_s, v_s)
```

```python
import functools

import jax
import jax.numpy as jnp
from jax import lax
from jax.experimental import pallas as pl
from jax.experimental.pallas import tpu as pltpu

BF = jnp.bfloat16
F32 = jnp.float32

D_MODEL = 1024
RET_HEADS = 4
RET_DK = 128
RET_DV = 256
RET_CHUNK = 128
ROPE_BASE = 10000.0
SWA_HEADS = 16
SWA_KV_HEADS = 4
SWA_GROUP = SWA_HEADS // SWA_KV_HEADS
SWA_DH = 64
WINDOW = 128
SWA_BLOCK = 128
N_MEM = 256
MEM_HEADS = 4
MEM_DH = 256
N_BRANCH = 3
N_EXPERTS = 256
TOP_K = 8
N_GROUPS = 8
GROUP_SIZE = N_EXPERTS // N_GROUPS
TOPK_GROUPS = 4
D_EXPERT = 256
ROUTED_SCALE = 2.5
MOE_BLOCK = 128
ROUTER_TILE = 384
ROW_SUB = D_MODEL // 128
STEP_BLOCKS = 4
LN_EPS = 1e-5
NORM_EPS = 1e-6
NEG = -1e30
PAST_LEN = 16384
SAMPLE_PAD = 8

COL_RQ, COL_RK, COL_RV, COL_RG, COL_SQ, COL_MQ, COL_GL, COL_SK, COL_SV = (
    0, 512, 1024, 2048, 3072, 4096, 5120, 8192, 8448)
D_IN = 8704
PROJ_TN = D_IN // 4

VMEM_LIMIT = 56 * 1024 * 1024


def _params(*sem):
    return pltpu.CompilerParams(dimension_semantics=sem, vmem_limit_bytes=VMEM_LIMIT)


def _bdot(a, b):
    return jnp.dot(a.astype(BF), b.astype(BF), preferred_element_type=F32)


def _bdot_nt(a, b):
    return lax.dot_general(a.astype(BF), b.astype(BF), (((1,), (1,)), ((), ())),
                           preferred_element_type=F32)


def _bdot_tn(a, b):
    return lax.dot_general(a.astype(BF), b.astype(BF), (((0,), (0,)), ((), ())),
                           preferred_element_type=F32)


def _layer_norm(z, g, b):
    zc = z - jnp.mean(z, -1, keepdims=True)
    var = jnp.mean(zc * zc, -1, keepdims=True)
    return zc * lax.rsqrt(var + LN_EPS) * g + b


def _load_tiled_rows(ref, idx, m):
    return jnp.concatenate([ref[(*idx, pl.ds(s, m, stride=ROW_SUB), slice(None))]
                            for s in range(ROW_SUB)], axis=1)


def _store_tiled_rows(ref, idx, val):
    m = val.shape[0]
    for s in range(ROW_SUB):
        ref[(*idx, pl.ds(s, m, stride=ROW_SUB), slice(None))] = val[:, s * 128:(s + 1) * 128]


def _proj_kernel(x_ref, w_ref, o_ref, xb_ref):
    @pl.when(pl.program_id(1) == 0)
    def _():
        xb_ref[...] = x_ref[...].astype(BF)

    o_ref[...] = jnp.dot(xb_ref[...], w_ref[...], preferred_element_type=F32)


def _proj(x, w_bf, tm, tn):
    m, k = x.shape
    n = w_bf.shape[1]
    return pl.pallas_call(
        _proj_kernel,
        grid=(m // tm, n // tn),
        in_specs=[pl.BlockSpec((tm, k), lambda i, j: (i, 0)),
                  pl.BlockSpec((k, tn), lambda i, j: (0, j))],
        out_specs=pl.BlockSpec((tm, tn), lambda i, j: (i, j)),
        out_shape=jax.ShapeDtypeStruct((m, n), F32),
        scratch_shapes=[pltpu.VMEM((tm, k), BF)],
        compiler_params=_params("parallel", "arbitrary"),
        name="proj",
    )(x, w_bf)


def _ret_tables(c_real, c_pad):
    lg = jnp.log1p(-jnp.exp2(-5.0 - jnp.arange(RET_HEADS, dtype=F32)))
    idx = jnp.arange(c_pad, dtype=F32)
    real = idx < c_real
    rel = idx[:, None] - idx[None, :]
    intra = jnp.where(rel >= 0, jnp.exp(lg[:, None, None] * jnp.maximum(rel, 0.0)), 0.0)
    intra = jnp.where(real[None, :, None] & real[None, None, :], intra, 0.0)
    q_dec = jnp.where(real[None, :], jnp.exp((idx[None, :] + 1.0) * lg[:, None]), 0.0)
    k_dec = jnp.where(real[None, :], jnp.exp((c_real - 1.0 - idx)[None, :] * lg[:, None]), 0.0)
    c_dec = jnp.exp(c_real * lg)
    bc = lambda t: jnp.broadcast_to(t[:, :, None], (RET_HEADS, c_pad, RET_DK))
    return intra, bc(q_dec), bc(k_dec), c_dec


def _rope_tables(pos):
    half = RET_DK // 2
    inv_freq = 1.0 / (ROPE_BASE ** (jnp.arange(half, dtype=F32) / half))
    ang = pos.astype(F32)[:, None] * inv_freq[None, :]
    cos, sin = jnp.cos(ang), jnp.sin(ang)
    return jnp.concatenate([cos, cos], -1), jnp.concatenate([-sin, sin], -1)


def _ret_kernel(cdec_ref, rq_ref, rk_ref, rv_ref, rg_ref, cos_ref, sin_ref, intra_ref, qdec_ref,
                kdec_ref, s0_ref, o_ref, s_out_ref, s_scr, *, n_chunks):
    c = pl.program_id(1)

    @pl.when(c == 0)
    def _():
        s_scr[...] = s0_ref[...]

    cos2 = cos_ref[...]
    sin2 = sin_ref[...]

    def rot(x):
        return x * cos2 + pltpu.roll(x, RET_DK // 2, 1) * sin2

    for bi in range(rq_ref.shape[0]):
        for h in range(RET_HEADS):
            q = rot(rq_ref[bi, :, h * RET_DK:(h + 1) * RET_DK])
            k = rot(rk_ref[bi, :, h * RET_DK:(h + 1) * RET_DK]) * (RET_DK ** -0.5)
            v = rv_ref[bi, :, h * RET_DV:(h + 1) * RET_DV].astype(BF)
            s_old = s_scr[bi, h]
            a = _bdot_nt(q, k) * intra_ref[h]
            o = _bdot(a, v) + _bdot(q * qdec_ref[h], s_old)
            s_scr[bi, h] = s_old * cdec_ref[h] + _bdot_tn(k * kdec_ref[h], v)
            o = o * lax.rsqrt(jnp.mean(o * o, -1, keepdims=True) + NORM_EPS)
            g = rg_ref[bi, :, h * RET_DV:(h + 1) * RET_DV]
            o_ref[bi, :, h * RET_DV:(h + 1) * RET_DV] = o * (g * jax.nn.sigmoid(g))

    @pl.when(c == n_chunks - 1)
    def _():
        s_out_ref[...] = s_scr[...]


def _retention(h3, pos, state0, c_real, c_pad, tb):
    b, l, _ = h3.shape
    n_chunks = l // c_pad
    intra, qdec, kdec, cdec = _ret_tables(c_real, c_pad)
    cos2, sin2 = _rope_tables(pos)
    full3 = lambda shape: pl.BlockSpec(shape, lambda i, c: (0, 0, 0))
    return pl.pallas_call(
        functools.partial(_ret_kernel, n_chunks=n_chunks),
        grid=(b // tb, n_chunks),
        in_specs=[
            pl.BlockSpec(memory_space=pltpu.SMEM),
            pl.BlockSpec((tb, c_pad, 512), lambda i, c: (i, c, COL_RQ // 512)),
            pl.BlockSpec((tb, c_pad, 512), lambda i, c: (i, c, COL_RK // 512)),
            pl.BlockSpec((tb, c_pad, 1024), lambda i, c: (i, c, COL_RV // 1024)),
            pl.BlockSpec((tb, c_pad, 1024), lambda i, c: (i, c, COL_RG // 1024)),
            pl.BlockSpec((c_pad, RET_DK), lambda i, c: (c, 0)),
            pl.BlockSpec((c_pad, RET_DK), lambda i, c: (c, 0)),
            full3((RET_HEADS, c_pad, c_pad)),
            full3((RET_HEADS, c_pad, RET_DK)),
            full3((RET_HEADS, c_pad, RET_DK)),
            pl.BlockSpec((tb, RET_HEADS, RET_DK, RET_DV), lambda i, c: (i, 0, 0, 0)),
        ],
        out_specs=[
            pl.BlockSpec((tb, c_pad, 1024), lambda i, c: (i, c, 0)),
            pl.BlockSpec((tb, RET_HEADS, RET_DK, RET_DV), lambda i, c: (i, 0, 0, 0)),
        ],
        out_shape=[jax.ShapeDtypeStruct((b, l, 1024), F32),
                   jax.ShapeDtypeStruct((b, RET_HEADS, RET_DK, RET_DV), F32)],
        scratch_shapes=[pltpu.VMEM((tb, RET_HEADS, RET_DK, RET_DV), F32)],
        compiler_params=_params("parallel", "arbitrary"),
        name="retention",
    )(cdec, h3, h3, h3, h3, cos2, sin2, intra, qdec, kdec, state0)


def _sink_softmax(s, sink):
    m = jnp.maximum(jnp.max(s, -1, keepdims=True), sink)
    p = jnp.exp(s - m)
    return p / (jnp.sum(p, -1, keepdims=True) + jnp.exp(sink - m))


def _swa_prompt_kernel(sinks_ref, q_ref, kp_ref, kc_ref, vp_ref, vc_ref, o_ref):
    n = pl.program_id(0)
    rows = SWA_GROUP * SWA_BLOCK
    qi = lax.broadcasted_iota(jnp.int32, (rows, 2 * SWA_BLOCK), 0) % SWA_BLOCK
    kj = lax.broadcasted_iota(jnp.int32, (rows, 2 * SWA_BLOCK), 1)
    rel = SWA_BLOCK + qi - kj
    valid = (rel >= 0) & (rel <= WINDOW) & ((kj >= SWA_BLOCK) | (n > 0))
    row_head = lax.broadcasted_iota(jnp.int32, (rows, 1), 0) // SWA_BLOCK
    for h in range(SWA_KV_HEADS):
        sl = slice(h * SWA_DH, (h + 1) * SWA_DH)
        k2 = jnp.concatenate([kp_ref[:, sl], kc_ref[:, sl]], 0).astype(BF)
        v2 = jnp.concatenate([vp_ref[:, sl], vc_ref[:, sl]], 0).astype(BF)
        heads = [h * SWA_GROUP + g for g in range(SWA_GROUP)]
        q = jnp.concatenate([q_ref[:, hq * SWA_DH:(hq + 1) * SWA_DH] for hq in heads], axis=0)
        sink = jnp.zeros((rows, 1), F32)
        for g, hq in enumerate(heads):
            sink = jnp.where(row_head == g, sinks_ref[hq], sink)
        s = _bdot_nt(q, k2) * (SWA_DH ** -0.5)
        s = jnp.where(valid, s, NEG)
        o = _bdot(_sink_softmax(s, sink), v2)
        for g, hq in enumerate(heads):
            o_ref[:, hq * SWA_DH:(hq + 1) * SWA_DH] = o[g * SWA_BLOCK:(g + 1) * SWA_BLOCK, :]


def _swa_prompt(h2, sinks):
    t = h2.shape[0]
    nb = t // SWA_BLOCK
    prev = lambda col: (lambda n: (jnp.maximum(n - 1, 0), col))
    cur = lambda col: (lambda n: (n, col))
    ck, cv = COL_SK // 256, COL_SV // 256
    return pl.pallas_call(
        _swa_prompt_kernel,
        grid=(nb,),
        in_specs=[
            pl.BlockSpec(memory_space=pltpu.SMEM),
            pl.BlockSpec((SWA_BLOCK, 1024), cur(COL_SQ // 1024)),
            pl.BlockSpec((SWA_BLOCK, 256), prev(ck)),
            pl.BlockSpec((SWA_BLOCK, 256), cur(ck)),
            pl.BlockSpec((SWA_BLOCK, 256), prev(cv)),
            pl.BlockSpec((SWA_BLOCK, 256), cur(cv)),
        ],
        out_specs=pl.BlockSpec((SWA_BLOCK, 1024), lambda n: (n, 0)),
        out_shape=jax.ShapeDtypeStruct((t, 1024), F32),
        compiler_params=_params("parallel"),
        name="swa_prompt",
    )(sinks, h2, h2, h2, h2, h2)


def _swa_sample_kernel(sinks_ref, q_ref, kn_ref, vn_ref, kp_ref, vp_ref, o_ref, *, n_new):
    tb = q_ref.shape[0]
    w = kp_ref.shape[1]
    p_ = SAMPLE_PAD
    rows = SWA_GROUP * p_
    qi = lax.broadcasted_iota(jnp.int32, (tb, rows, w), 1) % p_
    kj = lax.broadcasted_iota(jnp.int32, (tb, rows, w), 2)
    rel_prev = w + qi - kj
    valid_prev = (rel_prev >= 0) & (rel_prev <= WINDOW)
    qn = lax.broadcasted_iota(jnp.int32, (tb, rows, p_), 1) % p_
    kn = lax.broadcasted_iota(jnp.int32, (tb, rows, p_), 2)
    valid_new = (qn - kn >= 0) & (qn - kn <= WINDOW) & (kn < n_new)
    row_head = lax.broadcasted_iota(jnp.int32, (1, rows, 1), 1) // p_
    bdot = lambda eq, a, b: jnp.einsum(eq, a.astype(BF), b.astype(BF), preferred_element_type=F32)
    for h in range(SWA_KV_HEADS):
        sl = slice(h * SWA_DH, (h + 1) * SWA_DH)
        kp, vp = kp_ref[:, :, sl], vp_ref[:, :, sl]
        kn_h, vn_h = kn_ref[:, :, sl], vn_ref[:, :, sl]
        heads = [h * SWA_GROUP + g for g in range(SWA_GROUP)]
        q = jnp.concatenate([q_ref[:, :, hq * SWA_DH:(hq + 1) * SWA_DH] for hq in heads], axis=1)
        sink = jnp.zeros((1, rows, 1), F32)
        for g, hq in enumerate(heads):
            sink = jnp.where(row_head == g, sinks_ref[hq], sink)
        sp = bdot('bqd,bkd->bqk', q, kp) * (SWA_DH ** -0.5)
        sn = bdot('bqd,bkd->bqk', q, kn_h) * (SWA_DH ** -0.5)
        sp = jnp.where(valid_prev, sp, NEG)
        sn = jnp.where(valid_new, sn, NEG)
        m = jnp.maximum(jnp.maximum(jnp.max(sp, -1, keepdims=True),
                                    jnp.max(sn, -1, keepdims=True)), sink)
        pp = jnp.exp(sp - m)
        pn = jnp.exp(sn - m)
        den = jnp.sum(pp, -1, keepdims=True) + jnp.sum(pn, -1, keepdims=True) + jnp.exp(sink - m)
        o = bdot('bqk,bkd->bqd', pp / den, vp) + bdot('bqk,bkd->bqd', pn / den, vn_h)
        for g, hq in enumerate(heads):
            o_ref[:, :, hq * SWA_DH:(hq + 1) * SWA_DH] = o[:, g * p_:(g + 1) * p_, :]


def _swa_sample(h3, prev_k, prev_v, sinks, n_new, tb=8):
    b = h3.shape[0]
    w = prev_k.shape[1]
    return pl.pallas_call(
        functools.partial(_swa_sample_kernel, n_new=n_new),
        grid=(b // tb,),
        in_specs=[
            pl.BlockSpec(memory_space=pltpu.SMEM),
            pl.BlockSpec((tb, SAMPLE_PAD, 1024), lambda i: (i, 0, COL_SQ // 1024)),
            pl.BlockSpec((tb, SAMPLE_PAD, 256), lambda i: (i, 0, COL_SK // 256)),
            pl.BlockSpec((tb, SAMPLE_PAD, 256), lambda i: (i, 0, COL_SV // 256)),
            pl.BlockSpec((tb, w, 256), lambda i: (i, 0, 0)),
            pl.BlockSpec((tb, w, 256), lambda i: (i, 0, 0)),
        ],
        out_specs=pl.BlockSpec((tb, SAMPLE_PAD, 1024), lambda i: (i, 0, 0)),
        out_shape=jax.ShapeDtypeStruct((b, SAMPLE_PAD, 1024), F32),
        compiler_params=_params("parallel"),
        name="swa_sample",
    )(sinks, h3, h3, h3, prev_k, prev_v)


def _mem_kernel(q_ref, mk_ref, mv_ref, o_ref):
    for h in range(MEM_HEADS):
        sl = slice(h * MEM_DH, (h + 1) * MEM_DH)
        s = _bdot_nt(q_ref[0, :, sl], mk_ref[0, :, sl]) * (MEM_DH ** -0.5)
        m = jnp.max(s, -1, keepdims=True)
        e = jnp.exp(s - m)
        p = e / jnp.sum(e, -1, keepdims=True)
        o_ref[0, :, sl] = _bdot(p, mv_ref[0, :, sl])


def _mem_attend(h3, mk, mv, tl):
    b, l, _ = h3.shape
    return pl.pallas_call(
        _mem_kernel,
        grid=(b, l // tl),
        in_specs=[
            pl.BlockSpec((1, tl, 1024), lambda i, j: (i, j, COL_MQ // 1024)),
            pl.BlockSpec((1, N_MEM, 1024), lambda i, j: (i, 0, 0)),
            pl.BlockSpec((1, N_MEM, 1024), lambda i, j: (i, 0, 0)),
        ],
        out_specs=pl.BlockSpec((1, tl, 1024), lambda i, j: (i, j, 0)),
        out_shape=jax.ShapeDtypeStruct((b, l, 1024), F32),
        compiler_params=_params("parallel", "parallel"),
        name="mem_attend",
    )(h3, mk, mv)


def _merge_kernel(ro_ref, so_ref, mo_ref, g0_ref, g1_ref, g2_ref, x_ref, wb_ref, wo_ref, g_ref,
                  b_ref, o_ref, *, alpha):
    acc = None
    for n, (br, gl) in enumerate(((ro_ref, g0_ref), (so_ref, g1_ref), (mo_ref, g2_ref))):
        term = jax.nn.sigmoid(gl[...]) * jnp.dot(br[...].astype(BF), wb_ref[n],
                                                 preferred_element_type=F32)
        acc = term if acc is None else acc + term
    a = jnp.dot(acc.astype(BF), wo_ref[...], preferred_element_type=F32)
    _store_tiled_rows(o_ref, (), _layer_norm(alpha * x_ref[...] + a, g_ref[...], b_ref[...]))


def _merge(ro, so, mo, h2, x2, wb_bf, wo_bf, g, b, alpha, tm):
    t = x2.shape[0]
    tile = lambda col: pl.BlockSpec((tm, 1024), lambda i: (i, col))
    gl0 = COL_GL // 1024
    return pl.pallas_call(
        functools.partial(_merge_kernel, alpha=alpha),
        grid=(t // tm,),
        in_specs=[tile(0), tile(0), tile(0), tile(gl0), tile(gl0 + 1), tile(gl0 + 2), tile(0),
                  pl.BlockSpec((N_BRANCH, 1024, 1024), lambda i: (0, 0, 0)),
                  pl.BlockSpec((1024, 1024), lambda i: (0, 0)),
                  pl.BlockSpec((1, 1024), lambda i: (0, 0)),
                  pl.BlockSpec((1, 1024), lambda i: (0, 0))],
        out_specs=pl.BlockSpec((tm * ROW_SUB, 128), lambda i: (i, 0)),
        out_shape=jax.ShapeDtypeStruct((t * ROW_SUB, 128), F32),
        compiler_params=_params("parallel"),
        name="merge_ln1",
    )(ro, so, mo, h2, h2, h2, x2, wb_bf, wo_bf, g, b)


def _first_index_of_max(v, iota, big, axes):
    m = jnp.max(v, axis=axes, keepdims=True)
    idx = jnp.min(jnp.where(v == m, iota, big), axis=axes, keepdims=True)
    return m, idx


def _router_kernel(x_ref, wr_ref, bias_ref, eidx_ref, ew_ref):
    tt = x_ref.shape[0] // ROW_SUB
    x = _load_tiled_rows(x_ref, (), tt).astype(BF)
    logits = lax.dot_general(wr_ref[...], x, (((1,), (1,)), ((), ())),
                             preferred_element_type=F32)
    s = jax.nn.sigmoid(logits).reshape(N_GROUPS, GROUP_SIZE, tt)
    sb = s + bias_ref[...].reshape(N_GROUPS, GROUP_SIZE, 1)
    ninf = -jnp.inf
    r_iota = lax.broadcasted_iota(jnp.int32, sb.shape, 1)
    m1, i1 = _first_index_of_max(sb, r_iota, GROUP_SIZE, 1)
    m2 = jnp.max(jnp.where(r_iota == i1, ninf, sb), axis=1, keepdims=True)
    gsc = (m1 + m2).reshape(N_GROUPS, tt)
    g_iota = lax.broadcasted_iota(jnp.int32, gsc.shape, 0)
    gmask = jnp.zeros(gsc.shape, jnp.bool_)
    for _ in range(TOPK_GROUPS):
        _, gi = _first_index_of_max(gsc, g_iota, N_GROUPS, 0)
        hit = g_iota == gi
        gmask = gmask | hit
        gsc = jnp.where(hit, ninf, gsc)
    cand = jnp.where(gmask.reshape(N_GROUPS, 1, tt), sb, ninf)
    e_iota = lax.broadcasted_iota(jnp.int32, sb.shape, 0) * GROUP_SIZE + r_iota
    idxs, ws = [], []
    for _ in range(TOP_K):
        _, ei = _first_index_of_max(cand, e_iota, N_EXPERTS, (0, 1))
        hit = e_iota == ei
        idxs.append(ei.reshape(1, tt))
        ws.append(jnp.sum(jnp.where(hit, s, 0.0), axis=(0, 1)).reshape(1, tt))
        cand = jnp.where(hit, ninf, cand)
    w = jnp.concatenate(ws, 0)
    eidx_ref[...] = jnp.concatenate(idxs, 0)
    ew_ref[...] = w / jnp.sum(w, 0, keepdims=True) * ROUTED_SCALE


def _router(xt, wr_t_bf, bias_col, tt):
    t = xt.shape[0] // ROW_SUB
    return pl.pallas_call(
        _router_kernel,
        grid=(t // tt,),
        in_specs=[pl.BlockSpec((tt * ROW_SUB, 128), lambda i: (i, 0)),
                  pl.BlockSpec((N_EXPERTS, 1024), lambda i: (0, 0)),
                  pl.BlockSpec((N_EXPERTS, 1), lambda i: (0, 0))],
        out_specs=[pl.BlockSpec((TOP_K, tt), lambda i: (0, i)),
                   pl.BlockSpec((TOP_K, tt), lambda i: (0, i))],
        out_shape=[jax.ShapeDtypeStruct((TOP_K, t), jnp.int32),
                   jax.ShapeDtypeStruct((TOP_K, t), F32)],
        compiler_params=_params("parallel"),
        name="router",
    )(xt, wr_t_bf, bias_col)


IDX_BITS = 18


def _sorted_keys(eidx):
    flat_e = eidx.reshape(-1)
    assert flat_e.shape[0] < (1 << IDX_BITS)
    return jnp.sort(flat_e * (1 << IDX_BITS) + jnp.arange(flat_e.shape[0], dtype=jnp.int32))


def _dispatch(eidx, ew, skey, t):
    a = t * TOP_K
    nblk = -(-a // MOE_BLOCK) + N_EXPERTS
    assert nblk % STEP_BLOCKS == 0 and STEP_BLOCKS % 2 == 0
    flat_e = eidx.reshape(-1)
    si = skey & ((1 << IDX_BITS) - 1)
    experts = jnp.arange(N_EXPERTS, dtype=jnp.int32)
    counts = jnp.sum((flat_e[None, :] == experts[:, None]).astype(jnp.int32), axis=1)
    grp_start = jnp.cumsum(counts) - counts
    padded = (counts + MOE_BLOCK - 1) // MOE_BLOCK * MOE_BLOCK
    pad_end = jnp.cumsum(padded)
    pad_start = pad_end - padded
    blk_first = jnp.arange(nblk, dtype=jnp.int32) * MOE_BLOCK
    blk_exp = jnp.minimum(jnp.sum((pad_end[None, :] <= blk_first[:, None]).astype(jnp.int32), axis=1),
                          N_EXPERTS - 1)
    off = (jnp.arange(nblk * MOE_BLOCK, dtype=jnp.int32).reshape(nblk, MOE_BLOCK)
           - pad_start[blk_exp][:, None])
    valid = off < counts[blk_exp][:, None]
    src = jnp.clip(grp_start[blk_exp][:, None] + off, 0, a - 1)
    row_si = si[src]
    row_tok = jnp.where(valid, row_si >> 3, t).astype(jnp.int32)
    row_w = jnp.where(valid, ew.reshape(-1)[row_si], 0.0)
    nused = (pad_end[-1] // MOE_BLOCK).astype(jnp.int32).reshape(1)
    gidx, nxt = _group_tables(counts, blk_exp)
    step_rows = STEP_BLOCKS * MOE_BLOCK
    return (row_tok.reshape(-1), row_w.reshape(nblk // STEP_BLOCKS, 1, step_rows), blk_exp, nused,
            gidx, nxt)


def _group_tables(counts, blk_exp):
    nonempty = counts > 0
    gidx = (jnp.cumsum(nonempty.astype(jnp.int32)) - 1)[blk_exp]
    experts = jnp.arange(N_EXPERTS, dtype=jnp.int32)
    cand = jnp.where(nonempty, experts, N_EXPERTS)
    later = lax.cummin(cand, axis=0, reverse=True)
    nxt = jnp.concatenate([later[1:], jnp.full((1,), N_EXPERTS, jnp.int32)])
    nxt = jnp.where(nxt >= N_EXPERTS, -1, nxt)
    return gidx.astype(jnp.int32), nxt[blk_exp].astype(jnp.int32)


def _weight_copies(hbm_refs, buf_ref, sem_ref, e, slot):
    return [pltpu.make_async_copy(h.at[e], buf_ref.at[slot, k], sem_ref.at[slot, k])
            for k, h in enumerate(hbm_refs)]


def _stage_weights(b, be_ref, nused_ref, gidx_ref, nxt_ref, hbm_refs, buf_ref, sem_ref, cache_refs):
    first = ((b == 0) | (be_ref[b] != be_ref[jnp.maximum(b - 1, 0)])) & (b < nused_ref[0])

    @pl.when(first)
    def _():
        slot = gidx_ref[b] % 2

        @pl.when(b == 0)
        def _():
            for c in _weight_copies(hbm_refs, buf_ref, sem_ref, be_ref[0], 0):
                c.start()

        for c in _weight_copies(hbm_refs, buf_ref, sem_ref, be_ref[b], slot):
            c.wait()
        nxt = nxt_ref[b]

        @pl.when(nxt >= 0)
        def _():
            for c in _weight_copies(hbm_refs, buf_ref, sem_ref, nxt, 1 - slot):
                c.start()

        for k, cache in enumerate(cache_refs):
            cache[...] = buf_ref[slot, k].astype(BF)


def _gather_rows(tok_ref, b, x_ref, xg_ref, slot):
    for r in range(MOE_BLOCK):
        xg_ref[slot, r * ROW_SUB:(r + 1) * ROW_SUB, :] = x_ref[tok_ref[b * MOE_BLOCK + r]]


def _up_kernel(be_ref, nused_ref, gidx_ref, nxt_ref, tok_ref, x_ref, wg_hbm, wu_hbm, act_ref,
               xg_ref, wbuf_ref, wgb_ref, wub_ref, sem_ref):
    i = pl.program_id(0)
    nblk = pl.num_programs(0) * STEP_BLOCKS
    b0 = i * STEP_BLOCKS

    @pl.when(i == 0)
    def _():
        _gather_rows(tok_ref, 0, x_ref, xg_ref, 0)

    @pl.when(b0 < nused_ref[0])
    def _():
        for j in range(STEP_BLOCKS):
            b = b0 + j
            _stage_weights(b, be_ref, nused_ref, gidx_ref, nxt_ref, (wg_hbm, wu_hbm), wbuf_ref,
                           sem_ref, (wgb_ref, wub_ref))
            _gather_rows(tok_ref, jnp.minimum(b + 1, nblk - 1), x_ref, xg_ref, (j + 1) % 2)
            x = _load_tiled_rows(xg_ref, (j % 2,), MOE_BLOCK).astype(BF)
            g = jnp.dot(x, wgb_ref[...], preferred_element_type=F32)
            u = jnp.dot(x, wub_ref[...], preferred_element_type=F32)
            act_ref[j * MOE_BLOCK:(j + 1) * MOE_BLOCK, :] = ((g * jax.nn.sigmoid(g)) * u).astype(BF)

    @pl.when(b0 >= nused_ref[0])
    def _():
        act_ref[...] = jnp.zeros_like(act_ref)


def _experts_up(x, row_tok, blk_exp, nused, gidx, nxt, w_gate, w_up):
    nblk = blk_exp.shape[0]
    step_rows = STEP_BLOCKS * MOE_BLOCK
    return pl.pallas_call(
        _up_kernel,
        grid_spec=pltpu.PrefetchScalarGridSpec(
            num_scalar_prefetch=5,
            grid=(nblk // STEP_BLOCKS,),
            in_specs=[pl.BlockSpec(memory_space=pltpu.VMEM),
                      pl.BlockSpec(memory_space=pl.ANY),
                      pl.BlockSpec(memory_space=pl.ANY)],
            out_specs=pl.BlockSpec((step_rows, D_EXPERT), lambda i, *_: (i, 0)),
            scratch_shapes=[pltpu.VMEM((2, MOE_BLOCK * ROW_SUB, 128), F32),
                            pltpu.VMEM((2, 2, D_MODEL, D_EXPERT), F32),
                            pltpu.VMEM((D_MODEL, D_EXPERT), BF),
                            pltpu.VMEM((D_MODEL, D_EXPERT), BF),
                            pltpu.SemaphoreType.DMA((2, 2))],
        ),
        out_shape=jax.ShapeDtypeStruct((nblk * MOE_BLOCK, D_EXPERT), BF),
        compiler_params=_params("arbitrary"),
        name="experts_up",
    )(blk_exp, nused, gidx, nxt, row_tok, x, w_gate, w_up)


SCATTER_GROUP = 8


def _row_to_column(row):
    n = row.shape[1]
    eye = lax.broadcasted_iota(jnp.int32, (n, n), 0) == lax.broadcasted_iota(jnp.int32, (n, n), 1)
    return jnp.sum(jnp.where(eye, jnp.broadcast_to(row, (n, n)), 0.0), axis=1, keepdims=True)


def _scatter_add_rows(tok_ref, b, y_ref, yb_ref, slot):
    for r0 in range(0, MOE_BLOCK, SCATTER_GROUP):
        rs = range(r0, r0 + SCATTER_GROUP)
        toks = [tok_ref[b * MOE_BLOCK + r] for r in rs]
        new = [y_ref[t] + yb_ref[slot, r * ROW_SUB:(r + 1) * ROW_SUB, :] for r, t in zip(rs, toks)]
        for t, v in zip(toks, new):
            y_ref[t] = v


def _down_kernel(be_ref, nused_ref, gidx_ref, nxt_ref, tok_ref, act_ref, rw_ref, wd_hbm, y_ref,
                 yb_ref, wbuf_ref, wdb_ref, sem_ref):
    i = pl.program_id(0)
    nblk = pl.num_programs(0) * STEP_BLOCKS
    b0 = i * STEP_BLOCKS

    @pl.when(i == 0)
    def _():
        y_ref[...] = jnp.zeros_like(y_ref)
        yb_ref[1] = jnp.zeros(yb_ref.shape[1:], F32)

    @pl.when(b0 <= nused_ref[0])
    def _():
        for j in range(STEP_BLOCKS):
            b = b0 + j
            rows = slice(j * MOE_BLOCK, (j + 1) * MOE_BLOCK)
            _stage_weights(b, be_ref, nused_ref, gidx_ref, nxt_ref, (wd_hbm,), wbuf_ref, sem_ref,
                           (wdb_ref,))
            yb = jnp.dot(act_ref[rows, :], wdb_ref[...],
                         preferred_element_type=F32) * _row_to_column(rw_ref[0, :, rows])
            _store_tiled_rows(yb_ref, (j % 2,), yb)
            _scatter_add_rows(tok_ref, jnp.maximum(b - 1, 0), y_ref, yb_ref, (j + 1) % 2)

    @pl.when((i == pl.num_programs(0) - 1) & (nused_ref[0] >= nblk))
    def _():
        _scatter_add_rows(tok_ref, nblk - 1, y_ref, yb_ref, (STEP_BLOCKS - 1) % 2)


def _experts_down(act, row_tok, row_w, blk_exp, nused, gidx, nxt, w_down, t_rows):
    nblk = blk_exp.shape[0]
    step_rows = STEP_BLOCKS * MOE_BLOCK
    return pl.pallas_call(
        _down_kernel,
        grid_spec=pltpu.PrefetchScalarGridSpec(
            num_scalar_prefetch=5,
            grid=(nblk // STEP_BLOCKS,),
            in_specs=[pl.BlockSpec((step_rows, D_EXPERT), lambda i, *_: (i, 0)),
                      pl.BlockSpec((1, 1, step_rows), lambda i, *_: (i, 0, 0)),
                      pl.BlockSpec(memory_space=pl.ANY)],
            out_specs=pl.BlockSpec(memory_space=pltpu.VMEM),
            scratch_shapes=[pltpu.VMEM((2, MOE_BLOCK * ROW_SUB, 128), F32),
                            pltpu.VMEM((2, 1, D_EXPERT, D_MODEL), F32),
                            pltpu.VMEM((D_EXPERT, D_MODEL), BF),
                            pltpu.SemaphoreType.DMA((2, 1))],
        ),
        out_shape=jax.ShapeDtypeStruct((t_rows, ROW_SUB, 128), F32),
        compiler_params=_params("arbitrary"),
        name="experts_down",
    )(blk_exp, nused, gidx, nxt, row_tok, act, row_w, w_down)


def _final_kernel(h_ref, yr_ref, wg_ref, wu_ref, wd_ref, g_ref, b_ref, o_ref, *, alpha):
    h = _load_tiled_rows(h_ref, (), o_ref.shape[0])
    hb = h.astype(BF)
    g = jnp.dot(hb, wg_ref[...], preferred_element_type=F32)
    u = jnp.dot(hb, wu_ref[...], preferred_element_type=F32)
    shared = jnp.dot(((g * jax.nn.sigmoid(g)) * u).astype(BF), wd_ref[...],
                     preferred_element_type=F32)
    f = _load_tiled_rows(yr_ref, (), h.shape[0]) + shared
    o_ref[...] = _layer_norm(alpha * h + f, g_ref[...], b_ref[...])


def _final(ht, yr, wg_bf, wu_bf, wd_bf, g, b, alpha, tm):
    t = ht.shape[0] // ROW_SUB
    tiled = pl.BlockSpec((tm * ROW_SUB, 128), lambda i: (i, 0))
    full = lambda shape: pl.BlockSpec(shape, lambda i: (0, 0))
    return pl.pallas_call(
        functools.partial(_final_kernel, alpha=alpha),
        grid=(t // tm,),
        in_specs=[tiled, tiled, full((1024, 256)), full((1024, 256)), full((256, 1024)),
                  full((1, 1024)), full((1, 1024))],
        out_specs=pl.BlockSpec((tm, 1024), lambda i: (i, 0)),
        out_shape=jax.ShapeDtypeStruct((t, 1024), F32),
        compiler_params=_params("parallel"),
        name="shared_ln2",
    )(ht, yr, wg_bf, wu_bf, wd_bf, g, b)


def _moe_ln2(ht, tables, w_gate, w_up, w_down, wsg_bf, wsu_bf, wsd_bf, g, b, alpha):
    t = ht.shape[0] // ROW_SUB
    row_tok, row_w, blk_exp, nused, gidx, nxt = tables
    act = _experts_up(ht.reshape(t, ROW_SUB, 128), jnp.minimum(row_tok, t - 1), blk_exp, nused, gidx,
                      nxt, w_gate, w_up)
    yr = _experts_down(act, row_tok, row_w, blk_exp, nused, gidx, nxt, w_down, t + 8)
    return _final(ht, yr.reshape((t + 8) * ROW_SUB, 128), wsg_bf, wsu_bf, wsd_bf, g, b, alpha, 256)


def _permute_w_in(w_in):
    rq, rk, rv, rg, sq, sk, sv, mq, gl = jnp.split(
        w_in, [512, 1024, 2048, 3072, 4096, 4352, 4608, 5632], axis=-1)
    return jnp.concatenate([rq, rk, rv, rg, sq, mq, gl, sk, sv], -1)


def kernel(x_prompt, x_sample, mem_prompt, cache_ret_state, cache_swa_k, cache_swa_v, cache_mem_k,
           cache_mem_v, w_in, swa_sinks, w_mem_kv, w_branch, w_o, ln1_g, ln1_b, w_router,
           router_bias, w_gate_e, w_up_e, w_down_e, w_sh_gate, w_sh_up, w_sh_down, ln2_g, ln2_b):
    depth = w_in.shape[0]
    assert depth == 1
    alpha = (2.0 * depth) ** 0.25
    bp, lp, d = x_prompt.shape
    bs, ls, _ = x_sample.shape
    l = 0

    w_in_bf = _permute_w_in(w_in[l]).astype(BF)
    sinks = swa_sinks[l]
    wb_bf = w_branch[l].astype(BF)
    wo_bf = w_o[l].astype(BF)
    g1, b1 = ln1_g[l].reshape(1, d), ln1_b[l].reshape(1, d)
    g2, b2 = ln2_g[l].reshape(1, d), ln2_b[l].reshape(1, d)

    tp = bp * lp
    xp2 = x_prompt.reshape(tp, d)
    hp2 = _proj(xp2, w_in_bf, 1024, PROJ_TN)
    hp3 = hp2.reshape(bp, lp, D_IN)
    mkv = _proj(mem_prompt.reshape(bp * N_MEM, d), w_mem_kv[l].astype(BF), N_MEM, 512)
    mk_p, mv_p = mkv[:, :1024].reshape(bp, N_MEM, 1024), mkv[:, 1024:].reshape(bp, N_MEM, 1024)
    rs0 = jnp.zeros((bp, RET_HEADS, RET_DK, RET_DV), F32)
    ro_p, rs_p = _retention(hp3, jnp.arange(lp), rs0, RET_CHUNK, RET_CHUNK, 1)
    so_p = _swa_prompt(hp2, sinks)
    mo_p = _mem_attend(hp3, mk_p, mv_p, 256)
    hmid_p = _merge(ro_p.reshape(tp, d), so_p, mo_p.reshape(tp, d), hp2, xp2, wb_bf, wo_bf,
                    g1, b1, alpha, 256)

    ts = bs * SAMPLE_PAD
    xs3 = jnp.pad(x_sample, ((0, 0), (0, SAMPLE_PAD - ls), (0, 0)))
    xs2 = xs3.reshape(ts, d)
    hs2 = _proj(xs2, w_in_bf, ts, PROJ_TN)
    hs3 = hs2.reshape(bs, SAMPLE_PAD, D_IN)
    pos_s = PAST_LEN + jnp.arange(SAMPLE_PAD)
    ro_s, rs_s = _retention(hs3, pos_s, cache_ret_state.reshape(bs, RET_HEADS, RET_DK, RET_DV), ls,
                            SAMPLE_PAD, 4)
    w_buf = cache_swa_k.shape[2]
    prev_k = cache_swa_k.reshape(bs, w_buf, SWA_KV_HEADS * SWA_DH)
    prev_v = cache_swa_v.reshape(bs, w_buf, SWA_KV_HEADS * SWA_DH)
    so_s = _swa_sample(hs3, prev_k, prev_v, sinks, ls)
    mo_s = _mem_attend(hs3, cache_mem_k.reshape(bs, N_MEM, 1024).astype(BF),
                       cache_mem_v.reshape(bs, N_MEM, 1024).astype(BF), SAMPLE_PAD)
    hmid_s = _merge(ro_s.reshape(ts, d), so_s.reshape(ts, d), mo_s.reshape(ts, d), hs2, xs2,
                    wb_bf, wo_bf, g1, b1, alpha, 256)
    hmid_s = hmid_s.reshape(bs, SAMPLE_PAD, ROW_SUB, 128)[:, :ls].reshape(bs * ls * ROW_SUB, 128)

    t_all = tp + bs * ls
    assert t_all % (2 * ROUTER_TILE) == 0 and t_all // 2 <= tp
    th = t_all // 2
    groups = (hmid_p[:th * ROW_SUB], jnp.concatenate([hmid_p[th * ROW_SUB:], hmid_s], 0))
    wr_t_bf, bias_col = w_router[l].T.astype(BF), router_bias[l].reshape(N_EXPERTS, 1)
    routed = [_router(ht, wr_t_bf, bias_col, ROUTER_TILE) for ht in groups]
    tables = jax.vmap(functools.partial(_dispatch, t=th))(
        jnp.stack([e.T for e, _ in routed]), jnp.stack([w.T for _, w in routed]),
        jnp.stack([_sorted_keys(e.T) for e, _ in routed]))
    y_a, y_b = (
        _moe_ln2(ht, jax.tree.map(lambda a: a[i], tables), w_gate_e[l], w_up_e[l], w_down_e[l],
                 w_sh_gate[l].astype(BF), w_sh_up[l].astype(BF), w_sh_down[l].astype(BF), g2, b2, alpha)
        for i, ht in enumerate(groups))

    y_p = jnp.concatenate([y_a, y_b[:tp - th]], 0).reshape(bp, lp, d)
    y_s = y_b[tp - th:].reshape(bs, ls, d)
    kv4 = lambda a, n: a.reshape(1, a.shape[0], n, SWA_KV_HEADS, SWA_DH)
    k_p = kv4(hp3[:, lp - WINDOW:, COL_SK:COL_SK + 256], WINDOW)
    v_p = kv4(hp3[:, lp - WINDOW:, COL_SV:COL_SV + 256], WINDOW)
    mem4 = lambda a: a.reshape(1, bp, N_MEM, MEM_HEADS, MEM_DH)
    k_s = kv4(jnp.concatenate([prev_k, hs3[:, :ls, COL_SK:COL_SK + 256]], 1)[:, -w_buf:], w_buf)
    v_s = kv4(jnp.concatenate([prev_v, hs3[:, :ls, COL_SV:COL_SV + 256]], 1)[:, -w_buf:], w_buf)
    return (y_p, y_s, rs_p[None], k_p, v_p, mem4(mk_p), mem4(mv_p), rs_s[None], k_s, v_s)
```

```python
import functools

import jax
import jax.numpy as jnp
from jax import lax
from jax.experimental import pallas as pl
from jax.experimental.pallas import tpu as pltpu

BF = jnp.bfloat16
F32 = jnp.float32

D_MODEL = 1024
RET_HEADS = 4
RET_DK = 128
RET_DV = 256
RET_CHUNK = 128
ROPE_BASE = 10000.0
SWA_HEADS = 16
SWA_KV_HEADS = 4
SWA_GROUP = SWA_HEADS // SWA_KV_HEADS
SWA_DH = 64
WINDOW = 128
SWA_BLOCK = 128
N_MEM = 256
MEM_HEADS = 4
MEM_DH = 256
N_BRANCH = 3
N_EXPERTS = 256
TOP_K = 8
N_GROUPS = 8
GROUP_SIZE = N_EXPERTS // N_GROUPS
TOPK_GROUPS = 4
D_EXPERT = 256
ROUTED_SCALE = 2.5
MOE_BLOCK = 128
ROUTER_TILE = 384
ROW_SUB = D_MODEL // 128
STEP_BLOCKS = 4
LN_EPS = 1e-5
NORM_EPS = 1e-6
NEG = -1e30
PAST_LEN = 16384
SAMPLE_PAD = 8

COL_RQ, COL_RK, COL_RV, COL_RG, COL_SQ, COL_MQ, COL_GL, COL_SK, COL_SV = (
    0, 512, 1024, 2048, 3072, 4096, 5120, 8192, 8448)
D_IN = 8704
PROJ_TN = D_IN // 4

VMEM_LIMIT = 56 * 1024 * 1024


def _params(*sem):
    return pltpu.CompilerParams(dimension_semantics=sem, vmem_limit_bytes=VMEM_LIMIT)


def _bdot(a, b):
    return jnp.dot(a.astype(BF), b.astype(BF), preferred_element_type=F32)


def _bdot_nt(a, b):
    return lax.dot_general(a.astype(BF), b.astype(BF), (((1,), (1,)), ((), ())),
                           preferred_element_type=F32)


def _bdot_tn(a, b):
    return lax.dot_general(a.astype(BF), b.astype(BF), (((0,), (0,)), ((), ())),
                           preferred_element_type=F32)


def _layer_norm(z, g, b):
    zc = z - jnp.mean(z, -1, keepdims=True)
    var = jnp.mean(zc * zc, -1, keepdims=True)
    return zc * lax.rsqrt(var + LN_EPS) * g + b


def _load_tiled_rows(ref, idx, m):
    return jnp.concatenate([ref[(*idx, pl.ds(s, m, stride=ROW_SUB), slice(None))]
                            for s in range(ROW_SUB)], axis=1)


def _store_tiled_rows(ref, idx, val):
    m = val.shape[0]
    for s in range(ROW_SUB):
        ref[(*idx, pl.ds(s, m, stride=ROW_SUB), slice(None))] = val[:, s * 128:(s + 1) * 128]


def _proj_kernel(x_ref, w_ref, o_ref, xb_ref):
    @pl.when(pl.program_id(1) == 0)
    def _():
        xb_ref[...] = x_ref[...].astype(BF)

    o_ref[...] = jnp.dot(xb_ref[...], w_ref[...], preferred_element_type=F32)


def _proj(x, w_bf, tm, tn):
    m, k = x.shape
    n = w_bf.shape[1]
    return pl.pallas_call(
        _proj_kernel,
        grid=(m // tm, n // tn),
        in_specs=[pl.BlockSpec((tm, k), lambda i, j: (i, 0)),
                  pl.BlockSpec((k, tn), lambda i, j: (0, j))],
        out_specs=pl.BlockSpec((tm, tn), lambda i, j: (i, j)),
        out_shape=jax.ShapeDtypeStruct((m, n), F32),
        scratch_shapes=[pltpu.VMEM((tm, k), BF)],
        compiler_params=_params("parallel", "arbitrary"),
        name="proj",
    )(x, w_bf)


def _ret_tables(c_real, c_pad):
    lg = jnp.log1p(-jnp.exp2(-5.0 - jnp.arange(RET_HEADS, dtype=F32)))
    idx = jnp.arange(c_pad, dtype=F32)
    real = idx < c_real
    rel = idx[:, None] - idx[None, :]
    intra = jnp.where(rel >= 0, jnp.exp(lg[:, None, None] * jnp.maximum(rel, 0.0)), 0.0)
    intra = jnp.where(real[None, :, None] & real[None, None, :], intra, 0.0)
    q_dec = jnp.where(real[None, :], jnp.exp((idx[None, :] + 1.0) * lg[:, None]), 0.0)
    k_dec = jnp.where(real[None, :], jnp.exp((c_real - 1.0 - idx)[None, :] * lg[:, None]), 0.0)
    c_dec = jnp.exp(c_real * lg)
    bc = lambda t: jnp.broadcast_to(t[:, :, None], (RET_HEADS, c_pad, RET_DK))
    return intra, bc(q_dec), bc(k_dec), c_dec


def _rope_tables(pos):
    half = RET_DK // 2
    inv_freq = 1.0 / (ROPE_BASE ** (jnp.arange(half, dtype=F32) / half))
    ang = pos.astype(F32)[:, None] * inv_freq[None, :]
    cos, sin = jnp.cos(ang), jnp.sin(ang)
    return jnp.concatenate([cos, cos], -1), jnp.concatenate([-sin, sin], -1)


def _ret_kernel(cdec_ref, rq_ref, rk_ref, rv_ref, rg_ref, cos_ref, sin_ref, intra_ref, qdec_ref,
                kdec_ref, s0_ref, o_ref, s_out_ref, s_scr, *, n_chunks):
    c = pl.program_id(1)

    @pl.when(c == 0)
    def _():
        s_scr[...] = s0_ref[...]

    cos2 = cos_ref[...]
    sin2 = sin_ref[...]

    def rot(x):
        return x * cos2 + pltpu.roll(x, RET_DK // 2, 1) * sin2

    for bi in range(rq_ref.shape[0]):
        for h in range(RET_HEADS):
            q = rot(rq_ref[bi, :, h * RET_DK:(h + 1) * RET_DK])
            k = rot(rk_ref[bi, :, h * RET_DK:(h + 1) * RET_DK]) * (RET_DK ** -0.5)
            v = rv_ref[bi, :, h * RET_DV:(h + 1) * RET_DV].astype(BF)
            s_old = s_scr[bi, h]
            a = _bdot_nt(q, k) * intra_ref[h]
            o = _bdot(a, v) + _bdot(q * qdec_ref[h], s_old)
            s_scr[bi, h] = s_old * cdec_ref[h] + _bdot_tn(k * kdec_ref[h], v)
            o = o * lax.rsqrt(jnp.mean(o * o, -1, keepdims=True) + NORM_EPS)
            g = rg_ref[bi, :, h * RET_DV:(h + 1) * RET_DV]
            o_ref[bi, :, h * RET_DV:(h + 1) * RET_DV] = o * (g * jax.nn.sigmoid(g))

    @pl.when(c == n_chunks - 1)
    def _():
        s_out_ref[...] = s_scr[...]


def _retention(h3, pos, state0, c_real, c_pad, tb):
    b, l, _ = h3.shape
    n_chunks = l // c_pad
    intra, qdec, kdec, cdec = _ret_tables(c_real, c_pad)
    cos2, sin2 = _rope_tables(pos)
    full3 = lambda shape: pl.BlockSpec(shape, lambda i, c: (0, 0, 0))
    return pl.pallas_call(
        functools.partial(_ret_kernel, n_chunks=n_chunks),
        grid=(b // tb, n_chunks),
        in_specs=[
            pl.BlockSpec(memory_space=pltpu.SMEM),
            pl.BlockSpec((tb, c_pad, 512), lambda i, c: (i, c, COL_RQ // 512)),
            pl.BlockSpec((tb, c_pad, 512), lambda i, c: (i, c, COL_RK // 512)),
            pl.BlockSpec((tb, c_pad, 1024), lambda i, c: (i, c, COL_RV // 1024)),
            pl.BlockSpec((tb, c_pad, 1024), lambda i, c: (i, c, COL_RG // 1024)),
            pl.BlockSpec((c_pad, RET_DK), lambda i, c: (c, 0)),
            pl.BlockSpec((c_pad, RET_DK), lambda i, c: (c, 0)),
            full3((RET_HEADS, c_pad, c_pad)),
            full3((RET_HEADS, c_pad, RET_DK)),
            full3((RET_HEADS, c_pad, RET_DK)),
            pl.BlockSpec((tb, RET_HEADS, RET_DK, RET_DV), lambda i, c: (i, 0, 0, 0)),
        ],
        out_specs=[
            pl.BlockSpec((tb, c_pad, 1024), lambda i, c: (i, c, 0)),
            pl.BlockSpec((tb, RET_HEADS, RET_DK, RET_DV), lambda i, c: (i, 0, 0, 0)),
        ],
        out_shape=[jax.ShapeDtypeStruct((b, l, 1024), F32),
                   jax.ShapeDtypeStruct((b, RET_HEADS, RET_DK, RET_DV), F32)],
        scratch_shapes=[pltpu.VMEM((tb, RET_HEADS, RET_DK, RET_DV), F32)],
        compiler_params=_params("parallel", "arbitrary"),
        name="retention",
    )(cdec, h3, h3, h3, h3, cos2, sin2, intra, qdec, kdec, state0)


def _sink_softmax(s, sink):
    m = jnp.maximum(jnp.max(s, -1, keepdims=True), sink)
    p = jnp.exp(s - m)
    return p / (jnp.sum(p, -1, keepdims=True) + jnp.exp(sink - m))


def _swa_prompt_kernel(sinks_ref, q_ref, kp_ref, kc_ref, vp_ref, vc_ref, o_ref):
    n = pl.program_id(0)
    rows = SWA_GROUP * SWA_BLOCK
    qi = lax.broadcasted_iota(jnp.int32, (rows, 2 * SWA_BLOCK), 0) % SWA_BLOCK
    kj = lax.broadcasted_iota(jnp.int32, (rows, 2 * SWA_BLOCK), 1)
    rel = SWA_BLOCK + qi - kj
    valid = (rel >= 0) & (rel <= WINDOW) & ((kj >= SWA_BLOCK) | (n > 0))
    row_head = lax.broadcasted_iota(jnp.int32, (rows, 1), 0) // SWA_BLOCK
    for h in range(SWA_KV_HEADS):
        sl = slice(h * SWA_DH, (h + 1) * SWA_DH)
        k2 = jnp.concatenate([kp_ref[:, sl], kc_ref[:, sl]], 0).astype(BF)
        v2 = jnp.concatenate([vp_ref[:, sl], vc_ref[:, sl]], 0).astype(BF)
        heads = [h * SWA_GROUP + g for g in range(SWA_GROUP)]
        q = jnp.concatenate([q_ref[:, hq * SWA_DH:(hq + 1) * SWA_DH] for hq in heads], axis=0)
        sink = jnp.zeros((rows, 1), F32)
        for g, hq in enumerate(heads):
            sink = jnp.where(row_head == g, sinks_ref[hq], sink)
        s = _bdot_nt(q, k2) * (SWA_DH ** -0.5)
        s = jnp.where(valid, s, NEG)
        o = _bdot(_sink_softmax(s, sink), v2)
        for g, hq in enumerate(heads):
            o_ref[:, hq * SWA_DH:(hq + 1) * SWA_DH] = o[g * SWA_BLOCK:(g + 1) * SWA_BLOCK, :]


def _swa_prompt(h2, sinks):
    t = h2.shape[0]
    nb = t // SWA_BLOCK
    prev = lambda col: (lambda n: (jnp.maximum(n - 1, 0), col))
    cur = lambda col: (lambda n: (n, col))
    ck, cv = COL_SK // 256, COL_SV // 256
    return pl.pallas_call(
        _swa_prompt_kernel,
        grid=(nb,),
        in_specs=[
            pl.BlockSpec(memory_space=pltpu.SMEM),
            pl.BlockSpec((SWA_BLOCK, 1024), cur(COL_SQ // 1024)),
            pl.BlockSpec((SWA_BLOCK, 256), prev(ck)),
            pl.BlockSpec((SWA_BLOCK, 256), cur(ck)),
            pl.BlockSpec((SWA_BLOCK, 256), prev(cv)),
            pl.BlockSpec((SWA_BLOCK, 256), cur(cv)),
        ],
        out_specs=pl.BlockSpec((SWA_BLOCK, 1024), lambda n: (n, 0)),
        out_shape=jax.ShapeDtypeStruct((t, 1024), F32),
        compiler_params=_params("parallel"),
        name="swa_prompt",
    )(sinks, h2, h2, h2, h2, h2)


def _swa_sample_kernel(sinks_ref, q_ref, kn_ref, vn_ref, kp_ref, vp_ref, o_ref, *, n_new):
    tb = q_ref.shape[0]
    w = kp_ref.shape[1]
    p_ = SAMPLE_PAD
    rows = SWA_GROUP * p_
    qi = lax.broadcasted_iota(jnp.int32, (tb, rows, w), 1) % p_
    kj = lax.broadcasted_iota(jnp.int32, (tb, rows, w), 2)
    rel_prev = w + qi - kj
    valid_prev = (rel_prev >= 0) & (rel_prev <= WINDOW)
    qn = lax.broadcasted_iota(jnp.int32, (tb, rows, p_), 1) % p_
    kn = lax.broadcasted_iota(jnp.int32, (tb, rows, p_), 2)
    valid_new = (qn - kn >= 0) & (qn - kn <= WINDOW) & (kn < n_new)
    row_head = lax.broadcasted_iota(jnp.int32, (1, rows, 1), 1) // p_
    bdot = lambda eq, a, b: jnp.einsum(eq, a.astype(BF), b.astype(BF), preferred_element_type=F32)
    for h in range(SWA_KV_HEADS):
        sl = slice(h * SWA_DH, (h + 1) * SWA_DH)
        kp, vp = kp_ref[:, :, sl], vp_ref[:, :, sl]
        kn_h, vn_h = kn_ref[:, :, sl], vn_ref[:, :, sl]
        heads = [h * SWA_GROUP + g for g in range(SWA_GROUP)]
        q = jnp.concatenate([q_ref[:, :, hq * SWA_DH:(hq + 1) * SWA_DH] for hq in heads], axis=1)
        sink = jnp.zeros((1, rows, 1), F32)
        for g, hq in enumerate(heads):
            sink = jnp.where(row_head == g, sinks_ref[hq], sink)
        sp = bdot('bqd,bkd->bqk', q, kp) * (SWA_DH ** -0.5)
        sn = bdot('bqd,bkd->bqk', q, kn_h) * (SWA_DH ** -0.5)
        sp = jnp.where(valid_prev, sp, NEG)
        sn = jnp.where(valid_new, sn, NEG)
        m = jnp.maximum(jnp.maximum(jnp.max(sp, -1, keepdims=True),
                                    jnp.max(sn, -1, keepdims=True)), sink)
        pp = jnp.exp(sp - m)
        pn = jnp.exp(sn - m)
        den = jnp.sum(pp, -1, keepdims=True) + jnp.sum(pn, -1, keepdims=True) + jnp.exp(sink - m)
        o = bdot('bqk,bkd->bqd', pp / den, vp) + bdot('bqk,bkd->bqd', pn / den, vn_h)
        for g, hq in enumerate(heads):
            o_ref[:, :, hq * SWA_DH:(hq + 1) * SWA_DH] = o[:, g * p_:(g + 1) * p_, :]


def _swa_sample(h3, prev_k, prev_v, sinks, n_new, tb=8):
    b = h3.shape[0]
    w = prev_k.shape[1]
    return pl.pallas_call(
        functools.partial(_swa_sample_kernel, n_new=n_new),
        grid=(b // tb,),
        in_specs=[
            pl.BlockSpec(memory_space=pltpu.SMEM),
            pl.BlockSpec((tb, SAMPLE_PAD, 1024), lambda i: (i, 0, COL_SQ // 1024)),
            pl.BlockSpec((tb, SAMPLE_PAD, 256), lambda i: (i, 0, COL_SK // 256)),
            pl.BlockSpec((tb, SAMPLE_PAD, 256), lambda i: (i, 0, COL_SV // 256)),
            pl.BlockSpec((tb, w, 256), lambda i: (i, 0, 0)),
            pl.BlockSpec((tb, w, 256), lambda i: (i, 0, 0)),
        ],
        out_specs=pl.BlockSpec((tb, SAMPLE_PAD, 1024), lambda i: (i, 0, 0)),
        out_shape=jax.ShapeDtypeStruct((b, SAMPLE_PAD, 1024), F32),
        compiler_params=_params("parallel"),
        name="swa_sample",
    )(sinks, h3, h3, h3, prev_k, prev_v)


def _mem_kernel(q_ref, mk_ref, mv_ref, o_ref):
    for h in range(MEM_HEADS):
        sl = slice(h * MEM_DH, (h + 1) * MEM_DH)
        s = _bdot_nt(q_ref[0, :, sl], mk_ref[0, :, sl]) * (MEM_DH ** -0.5)
        m = jnp.max(s, -1, keepdims=True)
        e = jnp.exp(s - m)
        p = e / jnp.sum(e, -1, keepdims=True)
        o_ref[0, :, sl] = _bdot(p, mv_ref[0, :, sl])


def _mem_attend(h3, mk, mv, tl):
    b, l, _ = h3.shape
    return pl.pallas_call(
        _mem_kernel,
        grid=(b, l // tl),
        in_specs=[
            pl.BlockSpec((1, tl, 1024), lambda i, j: (i, j, COL_MQ // 1024)),
            pl.BlockSpec((1, N_MEM, 1024), lambda i, j: (i, 0, 0)),
            pl.BlockSpec((1, N_MEM, 1024), lambda i, j: (i, 0, 0)),
        ],
        out_specs=pl.BlockSpec((1, tl, 1024), lambda i, j: (i, j, 0)),
        out_shape=jax.ShapeDtypeStruct((b, l, 1024), F32),
        compiler_params=_params("parallel", "parallel"),
        name="mem_attend",
    )(h3, mk, mv)


def _merge_kernel(ro_ref, so_ref, mo_ref, g0_ref, g1_ref, g2_ref, x_ref, wb_ref, wo_ref, g_ref,
                  b_ref, o_ref, *, alpha):
    acc = None
    for n, (br, gl) in enumerate(((ro_ref, g0_ref), (so_ref, g1_ref), (mo_ref, g2_ref))):
        term = jax.nn.sigmoid(gl[...]) * jnp.dot(br[...].astype(BF), wb_ref[n],
                                                 preferred_element_type=F32)
        acc = term if acc is None else acc + term
    a = jnp.dot(acc.astype(BF), wo_ref[...], preferred_element_type=F32)
    _store_tiled_rows(o_ref, (), _layer_norm(alpha * x_ref[...] + a, g_ref[...], b_ref[...]))


def _merge(ro, so, mo, h2, x2, wb_bf, wo_bf, g, b, alpha, tm):
    t = x2.shape[0]
    tile = lambda col: pl.BlockSpec((tm, 1024), lambda i: (i, col))
    gl0 = COL_GL // 1024
    return pl.pallas_call(
        functools.partial(_merge_kernel, alpha=alpha),
        grid=(t // tm,),
        in_specs=[tile(0), tile(0), tile(0), tile(gl0), tile(gl0 + 1), tile(gl0 + 2), tile(0),
                  pl.BlockSpec((N_BRANCH, 1024, 1024), lambda i: (0, 0, 0)),
                  pl.BlockSpec((1024, 1024), lambda i: (0, 0)),
                  pl.BlockSpec((1, 1024), lambda i: (0, 0)),
                  pl.BlockSpec((1, 1024), lambda i: (0, 0))],
        out_specs=pl.BlockSpec((tm * ROW_SUB, 128), lambda i: (i, 0)),
        out_shape=jax.ShapeDtypeStruct((t * ROW_SUB, 128), F32),
        compiler_params=_params("parallel"),
        name="merge_ln1",
    )(ro, so, mo, h2, h2, h2, x2, wb_bf, wo_bf, g, b)


def _first_index_of_max(v, iota, big, axes):
    m = jnp.max(v, axis=axes, keepdims=True)
    idx = jnp.min(jnp.where(v == m, iota, big), axis=axes, keepdims=True)
    return m, idx


def _router_kernel(x_ref, wr_ref, bias_ref, eidx_ref, ew_ref):
    tt = x_ref.shape[0] // ROW_SUB
    x = _load_tiled_rows(x_ref, (), tt).astype(BF)
    logits = lax.dot_general(wr_ref[...], x, (((1,), (1,)), ((), ())),
                             preferred_element_type=F32)
    s = jax.nn.sigmoid(logits).reshape(N_GROUPS, GROUP_SIZE, tt)
    sb = s + bias_ref[...].reshape(N_GROUPS, GROUP_SIZE, 1)
    ninf = -jnp.inf
    r_iota = lax.broadcasted_iota(jnp.int32, sb.shape, 1)
    m1, i1 = _first_index_of_max(sb, r_iota, GROUP_SIZE, 1)
    m2 = jnp.max(jnp.where(r_iota == i1, ninf, sb), axis=1, keepdims=True)
    gsc = (m1 + m2).reshape(N_GROUPS, tt)
    g_iota = lax.broadcasted_iota(jnp.int32, gsc.shape, 0)
    gmask = jnp.zeros(gsc.shape, jnp.bool_)
    for _ in range(TOPK_GROUPS):
        _, gi = _first_index_of_max(gsc, g_iota, N_GROUPS, 0)
        hit = g_iota == gi
        gmask = gmask | hit
        gsc = jnp.where(hit, ninf, gsc)
    cand = jnp.where(gmask.reshape(N_GROUPS, 1, tt), sb, ninf)
    e_iota = lax.broadcasted_iota(jnp.int32, sb.shape, 0) * GROUP_SIZE + r_iota
    idxs, ws = [], []
    for _ in range(TOP_K):
        _, ei = _first_index_of_max(cand, e_iota, N_EXPERTS, (0, 1))
        hit = e_iota == ei
        idxs.append(ei.reshape(1, tt))
        ws.append(jnp.sum(jnp.where(hit, s, 0.0), axis=(0, 1)).reshape(1, tt))
        cand = jnp.where(hit, ninf, cand)
    w = jnp.concatenate(ws, 0)
    eidx_ref[...] = jnp.concatenate(idxs, 0)
    ew_ref[...] = w / jnp.sum(w, 0, keepdims=True) * ROUTED_SCALE


def _router(xt, wr_t_bf, bias_col, tt):
    t = xt.shape[0] // ROW_SUB
    return pl.pallas_call(
        _router_kernel,
        grid=(t // tt,),
        in_specs=[pl.BlockSpec((tt * ROW_SUB, 128), lambda i: (i, 0)),
                  pl.BlockSpec((N_EXPERTS, 1024), lambda i: (0, 0)),
                  pl.BlockSpec((N_EXPERTS, 1), lambda i: (0, 0))],
        out_specs=[pl.BlockSpec((TOP_K, tt), lambda i: (0, i)),
                   pl.BlockSpec((TOP_K, tt), lambda i: (0, i))],
        out_shape=[jax.ShapeDtypeStruct((TOP_K, t), jnp.int32),
                   jax.ShapeDtypeStruct((TOP_K, t), F32)],
        compiler_params=_params("parallel"),
        name="router",
    )(xt, wr_t_bf, bias_col)


IDX_BITS = 18


def _sorted_keys(eidx):
    flat_e = eidx.reshape(-1)
    assert flat_e.shape[0] < (1 << IDX_BITS)
    return jnp.sort(flat_e * (1 << IDX_BITS) + jnp.arange(flat_e.shape[0], dtype=jnp.int32))


def _dispatch(eidx, ew, skey, t):
    a = t * TOP_K
    nblk = -(-a // MOE_BLOCK) + N_EXPERTS
    assert nblk % STEP_BLOCKS == 0 and STEP_BLOCKS % 2 == 0
    flat_e = eidx.reshape(-1)
    si = skey & ((1 << IDX_BITS) - 1)
    experts = jnp.arange(N_EXPERTS, dtype=jnp.int32)
    counts = jnp.sum((flat_e[None, :] == experts[:, None]).astype(jnp.int32), axis=1)
    grp_start = jnp.cumsum(counts) - counts
    padded = (counts + MOE_BLOCK - 1) // MOE_BLOCK * MOE_BLOCK
    pad_end = jnp.cumsum(padded)
    pad_start = pad_end - padded
    blk_first = jnp.arange(nblk, dtype=jnp.int32) * MOE_BLOCK
    blk_exp = jnp.minimum(jnp.sum((pad_end[None, :] <= blk_first[:, None]).astype(jnp.int32), axis=1),
                          N_EXPERTS - 1)
    off = (jnp.arange(nblk * MOE_BLOCK, dtype=jnp.int32).reshape(nblk, MOE_BLOCK)
           - pad_start[blk_exp][:, None])
    valid = off < counts[blk_exp][:, None]
    src = jnp.clip(grp_start[blk_exp][:, None] + off, 0, a - 1)
    row_si = si[src]
    row_tok = jnp.where(valid, row_si >> 3, t).astype(jnp.int32)
    row_w = jnp.where(valid, ew.reshape(-1)[row_si], 0.0)
    nused = (pad_end[-1] // MOE_BLOCK).astype(jnp.int32).reshape(1)
    gidx, nxt = _group_tables(counts, blk_exp)
    step_rows = STEP_BLOCKS * MOE_BLOCK
    return (row_tok.reshape(-1), row_w.reshape(nblk // STEP_BLOCKS, 1, step_rows), blk_exp, nused,
            gidx, nxt)


def _group_tables(counts, blk_exp):
    nonempty = counts > 0
    gidx = (jnp.cumsum(nonempty.astype(jnp.int32)) - 1)[blk_exp]
    experts = jnp.arange(N_EXPERTS, dtype=jnp.int32)
    cand = jnp.where(nonempty, experts, N_EXPERTS)
    later = lax.cummin(cand, axis=0, reverse=True)
    nxt = jnp.concatenate([later[1:], jnp.full((2,), N_EXPERTS, jnp.int32)])
    nxt1 = nxt[blk_exp]
    nxt2 = nxt[jnp.minimum(nxt1, N_EXPERTS)]
    none = lambda a: jnp.where(a >= N_EXPERTS, -1, a).astype(jnp.int32)
    return gidx.astype(jnp.int32), jnp.concatenate([none(nxt1), none(nxt2)])


WEIGHT_SLOTS = 3


def _weight_copies(hbm_refs, buf_ref, sem_ref, e, slot):
    return [pltpu.make_async_copy(h.at[e], buf_ref.at[slot, k], sem_ref.at[slot, k])
            for k, h in enumerate(hbm_refs)]


def _stage_weights(b, be_ref, nused_ref, gidx_ref, nxt_ref, hbm_refs, buf_ref, sem_ref, cache_refs):
    first = ((b == 0) | (be_ref[b] != be_ref[jnp.maximum(b - 1, 0)])) & (b < nused_ref[0])
    nblk = be_ref.shape[0]

    def start(e, slot):
        for c in _weight_copies(hbm_refs, buf_ref, sem_ref, e, slot):
            c.start()

    @pl.when(first)
    def _():
        slot = gidx_ref[b] % WEIGHT_SLOTS

        @pl.when(b == 0)
        def _():
            start(be_ref[0], 0)

            @pl.when(nxt_ref[0] >= 0)
            def _():
                start(nxt_ref[0], 1)

        for c in _weight_copies(hbm_refs, buf_ref, sem_ref, be_ref[b], slot):
            c.wait()
        nxt2 = nxt_ref[nblk + b]

        @pl.when(nxt2 >= 0)
        def _():
            start(nxt2, (slot + 2) % WEIGHT_SLOTS)

        for k, cache in enumerate(cache_refs):
            cache[...] = buf_ref[slot, k].astype(BF)


def _gather_rows(tok_ref, b, x_ref, xg_ref, slot):
    for r in range(MOE_BLOCK):
        xg_ref[slot, r * ROW_SUB:(r + 1) * ROW_SUB, :] = x_ref[tok_ref[b * MOE_BLOCK + r]]


def _up_kernel(be_ref, nused_ref, gidx_ref, nxt_ref, tok_ref, x_ref, wg_hbm, wu_hbm, act_ref,
               xg_ref, wbuf_ref, wgb_ref, wub_ref, sem_ref):
    i = pl.program_id(0)
    nblk = pl.num_programs(0) * STEP_BLOCKS
    b0 = i * STEP_BLOCKS

    @pl.when(i == 0)
    def _():
        _gather_rows(tok_ref, 0, x_ref, xg_ref, 0)

    @pl.when(b0 < nused_ref[0])
    def _():
        for j in range(STEP_BLOCKS):
            b = b0 + j
            _stage_weights(b, be_ref, nused_ref, gidx_ref, nxt_ref, (wg_hbm, wu_hbm), wbuf_ref,
                           sem_ref, (wgb_ref, wub_ref))
            _gather_rows(tok_ref, jnp.minimum(b + 1, nblk - 1), x_ref, xg_ref, (j + 1) % 2)
            x = _load_tiled_rows(xg_ref, (j % 2,), MOE_BLOCK).astype(BF)
            g = jnp.dot(x, wgb_ref[...], preferred_element_type=F32)
            u = jnp.dot(x, wub_ref[...], preferred_element_type=F32)
            act_ref[j * MOE_BLOCK:(j + 1) * MOE_BLOCK, :] = ((g * jax.nn.sigmoid(g)) * u).astype(BF)

    @pl.when(b0 >= nused_ref[0])
    def _():
        act_ref[...] = jnp.zeros_like(act_ref)


def _experts_up(x, row_tok, blk_exp, nused, gidx, nxt, w_gate, w_up):
    nblk = blk_exp.shape[0]
    step_rows = STEP_BLOCKS * MOE_BLOCK
    return pl.pallas_call(
        _up_kernel,
        grid_spec=pltpu.PrefetchScalarGridSpec(
            num_scalar_prefetch=5,
            grid=(nblk // STEP_BLOCKS,),
            in_specs=[pl.BlockSpec(memory_space=pltpu.VMEM),
                      pl.BlockSpec(memory_space=pl.ANY),
                      pl.BlockSpec(memory_space=pl.ANY)],
            out_specs=pl.BlockSpec((step_rows, D_EXPERT), lambda i, *_: (i, 0)),
            scratch_shapes=[pltpu.VMEM((2, MOE_BLOCK * ROW_SUB, 128), F32),
                            pltpu.VMEM((WEIGHT_SLOTS, 2, D_MODEL, D_EXPERT), F32),
                            pltpu.VMEM((D_MODEL, D_EXPERT), BF),
                            pltpu.VMEM((D_MODEL, D_EXPERT), BF),
                            pltpu.SemaphoreType.DMA((WEIGHT_SLOTS, 2))],
        ),
        out_shape=jax.ShapeDtypeStruct((nblk * MOE_BLOCK, D_EXPERT), BF),
        compiler_params=_params("arbitrary"),
        name="experts_up",
    )(blk_exp, nused, gidx, nxt, row_tok, x, w_gate, w_up)


SCATTER_GROUP = 8


def _row_to_column(row):
    n = row.shape[1]
    eye = lax.broadcasted_iota(jnp.int32, (n, n), 0) == lax.broadcasted_iota(jnp.int32, (n, n), 1)
    return jnp.sum(jnp.where(eye, jnp.broadcast_to(row, (n, n)), 0.0), axis=1, keepdims=True)


def _scatter_add_rows(tok_ref, b, y_ref, yb_ref, slot):
    for r0 in range(0, MOE_BLOCK, SCATTER_GROUP):
        rs = range(r0, r0 + SCATTER_GROUP)
        toks = [tok_ref[b * MOE_BLOCK + r] for r in rs]
        new = [y_ref[t] + yb_ref[slot, r * ROW_SUB:(r + 1) * ROW_SUB, :] for r, t in zip(rs, toks)]
        for t, v in zip(toks, new):
            y_ref[t] = v


def _down_kernel(be_ref, nused_ref, gidx_ref, nxt_ref, tok_ref, act_ref, rw_ref, wd_hbm, y_ref,
                 yb_ref, wbuf_ref, wdb_ref, sem_ref):
    i = pl.program_id(0)
    nblk = pl.num_programs(0) * STEP_BLOCKS
    b0 = i * STEP_BLOCKS

    @pl.when(i == 0)
    def _():
        y_ref[...] = jnp.zeros_like(y_ref)
        yb_ref[1] = jnp.zeros(yb_ref.shape[1:], F32)

    @pl.when(b0 <= nused_ref[0])
    def _():
        for j in range(STEP_BLOCKS):
            b = b0 + j
            rows = slice(j * MOE_BLOCK, (j + 1) * MOE_BLOCK)
            _stage_weights(b, be_ref, nused_ref, gidx_ref, nxt_ref, (wd_hbm,), wbuf_ref, sem_ref,
                           (wdb_ref,))
            yb = jnp.dot(act_ref[rows, :], wdb_ref[...],
                         preferred_element_type=F32) * _row_to_column(rw_ref[0, :, rows])
            _store_tiled_rows(yb_ref, (j % 2,), yb)
            _scatter_add_rows(tok_ref, jnp.maximum(b - 1, 0), y_ref, yb_ref, (j + 1) % 2)

    @pl.when((i == pl.num_programs(0) - 1) & (nused_ref[0] >= nblk))
    def _():
        _scatter_add_rows(tok_ref, nblk - 1, y_ref, yb_ref, (STEP_BLOCKS - 1) % 2)


def _experts_down(act, row_tok, row_w, blk_exp, nused, gidx, nxt, w_down, t_rows):
    nblk = blk_exp.shape[0]
    step_rows = STEP_BLOCKS * MOE_BLOCK
    return pl.pallas_call(
        _down_kernel,
        grid_spec=pltpu.PrefetchScalarGridSpec(
            num_scalar_prefetch=5,
            grid=(nblk // STEP_BLOCKS,),
            in_specs=[pl.BlockSpec((step_rows, D_EXPERT), lambda i, *_: (i, 0)),
                      pl.BlockSpec((1, 1, step_rows), lambda i, *_: (i, 0, 0)),
                      pl.BlockSpec(memory_space=pl.ANY)],
            out_specs=pl.BlockSpec(memory_space=pltpu.VMEM),
            scratch_shapes=[pltpu.VMEM((2, MOE_BLOCK * ROW_SUB, 128), F32),
                            pltpu.VMEM((WEIGHT_SLOTS, 1, D_EXPERT, D_MODEL), F32),
                            pltpu.VMEM((D_EXPERT, D_MODEL), BF),
                            pltpu.SemaphoreType.DMA((WEIGHT_SLOTS, 1))],
        ),
        out_shape=jax.ShapeDtypeStruct((t_rows, ROW_SUB, 128), F32),
        compiler_params=_params("arbitrary"),
        name="experts_down",
    )(blk_exp, nused, gidx, nxt, row_tok, act, row_w, w_down)


def _final_kernel(h_ref, yr_ref, wg_ref, wu_ref, wd_ref, g_ref, b_ref, o_ref, *, alpha):
    h = _load_tiled_rows(h_ref, (), o_ref.shape[0])
    hb = h.astype(BF)
    g = jnp.dot(hb, wg_ref[...], preferred_element_type=F32)
    u = jnp.dot(hb, wu_ref[...], preferred_element_type=F32)
    shared = jnp.dot(((g * jax.nn.sigmoid(g)) * u).astype(BF), wd_ref[...],
                     preferred_element_type=F32)
    f = _load_tiled_rows(yr_ref, (), h.shape[0]) + shared
    o_ref[...] = _layer_norm(alpha * h + f, g_ref[...], b_ref[...])


def _final(ht, yr, wg_bf, wu_bf, wd_bf, g, b, alpha, tm):
    t = ht.shape[0] // ROW_SUB
    tiled = pl.BlockSpec((tm * ROW_SUB, 128), lambda i: (i, 0))
    full = lambda shape: pl.BlockSpec(shape, lambda i: (0, 0))
    return pl.pallas_call(
        functools.partial(_final_kernel, alpha=alpha),
        grid=(t // tm,),
        in_specs=[tiled, tiled, full((1024, 256)), full((1024, 256)), full((256, 1024)),
                  full((1, 1024)), full((1, 1024))],
        out_specs=pl.BlockSpec((tm, 1024), lambda i: (i, 0)),
        out_shape=jax.ShapeDtypeStruct((t, 1024), F32),
        compiler_params=_params("parallel"),
        name="shared_ln2",
    )(ht, yr, wg_bf, wu_bf, wd_bf, g, b)


def _moe_ln2(ht, tables, w_gate, w_up, w_down, wsg_bf, wsu_bf, wsd_bf, g, b, alpha):
    t = ht.shape[0] // ROW_SUB
    row_tok, row_w, blk_exp, nused, gidx, nxt = tables
    act = _experts_up(ht.reshape(t, ROW_SUB, 128), jnp.minimum(row_tok, t - 1), blk_exp, nused, gidx,
                      nxt, w_gate, w_up)
    yr = _experts_down(act, row_tok, row_w, blk_exp, nused, gidx, nxt, w_down, t + 8)
    return _final(ht, yr.reshape((t + 8) * ROW_SUB, 128), wsg_bf, wsu_bf, wsd_bf, g, b, alpha, 256)


def _permute_w_in(w_in):
    rq, rk, rv, rg, sq, sk, sv, mq, gl = jnp.split(
        w_in, [512, 1024, 2048, 3072, 4096, 4352, 4608, 5632], axis=-1)
    return jnp.concatenate([rq, rk, rv, rg, sq, mq, gl, sk, sv], -1)


def kernel(x_prompt, x_sample, mem_prompt, cache_ret_state, cache_swa_k, cache_swa_v, cache_mem_k,
           cache_mem_v, w_in, swa_sinks, w_mem_kv, w_branch, w_o, ln1_g, ln1_b, w_router,
           router_bias, w_gate_e, w_up_e, w_down_e, w_sh_gate, w_sh_up, w_sh_down, ln2_g, ln2_b):
    depth = w_in.shape[0]
    assert depth == 1
    alpha = (2.0 * depth) ** 0.25
    bp, lp, d = x_prompt.shape
    bs, ls, _ = x_sample.shape
    l = 0

    w_in_bf = _permute_w_in(w_in[l]).astype(BF)
    sinks = swa_sinks[l]
    wb_bf = w_branch[l].astype(BF)
    wo_bf = w_o[l].astype(BF)
    g1, b1 = ln1_g[l].reshape(1, d), ln1_b[l].reshape(1, d)
    g2, b2 = ln2_g[l].reshape(1, d), ln2_b[l].reshape(1, d)

    tp = bp * lp
    xp2 = x_prompt.reshape(tp, d)
    hp2 = _proj(xp2, w_in_bf, 1024, PROJ_TN)
    hp3 = hp2.reshape(bp, lp, D_IN)
    mkv = _proj(mem_prompt.reshape(bp * N_MEM, d), w_mem_kv[l].astype(BF), N_MEM, 512)
    mk_p, mv_p = mkv[:, :1024].reshape(bp, N_MEM, 1024), mkv[:, 1024:].reshape(bp, N_MEM, 1024)
    rs0 = jnp.zeros((bp, RET_HEADS, RET_DK, RET_DV), F32)
    ro_p, rs_p = _retention(hp3, jnp.arange(lp), rs0, RET_CHUNK, RET_CHUNK, 1)
    so_p = _swa_prompt(hp2, sinks)
    mo_p = _mem_attend(hp3, mk_p, mv_p, 256)
    hmid_p = _merge(ro_p.reshape(tp, d), so_p, mo_p.reshape(tp, d), hp2, xp2, wb_bf, wo_bf,
                    g1, b1, alpha, 256)

    ts = bs * SAMPLE_PAD
    xs3 = jnp.pad(x_sample, ((0, 0), (0, SAMPLE_PAD - ls), (0, 0)))
    xs2 = xs3.reshape(ts, d)
    hs2 = _proj(xs2, w_in_bf, ts, PROJ_TN)
    hs3 = hs2.reshape(bs, SAMPLE_PAD, D_IN)
    pos_s = PAST_LEN + jnp.arange(SAMPLE_PAD)
    ro_s, rs_s = _retention(hs3, pos_s, cache_ret_state.reshape(bs, RET_HEADS, RET_DK, RET_DV), ls,
                            SAMPLE_PAD, 4)
    w_buf = cache_swa_k.shape[2]
    prev_k = cache_swa_k.reshape(bs, w_buf, SWA_KV_HEADS * SWA_DH)
    prev_v = cache_swa_v.reshape(bs, w_buf, SWA_KV_HEADS * SWA_DH)
    so_s = _swa_sample(hs3, prev_k, prev_v, sinks, ls)
    mo_s = _mem_attend(hs3, cache_mem_k.reshape(bs, N_MEM, 1024),
                       cache_mem_v.reshape(bs, N_MEM, 1024), SAMPLE_PAD)
    hmid_s = _merge(ro_s.reshape(ts, d), so_s.reshape(ts, d), mo_s.reshape(ts, d), hs2, xs2,
                    wb_bf, wo_bf, g1, b1, alpha, 256)
    hmid_s = hmid_s.reshape(bs, SAMPLE_PAD, ROW_SUB, 128)[:, :ls].reshape(bs * ls * ROW_SUB, 128)

    t_all = tp + bs * ls
    assert t_all % (2 * ROUTER_TILE) == 0 and t_all // 2 <= tp
    th = t_all // 2
    groups = (hmid_p[:th * ROW_SUB], jnp.concatenate([hmid_p[th * ROW_SUB:], hmid_s], 0))
    wr_t_bf, bias_col = w_router[l].T.astype(BF), router_bias[l].reshape(N_EXPERTS, 1)
    routed = [_router(ht, wr_t_bf, bias_col, ROUTER_TILE) for ht in groups]
    tables = jax.vmap(functools.partial(_dispatch, t=th))(
        jnp.stack([e.T for e, _ in routed]), jnp.stack([w.T for _, w in routed]),
        jnp.stack([_sorted_keys(e.T) for e, _ in routed]))
    y_a, y_b = (
        _moe_ln2(ht, jax.tree.map(lambda a: a[i], tables), w_gate_e[l], w_up_e[l], w_down_e[l],
                 w_sh_gate[l].astype(BF), w_sh_up[l].astype(BF), w_sh_down[l].astype(BF), g2, b2, alpha)
        for i, ht in enumerate(groups))

    y_p = jnp.concatenate([y_a, y_b[:tp - th]], 0).reshape(bp, lp, d)
    y_s = y_b[tp - th:].reshape(bs, ls, d)
    kv4 = lambda a, n: a.reshape(1, a.shape[0], n, SWA_KV_HEADS, SWA_DH)
    k_p = kv4(hp3[:, lp - WINDOW:, COL_SK:COL_SK + 256], WINDOW)
    v_p = kv4(hp3[:, lp - WINDOW:, COL_SV:COL_SV + 256], WINDOW)
    mem4 = lambda a: a.reshape(1, bp, N_MEM, MEM_HEADS, MEM_DH)
    k_s = kv4(jnp.concatenate([prev_k, hs3[:, :ls, COL_SK:COL_SK + 256]], 1)[:, -w_buf:], w_buf)
    v_s = kv4(jnp.concatenate([prev_v, hs3[:, :ls, COL_SV:COL_SV + 256]], 1)[:, -w_buf:], w_buf)
    return (y_p, y_s, rs_p[None], k_p, v_p, mem4(mk_p), mem4(mv_p), rs_s[None], k_s, v_s)
```

```python
import functools

import jax
import jax.numpy as jnp
from jax import lax
from jax.experimental import pallas as pl
from jax.experimental.pallas import tpu as pltpu

BF = jnp.bfloat16
F32 = jnp.float32

D_MODEL = 1024
RET_HEADS = 4
RET_DK = 128
RET_DV = 256
RET_CHUNK = 128
ROPE_BASE = 10000.0
SWA_HEADS = 16
SWA_KV_HEADS = 4
SWA_GROUP = SWA_HEADS // SWA_KV_HEADS
SWA_DH = 64
WINDOW = 128
SWA_BLOCK = 128
N_MEM = 256
MEM_HEADS = 4
MEM_DH = 256
N_BRANCH = 3
N_EXPERTS = 256
TOP_K = 8
N_GROUPS = 8
GROUP_SIZE = N_EXPERTS // N_GROUPS
TOPK_GROUPS = 4
D_EXPERT = 256
ROUTED_SCALE = 2.5
MOE_BLOCK = 128
ROUTER_TILE = 384
ROW_SUB = D_MODEL // 128
STEP_BLOCKS = 4
LN_EPS = 1e-5
NORM_EPS = 1e-6
NEG = -1e30
PAST_LEN = 16384
SAMPLE_PAD = 8

COL_RQ, COL_RK, COL_RV, COL_RG, COL_SQ, COL_MQ, COL_GL, COL_SK, COL_SV = (
    0, 512, 1024, 2048, 3072, 4096, 5120, 8192, 8448)
D_IN = 8704
PROJ_TN = D_IN // 4

VMEM_LIMIT = 56 * 1024 * 1024


def _params(*sem):
    return pltpu.CompilerParams(dimension_semantics=sem, vmem_limit_bytes=VMEM_LIMIT)


def _bdot(a, b):
    return jnp.dot(a.astype(BF), b.astype(BF), preferred_element_type=F32)


def _bdot_nt(a, b):
    return lax.dot_general(a.astype(BF), b.astype(BF), (((1,), (1,)), ((), ())),
                           preferred_element_type=F32)


def _bdot_tn(a, b):
    return lax.dot_general(a.astype(BF), b.astype(BF), (((0,), (0,)), ((), ())),
                           preferred_element_type=F32)


def _layer_norm(z, g, b):
    zc = z - jnp.mean(z, -1, keepdims=True)
    var = jnp.mean(zc * zc, -1, keepdims=True)
    return zc * lax.rsqrt(var + LN_EPS) * g + b


def _load_tiled_rows(ref, idx, m):
    return jnp.concatenate([ref[(*idx, pl.ds(s, m, stride=ROW_SUB), slice(None))]
                            for s in range(ROW_SUB)], axis=1)


def _store_tiled_rows(ref, idx, val):
    m = val.shape[0]
    for s in range(ROW_SUB):
        ref[(*idx, pl.ds(s, m, stride=ROW_SUB), slice(None))] = val[:, s * 128:(s + 1) * 128]


def _proj_kernel(x_ref, w_ref, o_ref, xb_ref):
    @pl.when(pl.program_id(1) == 0)
    def _():
        xb_ref[...] = x_ref[...].astype(BF)

    o_ref[...] = jnp.dot(xb_ref[...], w_ref[...], preferred_element_type=F32)


def _proj(x, w_bf, tm, tn):
    m, k = x.shape
    n = w_bf.shape[1]
    return pl.pallas_call(
        _proj_kernel,
        grid=(m // tm, n // tn),
        in_specs=[pl.BlockSpec((tm, k), lambda i, j: (i, 0)),
                  pl.BlockSpec((k, tn), lambda i, j: (0, j))],
        out_specs=pl.BlockSpec((tm, tn), lambda i, j: (i, j)),
        out_shape=jax.ShapeDtypeStruct((m, n), F32),
        scratch_shapes=[pltpu.VMEM((tm, k), BF)],
        compiler_params=_params("parallel", "arbitrary"),
        name="proj",
    )(x, w_bf)


def _ret_tables(c_real, c_pad):
    lg = jnp.log1p(-jnp.exp2(-5.0 - jnp.arange(RET_HEADS, dtype=F32)))
    idx = jnp.arange(c_pad, dtype=F32)
    real = idx < c_real
    rel = idx[:, None] - idx[None, :]
    intra = jnp.where(rel >= 0, jnp.exp(lg[:, None, None] * jnp.maximum(rel, 0.0)), 0.0)
    intra = jnp.where(real[None, :, None] & real[None, None, :], intra, 0.0)
    q_dec = jnp.where(real[None, :], jnp.exp((idx[None, :] + 1.0) * lg[:, None]), 0.0)
    k_dec = jnp.where(real[None, :], jnp.exp((c_real - 1.0 - idx)[None, :] * lg[:, None]), 0.0)
    c_dec = jnp.exp(c_real * lg)
    bc = lambda t: jnp.broadcast_to(t[:, :, None], (RET_HEADS, c_pad, RET_DK))
    return intra, bc(q_dec), bc(k_dec), c_dec


def _rope_tables(pos):
    half = RET_DK // 2
    inv_freq = 1.0 / (ROPE_BASE ** (jnp.arange(half, dtype=F32) / half))
    ang = pos.astype(F32)[:, None] * inv_freq[None, :]
    cos, sin = jnp.cos(ang), jnp.sin(ang)
    return jnp.concatenate([cos, cos], -1), jnp.concatenate([-sin, sin], -1)


def _ret_kernel(cdec_ref, rq_ref, rk_ref, rv_ref, rg_ref, cos_ref, sin_ref, intra_ref, qdec_ref,
                kdec_ref, s0_ref, o_ref, s_out_ref, s_scr, *, n_chunks):
    c = pl.program_id(1)

    @pl.when(c == 0)
    def _():
        s_scr[...] = s0_ref[...]

    cos2 = cos_ref[...]
    sin2 = sin_ref[...]

    def rot(x):
        return x * cos2 + pltpu.roll(x, RET_DK // 2, 1) * sin2

    for bi in range(rq_ref.shape[0]):
        for h in range(RET_HEADS):
            q = rot(rq_ref[bi, :, h * RET_DK:(h + 1) * RET_DK])
            k = rot(rk_ref[bi, :, h * RET_DK:(h + 1) * RET_DK]) * (RET_DK ** -0.5)
            v = rv_ref[bi, :, h * RET_DV:(h + 1) * RET_DV].astype(BF)
            s_old = s_scr[bi, h]
            a = _bdot_nt(q, k) * intra_ref[h]
            o = _bdot(a, v) + _bdot(q * qdec_ref[h], s_old)
            s_scr[bi, h] = s_old * cdec_ref[h] + _bdot_tn(k * kdec_ref[h], v)
            o = o * lax.rsqrt(jnp.mean(o * o, -1, keepdims=True) + NORM_EPS)
            g = rg_ref[bi, :, h * RET_DV:(h + 1) * RET_DV]
            o_ref[bi, :, h * RET_DV:(h + 1) * RET_DV] = o * (g * jax.nn.sigmoid(g))

    @pl.when(c == n_chunks - 1)
    def _():
        s_out_ref[...] = s_scr[...]


def _retention(h3, pos, state0, c_real, c_pad, tb):
    b, l, _ = h3.shape
    n_chunks = l // c_pad
    intra, qdec, kdec, cdec = _ret_tables(c_real, c_pad)
    cos2, sin2 = _rope_tables(pos)
    full3 = lambda shape: pl.BlockSpec(shape, lambda i, c: (0, 0, 0))
    return pl.pallas_call(
        functools.partial(_ret_kernel, n_chunks=n_chunks),
        grid=(b // tb, n_chunks),
        in_specs=[
            pl.BlockSpec(memory_space=pltpu.SMEM),
            pl.BlockSpec((tb, c_pad, 512), lambda i, c: (i, c, COL_RQ // 512)),
            pl.BlockSpec((tb, c_pad, 512), lambda i, c: (i, c, COL_RK // 512)),
            pl.BlockSpec((tb, c_pad, 1024), lambda i, c: (i, c, COL_RV // 1024)),
            pl.BlockSpec((tb, c_pad, 1024), lambda i, c: (i, c, COL_RG // 1024)),
            pl.BlockSpec((c_pad, RET_DK), lambda i, c: (c, 0)),
            pl.BlockSpec((c_pad, RET_DK), lambda i, c: (c, 0)),
            full3((RET_HEADS, c_pad, c_pad)),
            full3((RET_HEADS, c_pad, RET_DK)),
            full3((RET_HEADS, c_pad, RET_DK)),
            pl.BlockSpec((tb, RET_HEADS, RET_DK, RET_DV), lambda i, c: (i, 0, 0, 0)),
        ],
        out_specs=[
            pl.BlockSpec((tb, c_pad, 1024), lambda i, c: (i, c, 0)),
            pl.BlockSpec((tb, RET_HEADS, RET_DK, RET_DV), lambda i, c: (i, 0, 0, 0)),
        ],
        out_shape=[jax.ShapeDtypeStruct((b, l, 1024), F32),
                   jax.ShapeDtypeStruct((b, RET_HEADS, RET_DK, RET_DV), F32)],
        scratch_shapes=[pltpu.VMEM((tb, RET_HEADS, RET_DK, RET_DV), F32)],
        compiler_params=_params("parallel", "arbitrary"),
        name="retention",
    )(cdec, h3, h3, h3, h3, cos2, sin2, intra, qdec, kdec, state0)


def _sink_softmax(s, sink):
    m = jnp.maximum(jnp.max(s, -1, keepdims=True), sink)
    p = jnp.exp(s - m)
    return p / (jnp.sum(p, -1, keepdims=True) + jnp.exp(sink - m))


def _swa_prompt_kernel(sinks_ref, q_ref, kp_ref, kc_ref, vp_ref, vc_ref, o_ref):
    n = pl.program_id(0)
    rows = SWA_GROUP * SWA_BLOCK
    qi = lax.broadcasted_iota(jnp.int32, (rows, 2 * SWA_BLOCK), 0) % SWA_BLOCK
    kj = lax.broadcasted_iota(jnp.int32, (rows, 2 * SWA_BLOCK), 1)
    rel = SWA_BLOCK + qi - kj
    valid = (rel >= 0) & (rel <= WINDOW) & ((kj >= SWA_BLOCK) | (n > 0))
    row_head = lax.broadcasted_iota(jnp.int32, (rows, 1), 0) // SWA_BLOCK
    for h in range(SWA_KV_HEADS):
        sl = slice(h * SWA_DH, (h + 1) * SWA_DH)
        k2 = jnp.concatenate([kp_ref[:, sl], kc_ref[:, sl]], 0).astype(BF)
        v2 = jnp.concatenate([vp_ref[:, sl], vc_ref[:, sl]], 0).astype(BF)
        heads = [h * SWA_GROUP + g for g in range(SWA_GROUP)]
        q = jnp.concatenate([q_ref[:, hq * SWA_DH:(hq + 1) * SWA_DH] for hq in heads], axis=0)
        sink = jnp.zeros((rows, 1), F32)
        for g, hq in enumerate(heads):
            sink = jnp.where(row_head == g, sinks_ref[hq], sink)
        s = _bdot_nt(q, k2) * (SWA_DH ** -0.5)
        s = jnp.where(valid, s, NEG)
        o = _bdot(_sink_softmax(s, sink), v2)
        for g, hq in enumerate(heads):
            o_ref[:, hq * SWA_DH:(hq + 1) * SWA_DH] = o[g * SWA_BLOCK:(g + 1) * SWA_BLOCK, :]


def _swa_prompt(h2, sinks):
    t = h2.shape[0]
    nb = t // SWA_BLOCK
    prev = lambda col: (lambda n: (jnp.maximum(n - 1, 0), col))
    cur = lambda col: (lambda n: (n, col))
    ck, cv = COL_SK // 256, COL_SV // 256
    return pl.pallas_call(
        _swa_prompt_kernel,
        grid=(nb,),
        in_specs=[
            pl.BlockSpec(memory_space=pltpu.SMEM),
            pl.BlockSpec((SWA_BLOCK, 1024), cur(COL_SQ // 1024)),
            pl.BlockSpec((SWA_BLOCK, 256), prev(ck)),
            pl.BlockSpec((SWA_BLOCK, 256), cur(ck)),
            pl.BlockSpec((SWA_BLOCK, 256), prev(cv)),
            pl.BlockSpec((SWA_BLOCK, 256), cur(cv)),
        ],
        out_specs=pl.BlockSpec((SWA_BLOCK, 1024), lambda n: (n, 0)),
        out_shape=jax.ShapeDtypeStruct((t, 1024), F32),
        compiler_params=_params("parallel"),
        name="swa_prompt",
    )(sinks, h2, h2, h2, h2, h2)


def _swa_sample_kernel(sinks_ref, q_ref, kn_ref, vn_ref, kp_ref, vp_ref, o_ref, *, n_new):
    tb = q_ref.shape[0]
    w = kp_ref.shape[1]
    p_ = SAMPLE_PAD
    rows = SWA_GROUP * p_
    qi = lax.broadcasted_iota(jnp.int32, (tb, rows, w), 1) % p_
    kj = lax.broadcasted_iota(jnp.int32, (tb, rows, w), 2)
    rel_prev = w + qi - kj
    valid_prev = (rel_prev >= 0) & (rel_prev <= WINDOW)
    qn = lax.broadcasted_iota(jnp.int32, (tb, rows, p_), 1) % p_
    kn = lax.broadcasted_iota(jnp.int32, (tb, rows, p_), 2)
    valid_new = (qn - kn >= 0) & (qn - kn <= WINDOW) & (kn < n_new)
    row_head = lax.broadcasted_iota(jnp.int32, (1, rows, 1), 1) // p_
    bdot = lambda eq, a, b: jnp.einsum(eq, a.astype(BF), b.astype(BF), preferred_element_type=F32)
    for h in range(SWA_KV_HEADS):
        sl = slice(h * SWA_DH, (h + 1) * SWA_DH)
        kp, vp = kp_ref[:, :, sl], vp_ref[:, :, sl]
        kn_h, vn_h = kn_ref[:, :, sl], vn_ref[:, :, sl]
        heads = [h * SWA_GROUP + g for g in range(SWA_GROUP)]
        q = jnp.concatenate([q_ref[:, :, hq * SWA_DH:(hq + 1) * SWA_DH] for hq in heads], axis=1)
        sink = jnp.zeros((1, rows, 1), F32)
        for g, hq in enumerate(heads):
            sink = jnp.where(row_head == g, sinks_ref[hq], sink)
        sp = bdot('bqd,bkd->bqk', q, kp) * (SWA_DH ** -0.5)
        sn = bdot('bqd,bkd->bqk', q, kn_h) * (SWA_DH ** -0.5)
        sp = jnp.where(valid_prev, sp, NEG)
        sn = jnp.where(valid_new, sn, NEG)
        m = jnp.maximum(jnp.maximum(jnp.max(sp, -1, keepdims=True),
                                    jnp.max(sn, -1, keepdims=True)), sink)
        pp = jnp.exp(sp - m)
        pn = jnp.exp(sn - m)
        den = jnp.sum(pp, -1, keepdims=True) + jnp.sum(pn, -1, keepdims=True) + jnp.exp(sink - m)
        o = bdot('bqk,bkd->bqd', pp / den, vp) + bdot('bqk,bkd->bqd', pn / den, vn_h)
        for g, hq in enumerate(heads):
            o_ref[:, :, hq * SWA_DH:(hq + 1) * SWA_DH] = o[:, g * p_:(g + 1) * p_, :]


def _swa_sample(h3, prev_k, prev_v, sinks, n_new, tb=8):
    b = h3.shape[0]
    w = prev_k.shape[1]
    return pl.pallas_call(
        functools.partial(_swa_sample_kernel, n_new=n_new),
        grid=(b // tb,),
        in_specs=[
            pl.BlockSpec(memory_space=pltpu.SMEM),
            pl.BlockSpec((tb, SAMPLE_PAD, 1024), lambda i: (i, 0, COL_SQ // 1024)),
            pl.BlockSpec((tb, SAMPLE_PAD, 256), lambda i: (i, 0, COL_SK // 256)),
            pl.BlockSpec((tb, SAMPLE_PAD, 256), lambda i: (i, 0, COL_SV // 256)),
            pl.BlockSpec((tb, w, 256), lambda i: (i, 0, 0)),
            pl.BlockSpec((tb, w, 256), lambda i: (i, 0, 0)),
        ],
        out_specs=pl.BlockSpec((tb, SAMPLE_PAD, 1024), lambda i: (i, 0, 0)),
        out_shape=jax.ShapeDtypeStruct((b, SAMPLE_PAD, 1024), F32),
        compiler_params=_params("parallel"),
        name="swa_sample",
    )(sinks, h3, h3, h3, prev_k, prev_v)


def _mem_head(q, k, v):
    s = _bdot_nt(q, k) * (MEM_DH ** -0.5)
    m = jnp.max(s, -1, keepdims=True)
    e = jnp.exp(s - m)
    return _bdot(e / jnp.sum(e, -1, keepdims=True), v)


def _mem_kernel(q_ref, mk_ref, mv_ref, o_ref):
    for h in range(MEM_HEADS):
        sl = slice(h * MEM_DH, (h + 1) * MEM_DH)
        o_ref[0, :, sl] = _mem_head(q_ref[0, :, sl], mk_ref[0, :, sl], mv_ref[0, :, sl])


def _mem_cached_kernel(q_ref, mk_hbm, mv_hbm, o_ref, kv_ref, sem_ref):
    b = pl.program_id(0)
    slot = b % 2

    def copies(seq, s):
        return [pltpu.make_async_copy(src.at[seq, :, h, :], kv_ref.at[s, j, h], sem_ref.at[s, j, h])
                for j, src in enumerate((mk_hbm, mv_hbm)) for h in range(MEM_HEADS)]

    @pl.when(b == 0)
    def _():
        for c in copies(0, 0):
            c.start()

    @pl.when(b + 1 < pl.num_programs(0))
    def _():
        for c in copies(b + 1, 1 - slot):
            c.start()

    for c in copies(b, slot):
        c.wait()
    for h in range(MEM_HEADS):
        sl = slice(h * MEM_DH, (h + 1) * MEM_DH)
        o_ref[0, :, sl] = _mem_head(q_ref[0, :, sl], kv_ref[slot, 0, h], kv_ref[slot, 1, h])


def _mem_attend_cached(h3, mk, mv):
    b, l, _ = h3.shape
    return pl.pallas_call(
        _mem_cached_kernel,
        grid=(b,),
        in_specs=[
            pl.BlockSpec((1, l, 1024), lambda i: (i, 0, COL_MQ // 1024)),
            pl.BlockSpec(memory_space=pl.ANY),
            pl.BlockSpec(memory_space=pl.ANY),
        ],
        out_specs=pl.BlockSpec((1, l, 1024), lambda i: (i, 0, 0)),
        out_shape=jax.ShapeDtypeStruct((b, l, 1024), F32),
        scratch_shapes=[pltpu.VMEM((2, 2, MEM_HEADS, N_MEM, MEM_DH), F32),
                        pltpu.SemaphoreType.DMA((2, 2, MEM_HEADS))],
        compiler_params=_params("arbitrary"),
        name="mem_attend_cached",
    )(h3, mk, mv)


def _mem_attend(h3, mk, mv, tl):
    b, l, _ = h3.shape
    return pl.pallas_call(
        _mem_kernel,
        grid=(b, l // tl),
        in_specs=[
            pl.BlockSpec((1, tl, 1024), lambda i, j: (i, j, COL_MQ // 1024)),
            pl.BlockSpec((1, N_MEM, 1024), lambda i, j: (i, 0, 0)),
            pl.BlockSpec((1, N_MEM, 1024), lambda i, j: (i, 0, 0)),
        ],
        out_specs=pl.BlockSpec((1, tl, 1024), lambda i, j: (i, j, 0)),
        out_shape=jax.ShapeDtypeStruct((b, l, 1024), F32),
        compiler_params=_params("parallel", "parallel"),
        name="mem_attend",
    )(h3, mk, mv)


def _merge_kernel(ro_ref, so_ref, mo_ref, g0_ref, g1_ref, g2_ref, x_ref, wb_ref, wo_ref, g_ref,
                  b_ref, o_ref, *, alpha):
    acc = None
    for n, (br, gl) in enumerate(((ro_ref, g0_ref), (so_ref, g1_ref), (mo_ref, g2_ref))):
        term = jax.nn.sigmoid(gl[...]) * jnp.dot(br[...].astype(BF), wb_ref[n],
                                                 preferred_element_type=F32)
        acc = term if acc is None else acc + term
    a = jnp.dot(acc.astype(BF), wo_ref[...], preferred_element_type=F32)
    _store_tiled_rows(o_ref, (), _layer_norm(alpha * x_ref[...] + a, g_ref[...], b_ref[...]))


def _merge(ro, so, mo, h2, x2, wb_bf, wo_bf, g, b, alpha, tm):
    t = x2.shape[0]
    tile = lambda col: pl.BlockSpec((tm, 1024), lambda i: (i, col))
    gl0 = COL_GL // 1024
    return pl.pallas_call(
        functools.partial(_merge_kernel, alpha=alpha),
        grid=(t // tm,),
        in_specs=[tile(0), tile(0), tile(0), tile(gl0), tile(gl0 + 1), tile(gl0 + 2), tile(0),
                  pl.BlockSpec((N_BRANCH, 1024, 1024), lambda i: (0, 0, 0)),
                  pl.BlockSpec((1024, 1024), lambda i: (0, 0)),
                  pl.BlockSpec((1, 1024), lambda i: (0, 0)),
                  pl.BlockSpec((1, 1024), lambda i: (0, 0))],
        out_specs=pl.BlockSpec((tm * ROW_SUB, 128), lambda i: (i, 0)),
        out_shape=jax.ShapeDtypeStruct((t * ROW_SUB, 128), F32),
        compiler_params=_params("parallel"),
        name="merge_ln1",
    )(ro, so, mo, h2, h2, h2, x2, wb_bf, wo_bf, g, b)


def _first_index_of_max(v, iota, big, axes):
    m = jnp.max(v, axis=axes, keepdims=True)
    idx = jnp.min(jnp.where(v == m, iota, big), axis=axes, keepdims=True)
    return m, idx


def _router_kernel(x_ref, wr_ref, bias_ref, eidx_ref, ew_ref):
    tt = x_ref.shape[0] // ROW_SUB
    x = _load_tiled_rows(x_ref, (), tt).astype(BF)
    logits = lax.dot_general(wr_ref[...], x, (((1,), (1,)), ((), ())),
                             preferred_element_type=F32)
    s = jax.nn.sigmoid(logits).reshape(N_GROUPS, GROUP_SIZE, tt)
    sb = s + bias_ref[...].reshape(N_GROUPS, GROUP_SIZE, 1)
    ninf = -jnp.inf
    r_iota = lax.broadcasted_iota(jnp.int32, sb.shape, 1)
    m1, i1 = _first_index_of_max(sb, r_iota, GROUP_SIZE, 1)
    m2 = jnp.max(jnp.where(r_iota == i1, ninf, sb), axis=1, keepdims=True)
    gsc = (m1 + m2).reshape(N_GROUPS, tt)
    g_iota = lax.broadcasted_iota(jnp.int32, gsc.shape, 0)
    gmask = jnp.zeros(gsc.shape, jnp.bool_)
    for _ in range(TOPK_GROUPS):
        _, gi = _first_index_of_max(gsc, g_iota, N_GROUPS, 0)
        hit = g_iota == gi
        gmask = gmask | hit
        gsc = jnp.where(hit, ninf, gsc)
    cand = jnp.where(gmask.reshape(N_GROUPS, 1, tt), sb, ninf)
    e_iota = lax.broadcasted_iota(jnp.int32, sb.shape, 0) * GROUP_SIZE + r_iota
    idxs, ws = [], []
    for _ in range(TOP_K):
        _, ei = _first_index_of_max(cand, e_iota, N_EXPERTS, (0, 1))
        hit = e_iota == ei
        idxs.append(ei.reshape(1, tt))
        ws.append(jnp.sum(jnp.where(hit, s, 0.0), axis=(0, 1)).reshape(1, tt))
        cand = jnp.where(hit, ninf, cand)
    w = jnp.concatenate(ws, 0)
    eidx_ref[...] = jnp.concatenate(idxs, 0)
    ew_ref[...] = w / jnp.sum(w, 0, keepdims=True) * ROUTED_SCALE


def _router(xt, wr_t_bf, bias_col, tt):
    t = xt.shape[0] // ROW_SUB
    return pl.pallas_call(
        _router_kernel,
        grid=(t // tt,),
        in_specs=[pl.BlockSpec((tt * ROW_SUB, 128), lambda i: (i, 0)),
                  pl.BlockSpec((N_EXPERTS, 1024), lambda i: (0, 0)),
                  pl.BlockSpec((N_EXPERTS, 1), lambda i: (0, 0))],
        out_specs=[pl.BlockSpec((TOP_K, tt), lambda i: (0, i)),
                   pl.BlockSpec((TOP_K, tt), lambda i: (0, i))],
        out_shape=[jax.ShapeDtypeStruct((TOP_K, t), jnp.int32),
                   jax.ShapeDtypeStruct((TOP_K, t), F32)],
        compiler_params=_params("parallel"),
        name="router",
    )(xt, wr_t_bf, bias_col)


IDX_BITS = 18


def _sorted_keys(eidx):
    flat_e = eidx.reshape(-1)
    assert flat_e.shape[0] < (1 << IDX_BITS)
    return jnp.sort(flat_e * (1 << IDX_BITS) + jnp.arange(flat_e.shape[0], dtype=jnp.int32))


def _dispatch(eidx, ew, skey, t):
    a = t * TOP_K
    nblk = -(-a // MOE_BLOCK) + N_EXPERTS
    assert nblk % STEP_BLOCKS == 0 and STEP_BLOCKS % 2 == 0
    flat_e = eidx.reshape(-1)
    si = skey & ((1 << IDX_BITS) - 1)
    experts = jnp.arange(N_EXPERTS, dtype=jnp.int32)
    counts = jnp.sum((flat_e[None, :] == experts[:, None]).astype(jnp.int32), axis=1)
    grp_start = jnp.cumsum(counts) - counts
    padded = (counts + MOE_BLOCK - 1) // MOE_BLOCK * MOE_BLOCK
    pad_end = jnp.cumsum(padded)
    pad_start = pad_end - padded
    blk_first = jnp.arange(nblk, dtype=jnp.int32) * MOE_BLOCK
    blk_exp = jnp.minimum(jnp.sum((pad_end[None, :] <= blk_first[:, None]).astype(jnp.int32), axis=1),
                          N_EXPERTS - 1)
    off = (jnp.arange(nblk * MOE_BLOCK, dtype=jnp.int32).reshape(nblk, MOE_BLOCK)
           - pad_start[blk_exp][:, None])
    valid = off < counts[blk_exp][:, None]
    src = jnp.clip(grp_start[blk_exp][:, None] + off, 0, a - 1)
    row_si = si[src]
    row_tok = jnp.where(valid, row_si >> 3, t).astype(jnp.int32)
    row_w = jnp.where(valid, ew.reshape(-1)[row_si], 0.0)
    nused = (pad_end[-1] // MOE_BLOCK).astype(jnp.int32).reshape(1)
    gidx, nxt = _group_tables(counts, blk_exp)
    step_rows = STEP_BLOCKS * MOE_BLOCK
    return (row_tok.reshape(-1), row_w.reshape(nblk // STEP_BLOCKS, 1, step_rows), blk_exp, nused,
            gidx, nxt)


def _group_tables(counts, blk_exp):
    nonempty = counts > 0
    gidx = (jnp.cumsum(nonempty.astype(jnp.int32)) - 1)[blk_exp]
    experts = jnp.arange(N_EXPERTS, dtype=jnp.int32)
    cand = jnp.where(nonempty, experts, N_EXPERTS)
    later = lax.cummin(cand, axis=0, reverse=True)
    nxt = jnp.concatenate([later[1:], jnp.full((2,), N_EXPERTS, jnp.int32)])
    nxt1 = nxt[blk_exp]
    nxt2 = nxt[jnp.minimum(nxt1, N_EXPERTS)]
    none = lambda a: jnp.where(a >= N_EXPERTS, -1, a).astype(jnp.int32)
    return gidx.astype(jnp.int32), jnp.concatenate([none(nxt1), none(nxt2)])


WEIGHT_SLOTS = 3


def _weight_copies(hbm_refs, buf_ref, sem_ref, e, slot):
    return [pltpu.make_async_copy(h.at[e], buf_ref.at[slot, k], sem_ref.at[slot, k])
            for k, h in enumerate(hbm_refs)]


def _stage_weights(b, be_ref, nused_ref, gidx_ref, nxt_ref, hbm_refs, buf_ref, sem_ref, cache_refs):
    first = ((b == 0) | (be_ref[b] != be_ref[jnp.maximum(b - 1, 0)])) & (b < nused_ref[0])
    nblk = be_ref.shape[0]

    def start(e, slot):
        for c in _weight_copies(hbm_refs, buf_ref, sem_ref, e, slot):
            c.start()

    @pl.when(first)
    def _():
        slot = gidx_ref[b] % WEIGHT_SLOTS

        @pl.when(b == 0)
        def _():
            start(be_ref[0], 0)

            @pl.when(nxt_ref[0] >= 0)
            def _():
                start(nxt_ref[0], 1)

        for c in _weight_copies(hbm_refs, buf_ref, sem_ref, be_ref[b], slot):
            c.wait()
        nxt2 = nxt_ref[nblk + b]

        @pl.when(nxt2 >= 0)
        def _():
            start(nxt2, (slot + 2) % WEIGHT_SLOTS)

        for k, cache in enumerate(cache_refs):
            cache[...] = buf_ref[slot, k].astype(BF)


def _gather_rows(tok_ref, b, x_ref, xg_ref, slot):
    for r in range(MOE_BLOCK):
        xg_ref[slot, r * ROW_SUB:(r + 1) * ROW_SUB, :] = x_ref[tok_ref[b * MOE_BLOCK + r]]


def _up_kernel(be_ref, nused_ref, gidx_ref, nxt_ref, tok_ref, x_ref, wg_hbm, wu_hbm, act_ref,
               xg_ref, wbuf_ref, wgb_ref, wub_ref, sem_ref):
    i = pl.program_id(0)
    nblk = pl.num_programs(0) * STEP_BLOCKS
    b0 = i * STEP_BLOCKS

    @pl.when(i == 0)
    def _():
        _gather_rows(tok_ref, 0, x_ref, xg_ref, 0)

    @pl.when(b0 < nused_ref[0])
    def _():
        for j in range(STEP_BLOCKS):
            b = b0 + j
            _stage_weights(b, be_ref, nused_ref, gidx_ref, nxt_ref, (wg_hbm, wu_hbm), wbuf_ref,
                           sem_ref, (wgb_ref, wub_ref))
            _gather_rows(tok_ref, jnp.minimum(b + 1, nblk - 1), x_ref, xg_ref, (j + 1) % 2)
            x = _load_tiled_rows(xg_ref, (j % 2,), MOE_BLOCK).astype(BF)
            g = jnp.dot(x, wgb_ref[...], preferred_element_type=F32)
            u = jnp.dot(x, wub_ref[...], preferred_element_type=F32)
            act_ref[j * MOE_BLOCK:(j + 1) * MOE_BLOCK, :] = ((g * jax.nn.sigmoid(g)) * u).astype(BF)

    @pl.when(b0 >= nused_ref[0])
    def _():
        act_ref[...] = jnp.zeros_like(act_ref)


def _experts_up(x, row_tok, blk_exp, nused, gidx, nxt, w_gate, w_up):
    nblk = blk_exp.shape[0]
    step_rows = STEP_BLOCKS * MOE_BLOCK
    return pl.pallas_call(
        _up_kernel,
        grid_spec=pltpu.PrefetchScalarGridSpec(
            num_scalar_prefetch=5,
            grid=(nblk // STEP_BLOCKS,),
            in_specs=[pl.BlockSpec(memory_space=pltpu.VMEM),
                      pl.BlockSpec(memory_space=pl.ANY),
                      pl.BlockSpec(memory_space=pl.ANY)],
            out_specs=pl.BlockSpec((step_rows, D_EXPERT), lambda i, *_: (i, 0)),
            scratch_shapes=[pltpu.VMEM((2, MOE_BLOCK * ROW_SUB, 128), F32),
                            pltpu.VMEM((WEIGHT_SLOTS, 2, D_MODEL, D_EXPERT), F32),
                            pltpu.VMEM((D_MODEL, D_EXPERT), BF),
                            pltpu.VMEM((D_MODEL, D_EXPERT), BF),
                            pltpu.SemaphoreType.DMA((WEIGHT_SLOTS, 2))],
        ),
        out_shape=jax.ShapeDtypeStruct((nblk * MOE_BLOCK, D_EXPERT), BF),
        compiler_params=_params("arbitrary"),
        name="experts_up",
    )(blk_exp, nused, gidx, nxt, row_tok, x, w_gate, w_up)


SCATTER_GROUP = 8


def _row_to_column(row):
    n = row.shape[1]
    eye = lax.broadcasted_iota(jnp.int32, (n, n), 0) == lax.broadcasted_iota(jnp.int32, (n, n), 1)
    return jnp.sum(jnp.where(eye, jnp.broadcast_to(row, (n, n)), 0.0), axis=1, keepdims=True)


def _scatter_add_rows(tok_ref, b, y_ref, yb_ref, slot):
    for r0 in range(0, MOE_BLOCK, SCATTER_GROUP):
        rs = range(r0, r0 + SCATTER_GROUP)
        toks = [tok_ref[b * MOE_BLOCK + r] for r in rs]
        new = [y_ref[t] + yb_ref[slot, r * ROW_SUB:(r + 1) * ROW_SUB, :] for r, t in zip(rs, toks)]
        for t, v in zip(toks, new):
            y_ref[t] = v


def _down_kernel(be_ref, nused_ref, gidx_ref, nxt_ref, tok_ref, act_ref, rw_ref, wd_hbm, y_ref,
                 yb_ref, wbuf_ref, wdb_ref, sem_ref):
    i = pl.program_id(0)
    nblk = pl.num_programs(0) * STEP_BLOCKS
    b0 = i * STEP_BLOCKS

    @pl.when(i == 0)
    def _():
        y_ref[...] = jnp.zeros_like(y_ref)
        yb_ref[1] = jnp.zeros(yb_ref.shape[1:], F32)

    @pl.when(b0 <= nused_ref[0])
    def _():
        for j in range(STEP_BLOCKS):
            b = b0 + j
            rows = slice(j * MOE_BLOCK, (j + 1) * MOE_BLOCK)
            _stage_weights(b, be_ref, nused_ref, gidx_ref, nxt_ref, (wd_hbm,), wbuf_ref, sem_ref,
                           (wdb_ref,))
            yb = jnp.dot(act_ref[rows, :], wdb_ref[...],
                         preferred_element_type=F32) * _row_to_column(rw_ref[0, :, rows])
            _store_tiled_rows(yb_ref, (j % 2,), yb)
            _scatter_add_rows(tok_ref, jnp.maximum(b - 1, 0), y_ref, yb_ref, (j + 1) % 2)

    @pl.when((i == pl.num_programs(0) - 1) & (nused_ref[0] >= nblk))
    def _():
        _scatter_add_rows(tok_ref, nblk - 1, y_ref, yb_ref, (STEP_BLOCKS - 1) % 2)


def _experts_down(act, row_tok, row_w, blk_exp, nused, gidx, nxt, w_down, t_rows):
    nblk = blk_exp.shape[0]
    step_rows = STEP_BLOCKS * MOE_BLOCK
    return pl.pallas_call(
        _down_kernel,
        grid_spec=pltpu.PrefetchScalarGridSpec(
            num_scalar_prefetch=5,
            grid=(nblk // STEP_BLOCKS,),
            in_specs=[pl.BlockSpec((step_rows, D_EXPERT), lambda i, *_: (i, 0)),
                      pl.BlockSpec((1, 1, step_rows), lambda i, *_: (i, 0, 0)),
                      pl.BlockSpec(memory_space=pl.ANY)],
            out_specs=pl.BlockSpec(memory_space=pltpu.VMEM),
            scratch_shapes=[pltpu.VMEM((2, MOE_BLOCK * ROW_SUB, 128), F32),
                            pltpu.VMEM((WEIGHT_SLOTS, 1, D_EXPERT, D_MODEL), F32),
                            pltpu.VMEM((D_EXPERT, D_MODEL), BF),
                            pltpu.SemaphoreType.DMA((WEIGHT_SLOTS, 1))],
        ),
        out_shape=jax.ShapeDtypeStruct((t_rows, ROW_SUB, 128), F32),
        compiler_params=_params("arbitrary"),
        name="experts_down",
    )(blk_exp, nused, gidx, nxt, row_tok, act, row_w, w_down)


def _final_kernel(h_ref, yr_ref, wg_ref, wu_ref, wd_ref, g_ref, b_ref, o_ref, *, alpha):
    h = _load_tiled_rows(h_ref, (), o_ref.shape[0])
    hb = h.astype(BF)
    g = jnp.dot(hb, wg_ref[...], preferred_element_type=F32)
    u = jnp.dot(hb, wu_ref[...], preferred_element_type=F32)
    shared = jnp.dot(((g * jax.nn.sigmoid(g)) * u).astype(BF), wd_ref[...],
                     preferred_element_type=F32)
    f = _load_tiled_rows(yr_ref, (), h.shape[0]) + shared
    o_ref[...] = _layer_norm(alpha * h + f, g_ref[...], b_ref[...])


def _final(ht, yr, wg_bf, wu_bf, wd_bf, g, b, alpha, tm):
    t = ht.shape[0] // ROW_SUB
    tiled = pl.BlockSpec((tm * ROW_SUB, 128), lambda i: (i, 0))
    full = lambda shape: pl.BlockSpec(shape, lambda i: (0, 0))
    return pl.pallas_call(
        functools.partial(_final_kernel, alpha=alpha),
        grid=(t // tm,),
        in_specs=[tiled, tiled, full((1024, 256)), full((1024, 256)), full((256, 1024)),
                  full((1, 1024)), full((1, 1024))],
        out_specs=pl.BlockSpec((tm, 1024), lambda i: (i, 0)),
        out_shape=jax.ShapeDtypeStruct((t, 1024), F32),
        compiler_params=_params("parallel"),
        name="shared_ln2",
    )(ht, yr, wg_bf, wu_bf, wd_bf, g, b)


def _moe_ln2(ht, tables, w_gate, w_up, w_down, wsg_bf, wsu_bf, wsd_bf, g, b, alpha):
    t = ht.shape[0] // ROW_SUB
    row_tok, row_w, blk_exp, nused, gidx, nxt = tables
    act = _experts_up(ht.reshape(t, ROW_SUB, 128), jnp.minimum(row_tok, t - 1), blk_exp, nused, gidx,
                      nxt, w_gate, w_up)
    yr = _experts_down(act, row_tok, row_w, blk_exp, nused, gidx, nxt, w_down, t + 8)
    return _final(ht, yr.reshape((t + 8) * ROW_SUB, 128), wsg_bf, wsu_bf, wsd_bf, g, b, alpha, 256)


def _permute_w_in(w_in):
    rq, rk, rv, rg, sq, sk, sv, mq, gl = jnp.split(
        w_in, [512, 1024, 2048, 3072, 4096, 4352, 4608, 5632], axis=-1)
    return jnp.concatenate([rq, rk, rv, rg, sq, mq, gl, sk, sv], -1)


def kernel(x_prompt, x_sample, mem_prompt, cache_ret_state, cache_swa_k, cache_swa_v, cache_mem_k,
           cache_mem_v, w_in, swa_sinks, w_mem_kv, w_branch, w_o, ln1_g, ln1_b, w_router,
           router_bias, w_gate_e, w_up_e, w_down_e, w_sh_gate, w_sh_up, w_sh_down, ln2_g, ln2_b):
    depth = w_in.shape[0]
    assert depth == 1
    alpha = (2.0 * depth) ** 0.25
    bp, lp, d = x_prompt.shape
    bs, ls, _ = x_sample.shape
    l = 0

    w_in_bf = _permute_w_in(w_in[l]).astype(BF)
    sinks = swa_sinks[l]
    wb_bf = w_branch[l].astype(BF)
    wo_bf = w_o[l].astype(BF)
    g1, b1 = ln1_g[l].reshape(1, d), ln1_b[l].reshape(1, d)
    g2, b2 = ln2_g[l].reshape(1, d), ln2_b[l].reshape(1, d)

    tp = bp * lp
    xp2 = x_prompt.reshape(tp, d)
    hp2 = _proj(xp2, w_in_bf, 1024, PROJ_TN)
    hp3 = hp2.reshape(bp, lp, D_IN)
    mkv = _proj(mem_prompt.reshape(bp * N_MEM, d), w_mem_kv[l].astype(BF), N_MEM, 512)
    mk_p, mv_p = mkv[:, :1024].reshape(bp, N_MEM, 1024), mkv[:, 1024:].reshape(bp, N_MEM, 1024)
    rs0 = jnp.zeros((bp, RET_HEADS, RET_DK, RET_DV), F32)
    ro_p, rs_p = _retention(hp3, jnp.arange(lp), rs0, RET_CHUNK, RET_CHUNK, 1)
    so_p = _swa_prompt(hp2, sinks)
    mo_p = _mem_attend(hp3, mk_p, mv_p, 256)
    hmid_p = _merge(ro_p.reshape(tp, d), so_p, mo_p.reshape(tp, d), hp2, xp2, wb_bf, wo_bf,
                    g1, b1, alpha, 256)

    ts = bs * SAMPLE_PAD
    xs3 = jnp.pad(x_sample, ((0, 0), (0, SAMPLE_PAD - ls), (0, 0)))
    xs2 = xs3.reshape(ts, d)
    hs2 = _proj(xs2, w_in_bf, ts, PROJ_TN)
    hs3 = hs2.reshape(bs, SAMPLE_PAD, D_IN)
    pos_s = PAST_LEN + jnp.arange(SAMPLE_PAD)
    ro_s, rs_s = _retention(hs3, pos_s, cache_ret_state.reshape(bs, RET_HEADS, RET_DK, RET_DV), ls,
                            SAMPLE_PAD, 4)
    w_buf = cache_swa_k.shape[2]
    prev_k = cache_swa_k.reshape(bs, w_buf, SWA_KV_HEADS * SWA_DH)
    prev_v = cache_swa_v.reshape(bs, w_buf, SWA_KV_HEADS * SWA_DH)
    so_s = _swa_sample(hs3, prev_k, prev_v, sinks, ls)
    mo_s = _mem_attend_cached(hs3, cache_mem_k.reshape(bs, N_MEM, MEM_HEADS, MEM_DH),
                              cache_mem_v.reshape(bs, N_MEM, MEM_HEADS, MEM_DH))
    hmid_s = _merge(ro_s.reshape(ts, d), so_s.reshape(ts, d), mo_s.reshape(ts, d), hs2, xs2,
                    wb_bf, wo_bf, g1, b1, alpha, 256)
    hmid_s = hmid_s.reshape(bs, SAMPLE_PAD, ROW_SUB, 128)[:, :ls].reshape(bs * ls * ROW_SUB, 128)

    t_all = tp + bs * ls
    assert t_all % (2 * ROUTER_TILE) == 0 and t_all // 2 <= tp
    th = t_all // 2
    groups = (hmid_p[:th * ROW_SUB], jnp.concatenate([hmid_p[th * ROW_SUB:], hmid_s], 0))
    wr_t_bf, bias_col = w_router[l].T.astype(BF), router_bias[l].reshape(N_EXPERTS, 1)
    routed = [_router(ht, wr_t_bf, bias_col, ROUTER_TILE) for ht in groups]
    tables = jax.vmap(functools.partial(_dispatch, t=th))(
        jnp.stack([e.T for e, _ in routed]), jnp.stack([w.T for _, w in routed]),
        jnp.stack([_sorted_keys(e.T) for e, _ in routed]))
    y_a, y_b = (
        _moe_ln2(ht, jax.tree.map(lambda a: a[i], tables), w_gate_e[l], w_up_e[l], w_down_e[l],
                 w_sh_gate[l].astype(BF), w_sh_up[l].astype(BF), w_sh_down[l].astype(BF), g2, b2, alpha)
        for i, ht in enumerate(groups))

    y_p = jnp.concatenate([y_a, y_b[:tp - th]], 0).reshape(bp, lp, d)
    y_s = y_b[tp - th:].reshape(bs, ls, d)
    kv4 = lambda a, n: a.reshape(1, a.shape[0], n, SWA_KV_HEADS, SWA_DH)
    k_p = kv4(hp3[:, lp - WINDOW:, COL_SK:COL_SK + 256], WINDOW)
    v_p = kv4(hp3[:, lp - WINDOW:, COL_SV:COL_SV + 256], WINDOW)
    mem4 = lambda a: a.reshape(1, bp, N_MEM, MEM_HEADS, MEM_DH)
    k_s = kv4(jnp.concatenate([prev_k, hs3[:, :ls, COL_SK:COL_SK + 256]], 1)[:, -w_buf:], w_buf)
    v_s = kv4(jnp.concatenate([prev_v, hs3[:, :ls, COL_SV:COL_SV + 256]], 1)[:, -w_buf:], w_buf)
    return (y_p, y_s, rs_p[None], k_p, v_p, mem4(mk_p), mem4(mv_p), rs_s[None], k_s, v_s)
```

```python
import functools

import jax
import jax.numpy as jnp
from jax import lax
from jax.experimental import pallas as pl
from jax.experimental.pallas import tpu as pltpu

BF = jnp.bfloat16
F32 = jnp.float32

D_MODEL = 1024
RET_HEADS = 4
RET_DK = 128
RET_DV = 256
RET_CHUNK = 128
ROPE_BASE = 10000.0
SWA_HEADS = 16
SWA_KV_HEADS = 4
SWA_GROUP = SWA_HEADS // SWA_KV_HEADS
SWA_DH = 64
WINDOW = 128
SWA_BLOCK = 128
N_MEM = 256
MEM_HEADS = 4
MEM_DH = 256
N_BRANCH = 3
N_EXPERTS = 256
TOP_K = 8
N_GROUPS = 8
GROUP_SIZE = N_EXPERTS // N_GROUPS
TOPK_GROUPS = 4
D_EXPERT = 256
ROUTED_SCALE = 2.5
MOE_BLOCK = 128
ROUTER_TILE = 384
ROW_SUB = D_MODEL // 128
STEP_BLOCKS = 8
LN_EPS = 1e-5
NORM_EPS = 1e-6
NEG = -1e30
PAST_LEN = 16384
SAMPLE_PAD = 8

COL_RQ, COL_RK, COL_RV, COL_RG, COL_SQ, COL_MQ, COL_GL, COL_SK, COL_SV = (
    0, 512, 1024, 2048, 3072, 4096, 5120, 8192, 8448)
D_IN = 8704
PROJ_TN = D_IN // 4

VMEM_LIMIT = 56 * 1024 * 1024


def _params(*sem):
    return pltpu.CompilerParams(dimension_semantics=sem, vmem_limit_bytes=VMEM_LIMIT)


def _bdot(a, b):
    return jnp.dot(a.astype(BF), b.astype(BF), preferred_element_type=F32)


def _bdot_nt(a, b):
    return lax.dot_general(a.astype(BF), b.astype(BF), (((1,), (1,)), ((), ())),
                           preferred_element_type=F32)


def _bdot_tn(a, b):
    return lax.dot_general(a.astype(BF), b.astype(BF), (((0,), (0,)), ((), ())),
                           preferred_element_type=F32)


def _layer_norm(z, g, b):
    zc = z - jnp.mean(z, -1, keepdims=True)
    var = jnp.mean(zc * zc, -1, keepdims=True)
    return zc * lax.rsqrt(var + LN_EPS) * g + b


def _load_tiled_rows(ref, idx, m):
    return jnp.concatenate([ref[(*idx, pl.ds(s, m, stride=ROW_SUB), slice(None))]
                            for s in range(ROW_SUB)], axis=1)


def _store_tiled_rows(ref, idx, val):
    m = val.shape[0]
    for s in range(ROW_SUB):
        ref[(*idx, pl.ds(s, m, stride=ROW_SUB), slice(None))] = val[:, s * 128:(s + 1) * 128]


def _proj_kernel(x_ref, w_ref, o_ref, xb_ref):
    @pl.when(pl.program_id(1) == 0)
    def _():
        xb_ref[...] = x_ref[...].astype(BF)

    o_ref[...] = jnp.dot(xb_ref[...], w_ref[...], preferred_element_type=F32)


def _proj(x, w_bf, tm, tn):
    m, k = x.shape
    n = w_bf.shape[1]
    return pl.pallas_call(
        _proj_kernel,
        grid=(m // tm, n // tn),
        in_specs=[pl.BlockSpec((tm, k), lambda i, j: (i, 0)),
                  pl.BlockSpec((k, tn), lambda i, j: (0, j))],
        out_specs=pl.BlockSpec((tm, tn), lambda i, j: (i, j)),
        out_shape=jax.ShapeDtypeStruct((m, n), F32),
        scratch_shapes=[pltpu.VMEM((tm, k), BF)],
        compiler_params=_params("parallel", "arbitrary"),
        name="proj",
    )(x, w_bf)


def _ret_tables(c_real, c_pad):
    lg = jnp.log1p(-jnp.exp2(-5.0 - jnp.arange(RET_HEADS, dtype=F32)))
    idx = jnp.arange(c_pad, dtype=F32)
    real = idx < c_real
    rel = idx[:, None] - idx[None, :]
    intra = jnp.where(rel >= 0, jnp.exp(lg[:, None, None] * jnp.maximum(rel, 0.0)), 0.0)
    intra = jnp.where(real[None, :, None] & real[None, None, :], intra, 0.0)
    q_dec = jnp.where(real[None, :], jnp.exp((idx[None, :] + 1.0) * lg[:, None]), 0.0)
    k_dec = jnp.where(real[None, :], jnp.exp((c_real - 1.0 - idx)[None, :] * lg[:, None]), 0.0)
    c_dec = jnp.exp(c_real * lg)
    bc = lambda t: jnp.broadcast_to(t[:, :, None], (RET_HEADS, c_pad, RET_DK))
    return intra, bc(q_dec), bc(k_dec), c_dec


def _rope_tables(pos):
    half = RET_DK // 2
    inv_freq = 1.0 / (ROPE_BASE ** (jnp.arange(half, dtype=F32) / half))
    ang = pos.astype(F32)[:, None] * inv_freq[None, :]
    cos, sin = jnp.cos(ang), jnp.sin(ang)
    return jnp.concatenate([cos, cos], -1), jnp.concatenate([-sin, sin], -1)


def _ret_kernel(cdec_ref, rq_ref, rk_ref, rv_ref, rg_ref, cos_ref, sin_ref, intra_ref, qdec_ref,
                kdec_ref, s0_ref, o_ref, s_out_ref, s_scr, *, n_chunks):
    c = pl.program_id(1)

    @pl.when(c == 0)
    def _():
        s_scr[...] = s0_ref[...]

    cos2 = cos_ref[...]
    sin2 = sin_ref[...]

    def rot(x):
        return x * cos2 + pltpu.roll(x, RET_DK // 2, 1) * sin2

    for bi in range(rq_ref.shape[0]):
        for h in range(RET_HEADS):
            q = rot(rq_ref[bi, :, h * RET_DK:(h + 1) * RET_DK])
            k = rot(rk_ref[bi, :, h * RET_DK:(h + 1) * RET_DK]) * (RET_DK ** -0.5)
            v = rv_ref[bi, :, h * RET_DV:(h + 1) * RET_DV].astype(BF)
            s_old = s_scr[bi, h]
            a = _bdot_nt(q, k) * intra_ref[h]
            o = _bdot(a, v) + _bdot(q * qdec_ref[h], s_old)
            s_scr[bi, h] = s_old * cdec_ref[h] + _bdot_tn(k * kdec_ref[h], v)
            o = o * lax.rsqrt(jnp.mean(o * o, -1, keepdims=True) + NORM_EPS)
            g = rg_ref[bi, :, h * RET_DV:(h + 1) * RET_DV]
            o_ref[bi, :, h * RET_DV:(h + 1) * RET_DV] = o * (g * jax.nn.sigmoid(g))

    @pl.when(c == n_chunks - 1)
    def _():
        s_out_ref[...] = s_scr[...]


def _retention(h3, pos, state0, c_real, c_pad, tb):
    b, l, _ = h3.shape
    n_chunks = l // c_pad
    intra, qdec, kdec, cdec = _ret_tables(c_real, c_pad)
    cos2, sin2 = _rope_tables(pos)
    full3 = lambda shape: pl.BlockSpec(shape, lambda i, c: (0, 0, 0))
    return pl.pallas_call(
        functools.partial(_ret_kernel, n_chunks=n_chunks),
        grid=(b // tb, n_chunks),
        in_specs=[
            pl.BlockSpec(memory_space=pltpu.SMEM),
            pl.BlockSpec((tb, c_pad, 512), lambda i, c: (i, c, COL_RQ // 512)),
            pl.BlockSpec((tb, c_pad, 512), lambda i, c: (i, c, COL_RK // 512)),
            pl.BlockSpec((tb, c_pad, 1024), lambda i, c: (i, c, COL_RV // 1024)),
            pl.BlockSpec((tb, c_pad, 1024), lambda i, c: (i, c, COL_RG // 1024)),
            pl.BlockSpec((c_pad, RET_DK), lambda i, c: (c, 0)),
            pl.BlockSpec((c_pad, RET_DK), lambda i, c: (c, 0)),
            full3((RET_HEADS, c_pad, c_pad)),
            full3((RET_HEADS, c_pad, RET_DK)),
            full3((RET_HEADS, c_pad, RET_DK)),
            pl.BlockSpec((tb, RET_HEADS, RET_DK, RET_DV), lambda i, c: (i, 0, 0, 0)),
        ],
        out_specs=[
            pl.BlockSpec((tb, c_pad, 1024), lambda i, c: (i, c, 0)),
            pl.BlockSpec((tb, RET_HEADS, RET_DK, RET_DV), lambda i, c: (i, 0, 0, 0)),
        ],
        out_shape=[jax.ShapeDtypeStruct((b, l, 1024), F32),
                   jax.ShapeDtypeStruct((b, RET_HEADS, RET_DK, RET_DV), F32)],
        scratch_shapes=[pltpu.VMEM((tb, RET_HEADS, RET_DK, RET_DV), F32)],
        compiler_params=_params("parallel", "arbitrary"),
        name="retention",
    )(cdec, h3, h3, h3, h3, cos2, sin2, intra, qdec, kdec, state0)


def _sink_softmax(s, sink):
    m = jnp.maximum(jnp.max(s, -1, keepdims=True), sink)
    p = jnp.exp(s - m)
    return p / (jnp.sum(p, -1, keepdims=True) + jnp.exp(sink - m))


def _swa_prompt_kernel(sinks_ref, q_ref, kp_ref, kc_ref, vp_ref, vc_ref, o_ref):
    n = pl.program_id(0)
    rows = SWA_GROUP * SWA_BLOCK
    qi = lax.broadcasted_iota(jnp.int32, (rows, 2 * SWA_BLOCK), 0) % SWA_BLOCK
    kj = lax.broadcasted_iota(jnp.int32, (rows, 2 * SWA_BLOCK), 1)
    rel = SWA_BLOCK + qi - kj
    valid = (rel >= 0) & (rel <= WINDOW) & ((kj >= SWA_BLOCK) | (n > 0))
    row_head = lax.broadcasted_iota(jnp.int32, (rows, 1), 0) // SWA_BLOCK
    for h in range(SWA_KV_HEADS):
        sl = slice(h * SWA_DH, (h + 1) * SWA_DH)
        k2 = jnp.concatenate([kp_ref[:, sl], kc_ref[:, sl]], 0).astype(BF)
        v2 = jnp.concatenate([vp_ref[:, sl], vc_ref[:, sl]], 0).astype(BF)
        heads = [h * SWA_GROUP + g for g in range(SWA_GROUP)]
        q = jnp.concatenate([q_ref[:, hq * SWA_DH:(hq + 1) * SWA_DH] for hq in heads], axis=0)
        sink = jnp.zeros((rows, 1), F32)
        for g, hq in enumerate(heads):
            sink = jnp.where(row_head == g, sinks_ref[hq], sink)
        s = _bdot_nt(q, k2) * (SWA_DH ** -0.5)
        s = jnp.where(valid, s, NEG)
        o = _bdot(_sink_softmax(s, sink), v2)
        for g, hq in enumerate(heads):
            o_ref[:, hq * SWA_DH:(hq + 1) * SWA_DH] = o[g * SWA_BLOCK:(g + 1) * SWA_BLOCK, :]


def _swa_prompt(h2, sinks):
    t = h2.shape[0]
    nb = t // SWA_BLOCK
    prev = lambda col: (lambda n: (jnp.maximum(n - 1, 0), col))
    cur = lambda col: (lambda n: (n, col))
    ck, cv = COL_SK // 256, COL_SV // 256
    return pl.pallas_call(
        _swa_prompt_kernel,
        grid=(nb,),
        in_specs=[
            pl.BlockSpec(memory_space=pltpu.SMEM),
            pl.BlockSpec((SWA_BLOCK, 1024), cur(COL_SQ // 1024)),
            pl.BlockSpec((SWA_BLOCK, 256), prev(ck)),
            pl.BlockSpec((SWA_BLOCK, 256), cur(ck)),
            pl.BlockSpec((SWA_BLOCK, 256), prev(cv)),
            pl.BlockSpec((SWA_BLOCK, 256), cur(cv)),
        ],
        out_specs=pl.BlockSpec((SWA_BLOCK, 1024), lambda n: (n, 0)),
        out_shape=jax.ShapeDtypeStruct((t, 1024), F32),
        compiler_params=_params("parallel"),
        name="swa_prompt",
    )(sinks, h2, h2, h2, h2, h2)


def _swa_sample_kernel(sinks_ref, q_ref, kn_ref, vn_ref, kp_ref, vp_ref, o_ref, *, n_new):
    tb = q_ref.shape[0]
    w = kp_ref.shape[1]
    p_ = SAMPLE_PAD
    rows = SWA_GROUP * p_
    qi = lax.broadcasted_iota(jnp.int32, (tb, rows, w), 1) % p_
    kj = lax.broadcasted_iota(jnp.int32, (tb, rows, w), 2)
    rel_prev = w + qi - kj
    valid_prev = (rel_prev >= 0) & (rel_prev <= WINDOW)
    qn = lax.broadcasted_iota(jnp.int32, (tb, rows, p_), 1) % p_
    kn = lax.broadcasted_iota(jnp.int32, (tb, rows, p_), 2)
    valid_new = (qn - kn >= 0) & (qn - kn <= WINDOW) & (kn < n_new)
    row_head = lax.broadcasted_iota(jnp.int32, (1, rows, 1), 1) // p_
    bdot = lambda eq, a, b: jnp.einsum(eq, a.astype(BF), b.astype(BF), preferred_element_type=F32)
    for h in range(SWA_KV_HEADS):
        sl = slice(h * SWA_DH, (h + 1) * SWA_DH)
        kp, vp = kp_ref[:, :, sl], vp_ref[:, :, sl]
        kn_h, vn_h = kn_ref[:, :, sl], vn_ref[:, :, sl]
        heads = [h * SWA_GROUP + g for g in range(SWA_GROUP)]
        q = jnp.concatenate([q_ref[:, :, hq * SWA_DH:(hq + 1) * SWA_DH] for hq in heads], axis=1)
        sink = jnp.zeros((1, rows, 1), F32)
        for g, hq in enumerate(heads):
            sink = jnp.where(row_head == g, sinks_ref[hq], sink)
        sp = bdot('bqd,bkd->bqk', q, kp) * (SWA_DH ** -0.5)
        sn = bdot('bqd,bkd->bqk', q, kn_h) * (SWA_DH ** -0.5)
        sp = jnp.where(valid_prev, sp, NEG)
        sn = jnp.where(valid_new, sn, NEG)
        m = jnp.maximum(jnp.maximum(jnp.max(sp, -1, keepdims=True),
                                    jnp.max(sn, -1, keepdims=True)), sink)
        pp = jnp.exp(sp - m)
        pn = jnp.exp(sn - m)
        den = jnp.sum(pp, -1, keepdims=True) + jnp.sum(pn, -1, keepdims=True) + jnp.exp(sink - m)
        o = bdot('bqk,bkd->bqd', pp / den, vp) + bdot('bqk,bkd->bqd', pn / den, vn_h)
        for g, hq in enumerate(heads):
            o_ref[:, :, hq * SWA_DH:(hq + 1) * SWA_DH] = o[:, g * p_:(g + 1) * p_, :]


def _swa_sample(h3, prev_k, prev_v, sinks, n_new, tb=8):
    b = h3.shape[0]
    w = prev_k.shape[1]
    return pl.pallas_call(
        functools.partial(_swa_sample_kernel, n_new=n_new),
        grid=(b // tb,),
        in_specs=[
            pl.BlockSpec(memory_space=pltpu.SMEM),
            pl.BlockSpec((tb, SAMPLE_PAD, 1024), lambda i: (i, 0, COL_SQ // 1024)),
            pl.BlockSpec((tb, SAMPLE_PAD, 256), lambda i: (i, 0, COL_SK // 256)),
            pl.BlockSpec((tb, SAMPLE_PAD, 256), lambda i: (i, 0, COL_SV // 256)),
            pl.BlockSpec((tb, w, 256), lambda i: (i, 0, 0)),
            pl.BlockSpec((tb, w, 256), lambda i: (i, 0, 0)),
        ],
        out_specs=pl.BlockSpec((tb, SAMPLE_PAD, 1024), lambda i: (i, 0, 0)),
        out_shape=jax.ShapeDtypeStruct((b, SAMPLE_PAD, 1024), F32),
        compiler_params=_params("parallel"),
        name="swa_sample",
    )(sinks, h3, h3, h3, prev_k, prev_v)


def _mem_head(q, k, v):
    s = _bdot_nt(q, k) * (MEM_DH ** -0.5)
    m = jnp.max(s, -1, keepdims=True)
    e = jnp.exp(s - m)
    return _bdot(e / jnp.sum(e, -1, keepdims=True), v)


def _mem_kernel(q_ref, mk_ref, mv_ref, o_ref):
    for h in range(MEM_HEADS):
        sl = slice(h * MEM_DH, (h + 1) * MEM_DH)
        o_ref[0, :, sl] = _mem_head(q_ref[0, :, sl], mk_ref[0, :, sl], mv_ref[0, :, sl])


def _mem_cached_kernel(q_ref, mk_hbm, mv_hbm, o_ref, kv_ref, sem_ref):
    b = pl.program_id(0)
    slot = b % 2

    def copies(seq, s):
        return [pltpu.make_async_copy(src.at[seq, :, h, :], kv_ref.at[s, j, h], sem_ref.at[s, j, h])
                for j, src in enumerate((mk_hbm, mv_hbm)) for h in range(MEM_HEADS)]

    @pl.when(b == 0)
    def _():
        for c in copies(0, 0):
            c.start()

    @pl.when(b + 1 < pl.num_programs(0))
    def _():
        for c in copies(b + 1, 1 - slot):
            c.start()

    for c in copies(b, slot):
        c.wait()
    for h in range(MEM_HEADS):
        sl = slice(h * MEM_DH, (h + 1) * MEM_DH)
        o_ref[0, :, sl] = _mem_head(q_ref[0, :, sl], kv_ref[slot, 0, h], kv_ref[slot, 1, h])


def _mem_attend_cached(h3, mk, mv):
    b, l, _ = h3.shape
    return pl.pallas_call(
        _mem_cached_kernel,
        grid=(b,),
        in_specs=[
            pl.BlockSpec((1, l, 1024), lambda i: (i, 0, COL_MQ // 1024)),
            pl.BlockSpec(memory_space=pl.ANY),
            pl.BlockSpec(memory_space=pl.ANY),
        ],
        out_specs=pl.BlockSpec((1, l, 1024), lambda i: (i, 0, 0)),
        out_shape=jax.ShapeDtypeStruct((b, l, 1024), F32),
        scratch_shapes=[pltpu.VMEM((2, 2, MEM_HEADS, N_MEM, MEM_DH), F32),
                        pltpu.SemaphoreType.DMA((2, 2, MEM_HEADS))],
        compiler_params=_params("arbitrary"),
        name="mem_attend_cached",
    )(h3, mk, mv)


def _mem_attend(h3, mk, mv, tl):
    b, l, _ = h3.shape
    return pl.pallas_call(
        _mem_kernel,
        grid=(b, l // tl),
        in_specs=[
            pl.BlockSpec((1, tl, 1024), lambda i, j: (i, j, COL_MQ // 1024)),
            pl.BlockSpec((1, N_MEM, 1024), lambda i, j: (i, 0, 0)),
            pl.BlockSpec((1, N_MEM, 1024), lambda i, j: (i, 0, 0)),
        ],
        out_specs=pl.BlockSpec((1, tl, 1024), lambda i, j: (i, j, 0)),
        out_shape=jax.ShapeDtypeStruct((b, l, 1024), F32),
        compiler_params=_params("parallel", "parallel"),
        name="mem_attend",
    )(h3, mk, mv)


def _merge_kernel(ro_ref, so_ref, mo_ref, g0_ref, g1_ref, g2_ref, x_ref, wb_ref, wo_ref, g_ref,
                  b_ref, o_ref, *, alpha):
    acc = None
    for n, (br, gl) in enumerate(((ro_ref, g0_ref), (so_ref, g1_ref), (mo_ref, g2_ref))):
        term = jax.nn.sigmoid(gl[...]) * jnp.dot(br[...].astype(BF), wb_ref[n],
                                                 preferred_element_type=F32)
        acc = term if acc is None else acc + term
    a = jnp.dot(acc.astype(BF), wo_ref[...], preferred_element_type=F32)
    _store_tiled_rows(o_ref, (), _layer_norm(alpha * x_ref[...] + a, g_ref[...], b_ref[...]))


def _merge(ro, so, mo, h2, x2, wb_bf, wo_bf, g, b, alpha, tm):
    t = x2.shape[0]
    tile = lambda col: pl.BlockSpec((tm, 1024), lambda i: (i, col))
    gl0 = COL_GL // 1024
    return pl.pallas_call(
        functools.partial(_merge_kernel, alpha=alpha),
        grid=(t // tm,),
        in_specs=[tile(0), tile(0), tile(0), tile(gl0), tile(gl0 + 1), tile(gl0 + 2), tile(0),
                  pl.BlockSpec((N_BRANCH, 1024, 1024), lambda i: (0, 0, 0)),
                  pl.BlockSpec((1024, 1024), lambda i: (0, 0)),
                  pl.BlockSpec((1, 1024), lambda i: (0, 0)),
                  pl.BlockSpec((1, 1024), lambda i: (0, 0))],
        out_specs=pl.BlockSpec((tm * ROW_SUB, 128), lambda i: (i, 0)),
        out_shape=jax.ShapeDtypeStruct((t * ROW_SUB, 128), F32),
        compiler_params=_params("parallel"),
        name="merge_ln1",
    )(ro, so, mo, h2, h2, h2, x2, wb_bf, wo_bf, g, b)


def _first_index_of_max(v, iota, big, axes):
    m = jnp.max(v, axis=axes, keepdims=True)
    idx = jnp.min(jnp.where(v == m, iota, big), axis=axes, keepdims=True)
    return m, idx


def _router_kernel(x_ref, wr_ref, bias_ref, eidx_ref, ew_ref):
    tt = x_ref.shape[0] // ROW_SUB
    x = _load_tiled_rows(x_ref, (), tt).astype(BF)
    logits = lax.dot_general(wr_ref[...], x, (((1,), (1,)), ((), ())),
                             preferred_element_type=F32)
    s = jax.nn.sigmoid(logits).reshape(N_GROUPS, GROUP_SIZE, tt)
    sb = s + bias_ref[...].reshape(N_GROUPS, GROUP_SIZE, 1)
    ninf = -jnp.inf
    r_iota = lax.broadcasted_iota(jnp.int32, sb.shape, 1)
    m1, i1 = _first_index_of_max(sb, r_iota, GROUP_SIZE, 1)
    m2 = jnp.max(jnp.where(r_iota == i1, ninf, sb), axis=1, keepdims=True)
    gsc = (m1 + m2).reshape(N_GROUPS, tt)
    g_iota = lax.broadcasted_iota(jnp.int32, gsc.shape, 0)
    gmask = jnp.zeros(gsc.shape, jnp.bool_)
    for _ in range(TOPK_GROUPS):
        _, gi = _first_index_of_max(gsc, g_iota, N_GROUPS, 0)
        hit = g_iota == gi
        gmask = gmask | hit
        gsc = jnp.where(hit, ninf, gsc)
    cand = jnp.where(gmask.reshape(N_GROUPS, 1, tt), sb, ninf)
    e_iota = lax.broadcasted_iota(jnp.int32, sb.shape, 0) * GROUP_SIZE + r_iota
    idxs, ws = [], []
    for _ in range(TOP_K):
        _, ei = _first_index_of_max(cand, e_iota, N_EXPERTS, (0, 1))
        hit = e_iota == ei
        idxs.append(ei.reshape(1, tt))
        ws.append(jnp.sum(jnp.where(hit, s, 0.0), axis=(0, 1)).reshape(1, tt))
        cand = jnp.where(hit, ninf, cand)
    w = jnp.concatenate(ws, 0)
    eidx_ref[...] = jnp.concatenate(idxs, 0)
    ew_ref[...] = w / jnp.sum(w, 0, keepdims=True) * ROUTED_SCALE


def _router(xt, wr_t_bf, bias_col, tt):
    t = xt.shape[0] // ROW_SUB
    return pl.pallas_call(
        _router_kernel,
        grid=(t // tt,),
        in_specs=[pl.BlockSpec((tt * ROW_SUB, 128), lambda i: (i, 0)),
                  pl.BlockSpec((N_EXPERTS, 1024), lambda i: (0, 0)),
                  pl.BlockSpec((N_EXPERTS, 1), lambda i: (0, 0))],
        out_specs=[pl.BlockSpec((TOP_K, tt), lambda i: (0, i)),
                   pl.BlockSpec((TOP_K, tt), lambda i: (0, i))],
        out_shape=[jax.ShapeDtypeStruct((TOP_K, t), jnp.int32),
                   jax.ShapeDtypeStruct((TOP_K, t), F32)],
        compiler_params=_params("parallel"),
        name="router",
    )(xt, wr_t_bf, bias_col)


IDX_BITS = 18


def _sorted_keys(eidx):
    n_grp = eidx.shape[0]
    flat_e = eidx.reshape(n_grp, -1)
    a = flat_e.shape[1]
    assert a < (1 << IDX_BITS) and n_grp * N_EXPERTS << IDX_BITS < 2 ** 31
    grp_exp = jnp.arange(n_grp, dtype=jnp.int32)[:, None] * N_EXPERTS + flat_e
    keys = grp_exp * (1 << IDX_BITS) + jnp.arange(a, dtype=jnp.int32)[None, :]
    return jnp.sort(keys.reshape(-1)).reshape(n_grp, a) & ((N_EXPERTS << IDX_BITS) - 1)


def _dispatch(eidx, ew, skey, t):
    a = t * TOP_K
    nblk = -(-a // MOE_BLOCK) + N_EXPERTS
    assert nblk % STEP_BLOCKS == 0 and STEP_BLOCKS % 2 == 0
    flat_e = eidx.reshape(-1)
    si = skey & ((1 << IDX_BITS) - 1)
    experts = jnp.arange(N_EXPERTS, dtype=jnp.int32)
    counts = jnp.sum((flat_e[None, :] == experts[:, None]).astype(jnp.int32), axis=1)
    grp_start = jnp.cumsum(counts) - counts
    padded = (counts + MOE_BLOCK - 1) // MOE_BLOCK * MOE_BLOCK
    pad_end = jnp.cumsum(padded)
    pad_start = pad_end - padded
    blk_first = jnp.arange(nblk, dtype=jnp.int32) * MOE_BLOCK
    blk_exp = jnp.minimum(jnp.sum((pad_end[None, :] <= blk_first[:, None]).astype(jnp.int32), axis=1),
                          N_EXPERTS - 1)
    off = (jnp.arange(nblk * MOE_BLOCK, dtype=jnp.int32).reshape(nblk, MOE_BLOCK)
           - pad_start[blk_exp][:, None])
    valid = off < counts[blk_exp][:, None]
    src = jnp.clip(grp_start[blk_exp][:, None] + off, 0, a - 1)
    row_si = si[src]
    row_tok = jnp.where(valid, row_si >> 3, t).astype(jnp.int32)
    row_w = jnp.where(valid, ew.reshape(-1)[row_si], 0.0)
    nused = (pad_end[-1] // MOE_BLOCK).astype(jnp.int32).reshape(1)
    gidx, nxt = _group_tables(counts, blk_exp)
    step_rows = STEP_BLOCKS * MOE_BLOCK
    return (row_tok.reshape(-1), row_w.reshape(nblk // STEP_BLOCKS, 1, step_rows), blk_exp, nused,
            gidx, nxt)


def _group_tables(counts, blk_exp):
    nonempty = counts > 0
    gidx = (jnp.cumsum(nonempty.astype(jnp.int32)) - 1)[blk_exp]
    experts = jnp.arange(N_EXPERTS, dtype=jnp.int32)
    cand = jnp.where(nonempty, experts, N_EXPERTS)
    later = lax.cummin(cand, axis=0, reverse=True)
    nxt = jnp.concatenate([later[1:], jnp.full((2,), N_EXPERTS, jnp.int32)])
    nxt1 = nxt[blk_exp]
    nxt2 = nxt[jnp.minimum(nxt1, N_EXPERTS)]
    none = lambda a: jnp.where(a >= N_EXPERTS, -1, a).astype(jnp.int32)
    return gidx.astype(jnp.int32), jnp.concatenate([none(nxt1), none(nxt2)])


WEIGHT_SLOTS = 3


def _weight_copies(hbm_refs, buf_ref, sem_ref, e, slot):
    return [pltpu.make_async_copy(h.at[e], buf_ref.at[slot, k], sem_ref.at[slot, k])
            for k, h in enumerate(hbm_refs)]


def _stage_weights(b, be_ref, nused_ref, gidx_ref, nxt_ref, hbm_refs, buf_ref, sem_ref, cache_refs):
    first = ((b == 0) | (be_ref[b] != be_ref[jnp.maximum(b - 1, 0)])) & (b < nused_ref[0])
    nblk = be_ref.shape[0]

    def start(e, slot):
        for c in _weight_copies(hbm_refs, buf_ref, sem_ref, e, slot):
            c.start()

    @pl.when(first)
    def _():
        slot = gidx_ref[b] % WEIGHT_SLOTS

        @pl.when(b == 0)
        def _():
            start(be_ref[0], 0)

            @pl.when(nxt_ref[0] >= 0)
            def _():
                start(nxt_ref[0], 1)

        for c in _weight_copies(hbm_refs, buf_ref, sem_ref, be_ref[b], slot):
            c.wait()
        nxt2 = nxt_ref[nblk + b]

        @pl.when(nxt2 >= 0)
        def _():
            start(nxt2, (slot + 2) % WEIGHT_SLOTS)

        for k, cache in enumerate(cache_refs):
            cache[...] = buf_ref[slot, k].astype(BF)


def _gather_rows(tok_ref, b, x_ref, xg_ref, slot):
    for r in range(MOE_BLOCK):
        xg_ref[slot, r * ROW_SUB:(r + 1) * ROW_SUB, :] = x_ref[tok_ref[b * MOE_BLOCK + r]]


def _up_kernel(be_ref, nused_ref, gidx_ref, nxt_ref, tok_ref, x_ref, wg_hbm, wu_hbm, act_ref,
               xg_ref, wbuf_ref, wgb_ref, wub_ref, sem_ref):
    i = pl.program_id(0)
    nblk = pl.num_programs(0) * STEP_BLOCKS
    b0 = i * STEP_BLOCKS

    @pl.when(i == 0)
    def _():
        _gather_rows(tok_ref, 0, x_ref, xg_ref, 0)

    @pl.when(b0 < nused_ref[0])
    def _():
        for j in range(STEP_BLOCKS):
            b = b0 + j
            _stage_weights(b, be_ref, nused_ref, gidx_ref, nxt_ref, (wg_hbm, wu_hbm), wbuf_ref,
                           sem_ref, (wgb_ref, wub_ref))
            _gather_rows(tok_ref, jnp.minimum(b + 1, nblk - 1), x_ref, xg_ref, (j + 1) % 2)
            x = _load_tiled_rows(xg_ref, (j % 2,), MOE_BLOCK).astype(BF)
            g = jnp.dot(x, wgb_ref[...], preferred_element_type=F32)
            u = jnp.dot(x, wub_ref[...], preferred_element_type=F32)
            act_ref[j * MOE_BLOCK:(j + 1) * MOE_BLOCK, :] = ((g * jax.nn.sigmoid(g)) * u).astype(BF)

    @pl.when(b0 >= nused_ref[0])
    def _():
        act_ref[...] = jnp.zeros_like(act_ref)


def _experts_up(x, row_tok, blk_exp, nused, gidx, nxt, w_gate, w_up):
    nblk = blk_exp.shape[0]
    step_rows = STEP_BLOCKS * MOE_BLOCK
    return pl.pallas_call(
        _up_kernel,
        grid_spec=pltpu.PrefetchScalarGridSpec(
            num_scalar_prefetch=5,
            grid=(nblk // STEP_BLOCKS,),
            in_specs=[pl.BlockSpec(memory_space=pltpu.VMEM),
                      pl.BlockSpec(memory_space=pl.ANY),
                      pl.BlockSpec(memory_space=pl.ANY)],
            out_specs=pl.BlockSpec((step_rows, D_EXPERT), lambda i, *_: (i, 0)),
            scratch_shapes=[pltpu.VMEM((2, MOE_BLOCK * ROW_SUB, 128), F32),
                            pltpu.VMEM((WEIGHT_SLOTS, 2, D_MODEL, D_EXPERT), F32),
                            pltpu.VMEM((D_MODEL, D_EXPERT), BF),
                            pltpu.VMEM((D_MODEL, D_EXPERT), BF),
                            pltpu.SemaphoreType.DMA((WEIGHT_SLOTS, 2))],
        ),
        out_shape=jax.ShapeDtypeStruct((nblk * MOE_BLOCK, D_EXPERT), BF),
        compiler_params=_params("arbitrary"),
        name="experts_up",
    )(blk_exp, nused, gidx, nxt, row_tok, x, w_gate, w_up)


SCATTER_GROUP = 8


def _row_to_column(row):
    n = row.shape[1]
    eye = lax.broadcasted_iota(jnp.int32, (n, n), 0) == lax.broadcasted_iota(jnp.int32, (n, n), 1)
    return jnp.sum(jnp.where(eye, jnp.broadcast_to(row, (n, n)), 0.0), axis=1, keepdims=True)


def _scatter_add_rows(tok_ref, b, y_ref, yb_ref, slot):
    for r0 in range(0, MOE_BLOCK, SCATTER_GROUP):
        rs = range(r0, r0 + SCATTER_GROUP)
        toks = [tok_ref[b * MOE_BLOCK + r] for r in rs]
        new = [y_ref[t] + yb_ref[slot, r * ROW_SUB:(r + 1) * ROW_SUB, :] for r, t in zip(rs, toks)]
        for t, v in zip(toks, new):
            y_ref[t] = v


def _down_kernel(be_ref, nused_ref, gidx_ref, nxt_ref, tok_ref, act_ref, rw_ref, wd_hbm, y_ref,
                 yb_ref, wbuf_ref, wdb_ref, sem_ref):
    i = pl.program_id(0)
    nblk = pl.num_programs(0) * STEP_BLOCKS
    b0 = i * STEP_BLOCKS

    @pl.when(i == 0)
    def _():
        y_ref[...] = jnp.zeros_like(y_ref)
        yb_ref[1] = jnp.zeros(yb_ref.shape[1:], F32)

    @pl.when(b0 <= nused_ref[0])
    def _():
        for j in range(STEP_BLOCKS):
            b = b0 + j
            rows = slice(j * MOE_BLOCK, (j + 1) * MOE_BLOCK)
            _stage_weights(b, be_ref, nused_ref, gidx_ref, nxt_ref, (wd_hbm,), wbuf_ref, sem_ref,
                           (wdb_ref,))
            yb = jnp.dot(act_ref[rows, :], wdb_ref[...],
                         preferred_element_type=F32) * _row_to_column(rw_ref[0, :, rows])
            _store_tiled_rows(yb_ref, (j % 2,), yb)
            _scatter_add_rows(tok_ref, jnp.maximum(b - 1, 0), y_ref, yb_ref, (j + 1) % 2)

    @pl.when((i == pl.num_programs(0) - 1) & (nused_ref[0] >= nblk))
    def _():
        _scatter_add_rows(tok_ref, nblk - 1, y_ref, yb_ref, (STEP_BLOCKS - 1) % 2)


def _experts_down(act, row_tok, row_w, blk_exp, nused, gidx, nxt, w_down, t_rows):
    nblk = blk_exp.shape[0]
    step_rows = STEP_BLOCKS * MOE_BLOCK
    return pl.pallas_call(
        _down_kernel,
        grid_spec=pltpu.PrefetchScalarGridSpec(
            num_scalar_prefetch=5,
            grid=(nblk // STEP_BLOCKS,),
            in_specs=[pl.BlockSpec((step_rows, D_EXPERT), lambda i, *_: (i, 0)),
                      pl.BlockSpec((1, 1, step_rows), lambda i, *_: (i, 0, 0)),
                      pl.BlockSpec(memory_space=pl.ANY)],
            out_specs=pl.BlockSpec(memory_space=pltpu.VMEM),
            scratch_shapes=[pltpu.VMEM((2, MOE_BLOCK * ROW_SUB, 128), F32),
                            pltpu.VMEM((WEIGHT_SLOTS, 1, D_EXPERT, D_MODEL), F32),
                            pltpu.VMEM((D_EXPERT, D_MODEL), BF),
                            pltpu.SemaphoreType.DMA((WEIGHT_SLOTS, 1))],
        ),
        out_shape=jax.ShapeDtypeStruct((t_rows, ROW_SUB, 128), F32),
        compiler_params=_params("arbitrary"),
        name="experts_down",
    )(blk_exp, nused, gidx, nxt, row_tok, act, row_w, w_down)


def _final_kernel(h_ref, yr_ref, wg_ref, wu_ref, wd_ref, g_ref, b_ref, o_ref, *, alpha):
    h = _load_tiled_rows(h_ref, (), o_ref.shape[0])
    hb = h.astype(BF)
    g = jnp.dot(hb, wg_ref[...], preferred_element_type=F32)
    u = jnp.dot(hb, wu_ref[...], preferred_element_type=F32)
    shared = jnp.dot(((g * jax.nn.sigmoid(g)) * u).astype(BF), wd_ref[...],
                     preferred_element_type=F32)
    f = _load_tiled_rows(yr_ref, (), h.shape[0]) + shared
    o_ref[...] = _layer_norm(alpha * h + f, g_ref[...], b_ref[...])


def _final(ht, yr, wg_bf, wu_bf, wd_bf, g, b, alpha, tm):
    t = ht.shape[0] // ROW_SUB
    tiled = pl.BlockSpec((tm * ROW_SUB, 128), lambda i: (i, 0))
    full = lambda shape: pl.BlockSpec(shape, lambda i: (0, 0))
    return pl.pallas_call(
        functools.partial(_final_kernel, alpha=alpha),
        grid=(t // tm,),
        in_specs=[tiled, tiled, full((1024, 256)), full((1024, 256)), full((256, 1024)),
                  full((1, 1024)), full((1, 1024))],
        out_specs=pl.BlockSpec((tm, 1024), lambda i: (i, 0)),
        out_shape=jax.ShapeDtypeStruct((t, 1024), F32),
        compiler_params=_params("parallel"),
        name="shared_ln2",
    )(ht, yr, wg_bf, wu_bf, wd_bf, g, b)


def _moe_ln2(ht, tables, w_gate, w_up, w_down, wsg_bf, wsu_bf, wsd_bf, g, b, alpha):
    t = ht.shape[0] // ROW_SUB
    row_tok, row_w, blk_exp, nused, gidx, nxt = tables
    act = _experts_up(ht.reshape(t, ROW_SUB, 128), jnp.minimum(row_tok, t - 1), blk_exp, nused, gidx,
                      nxt, w_gate, w_up)
    yr = _experts_down(act, row_tok, row_w, blk_exp, nused, gidx, nxt, w_down, t + 8)
    return _final(ht, yr.reshape((t + 8) * ROW_SUB, 128), wsg_bf, wsu_bf, wsd_bf, g, b, alpha, 256)


def _permute_w_in(w_in):
    rq, rk, rv, rg, sq, sk, sv, mq, gl = jnp.split(
        w_in, [512, 1024, 2048, 3072, 4096, 4352, 4608, 5632], axis=-1)
    return jnp.concatenate([rq, rk, rv, rg, sq, mq, gl, sk, sv], -1)


def kernel(x_prompt, x_sample, mem_prompt, cache_ret_state, cache_swa_k, cache_swa_v, cache_mem_k,
           cache_mem_v, w_in, swa_sinks, w_mem_kv, w_branch, w_o, ln1_g, ln1_b, w_router,
           router_bias, w_gate_e, w_up_e, w_down_e, w_sh_gate, w_sh_up, w_sh_down, ln2_g, ln2_b):
    depth = w_in.shape[0]
    assert depth == 1
    alpha = (2.0 * depth) ** 0.25
    bp, lp, d = x_prompt.shape
    bs, ls, _ = x_sample.shape
    l = 0

    w_in_bf = _permute_w_in(w_in[l]).astype(BF)
    sinks = swa_sinks[l]
    wb_bf = w_branch[l].astype(BF)
    wo_bf = w_o[l].astype(BF)
    g1, b1 = ln1_g[l].reshape(1, d), ln1_b[l].reshape(1, d)
    g2, b2 = ln2_g[l].reshape(1, d), ln2_b[l].reshape(1, d)

    tp = bp * lp
    xp2 = x_prompt.reshape(tp, d)
    hp2 = _proj(xp2, w_in_bf, 1024, PROJ_TN)
    hp3 = hp2.reshape(bp, lp, D_IN)
    mkv = _proj(mem_prompt.reshape(bp * N_MEM, d), w_mem_kv[l].astype(BF), N_MEM, 512)
    mk_p, mv_p = mkv[:, :1024].reshape(bp, N_MEM, 1024), mkv[:, 1024:].reshape(bp, N_MEM, 1024)
    rs0 = jnp.zeros((bp, RET_HEADS, RET_DK, RET_DV), F32)
    ro_p, rs_p = _retention(hp3, jnp.arange(lp), rs0, RET_CHUNK, RET_CHUNK, 1)
    so_p = _swa_prompt(hp2, sinks)
    mo_p = _mem_attend(hp3, mk_p, mv_p, 256)
    hmid_p = _merge(ro_p.reshape(tp, d), so_p, mo_p.reshape(tp, d), hp2, xp2, wb_bf, wo_bf,
                    g1, b1, alpha, 256)

    ts = bs * SAMPLE_PAD
    xs3 = jnp.pad(x_sample, ((0, 0), (0, SAMPLE_PAD - ls), (0, 0)))
    xs2 = xs3.reshape(ts, d)
    hs2 = _proj(xs2, w_in_bf, ts, PROJ_TN)
    hs3 = hs2.reshape(bs, SAMPLE_PAD, D_IN)
    pos_s = PAST_LEN + jnp.arange(SAMPLE_PAD)
    ro_s, rs_s = _retention(hs3, pos_s, cache_ret_state.reshape(bs, RET_HEADS, RET_DK, RET_DV), ls,
                            SAMPLE_PAD, 4)
    w_buf = cache_swa_k.shape[2]
    prev_k = cache_swa_k.reshape(bs, w_buf, SWA_KV_HEADS * SWA_DH)
    prev_v = cache_swa_v.reshape(bs, w_buf, SWA_KV_HEADS * SWA_DH)
    so_s = _swa_sample(hs3, prev_k, prev_v, sinks, ls)
    mo_s = _mem_attend_cached(hs3, cache_mem_k.reshape(bs, N_MEM, MEM_HEADS, MEM_DH),
                              cache_mem_v.reshape(bs, N_MEM, MEM_HEADS, MEM_DH))
    hmid_s = _merge(ro_s.reshape(ts, d), so_s.reshape(ts, d), mo_s.reshape(ts, d), hs2, xs2,
                    wb_bf, wo_bf, g1, b1, alpha, 256)
    hmid_s = hmid_s.reshape(bs, SAMPLE_PAD, ROW_SUB, 128)[:, :ls].reshape(bs * ls * ROW_SUB, 128)

    t_all = tp + bs * ls
    assert t_all % (2 * ROUTER_TILE) == 0 and t_all // 2 <= tp
    th = t_all // 2
    groups = (hmid_p[:th * ROW_SUB], jnp.concatenate([hmid_p[th * ROW_SUB:], hmid_s], 0))
    wr_t_bf, bias_col = w_router[l].T.astype(BF), router_bias[l].reshape(N_EXPERTS, 1)
    routed = [_router(ht, wr_t_bf, bias_col, ROUTER_TILE) for ht in groups]
    eidx = jnp.stack([e.T for e, _ in routed])
    tables = jax.vmap(functools.partial(_dispatch, t=th))(
        eidx, jnp.stack([w.T for _, w in routed]), _sorted_keys(eidx))
    y_a, y_b = (
        _moe_ln2(ht, jax.tree.map(lambda a: a[i], tables), w_gate_e[l], w_up_e[l], w_down_e[l],
                 w_sh_gate[l].astype(BF), w_sh_up[l].astype(BF), w_sh_down[l].astype(BF), g2, b2, alpha)
        for i, ht in enumerate(groups))

    y_p = jnp.concatenate([y_a, y_b[:tp - th]], 0).reshape(bp, lp, d)
    y_s = y_b[tp - th:].reshape(bs, ls, d)
    kv4 = lambda a, n: a.reshape(1, a.shape[0], n, SWA_KV_HEADS, SWA_DH)
    k_p = kv4(hp3[:, lp - WINDOW:, COL_SK:COL_SK + 256], WINDOW)
    v_p = kv4(hp3[:, lp - WINDOW:, COL_SV:COL_SV + 256], WINDOW)
    mem4 = lambda a: a.reshape(1, bp, N_MEM, MEM_HEADS, MEM_DH)
    k_s = kv4(jnp.concatenate([prev_k, hs3[:, :ls, COL_SK:COL_SK + 256]], 1)[:, -w_buf:], w_buf)
    v_s = kv4(jnp.concatenate([prev_v, hs3[:, :ls, COL_SV:COL_SV + 256]], 1)[:, -w_buf:], w_buf)
    return (y_p, y_s, rs_p[None], k_p, v_p, mem4(mk_p), mem4(mv_p), rs_s[None], k_s, v_s)
```

```python
import functools

import jax
import jax.numpy as jnp
from jax import lax
from jax.experimental import pallas as pl
from jax.experimental.pallas import tpu as pltpu

BF = jnp.bfloat16
F32 = jnp.float32

D_MODEL = 1024
RET_HEADS = 4
RET_DK = 128
RET_DV = 256
RET_CHUNK = 128
ROPE_BASE = 10000.0
SWA_HEADS = 16
SWA_KV_HEADS = 4
SWA_GROUP = SWA_HEADS // SWA_KV_HEADS
SWA_DH = 64
WINDOW = 128
SWA_BLOCK = 128
N_MEM = 256
MEM_HEADS = 4
MEM_DH = 256
N_BRANCH = 3
N_EXPERTS = 256
TOP_K = 8
N_GROUPS = 8
GROUP_SIZE = N_EXPERTS // N_GROUPS
TOPK_GROUPS = 4
D_EXPERT = 256
ROUTED_SCALE = 2.5
MOE_BLOCK = 128
ROUTER_TILE = 384
ROW_SUB = D_MODEL // 128
STEP_BLOCKS = 8
LN_EPS = 1e-5
NORM_EPS = 1e-6
NEG = -1e30
PAST_LEN = 16384
SAMPLE_PAD = 8

COL_RQ, COL_RK, COL_RV, COL_RG, COL_SQ, COL_MQ, COL_GL, COL_SK, COL_SV = (
    0, 512, 1024, 2048, 3072, 4096, 5120, 8192, 8448)
D_IN = 8704
PROJ_TN = D_IN // 4

VMEM_LIMIT = 56 * 1024 * 1024


def _params(*sem):
    return pltpu.CompilerParams(dimension_semantics=sem, vmem_limit_bytes=VMEM_LIMIT)


def _bdot(a, b):
    return jnp.dot(a.astype(BF), b.astype(BF), preferred_element_type=F32)


def _bdot_nt(a, b):
    return lax.dot_general(a.astype(BF), b.astype(BF), (((1,), (1,)), ((), ())),
                           preferred_element_type=F32)


def _bdot_tn(a, b):
    return lax.dot_general(a.astype(BF), b.astype(BF), (((0,), (0,)), ((), ())),
                           preferred_element_type=F32)


def _layer_norm(z, g, b):
    zc = z - jnp.mean(z, -1, keepdims=True)
    var = jnp.mean(zc * zc, -1, keepdims=True)
    return zc * lax.rsqrt(var + LN_EPS) * g + b


def _load_tiled_rows(ref, idx, m):
    return jnp.concatenate([ref[(*idx, pl.ds(s, m, stride=ROW_SUB), slice(None))]
                            for s in range(ROW_SUB)], axis=1)


def _store_tiled_rows(ref, idx, val):
    m = val.shape[0]
    for s in range(ROW_SUB):
        ref[(*idx, pl.ds(s, m, stride=ROW_SUB), slice(None))] = val[:, s * 128:(s + 1) * 128]


def _proj_kernel(x_ref, w_ref, o_ref, xb_ref):
    @pl.when(pl.program_id(1) == 0)
    def _():
        xb_ref[...] = x_ref[...].astype(BF)

    o_ref[...] = jnp.dot(xb_ref[...], w_ref[...], preferred_element_type=F32)


def _proj(x, w_bf, tm, tn):
    m, k = x.shape
    n = w_bf.shape[1]
    return pl.pallas_call(
        _proj_kernel,
        grid=(m // tm, n // tn),
        in_specs=[pl.BlockSpec((tm, k), lambda i, j: (i, 0)),
                  pl.BlockSpec((k, tn), lambda i, j: (0, j))],
        out_specs=pl.BlockSpec((tm, tn), lambda i, j: (i, j)),
        out_shape=jax.ShapeDtypeStruct((m, n), F32),
        scratch_shapes=[pltpu.VMEM((tm, k), BF)],
        compiler_params=_params("parallel", "arbitrary"),
        name="proj",
    )(x, w_bf)


def _ret_tables(c_real, c_pad):
    lg = jnp.log1p(-jnp.exp2(-5.0 - jnp.arange(RET_HEADS, dtype=F32)))
    idx = jnp.arange(c_pad, dtype=F32)
    real = idx < c_real
    rel = idx[:, None] - idx[None, :]
    intra = jnp.where(rel >= 0, jnp.exp(lg[:, None, None] * jnp.maximum(rel, 0.0)), 0.0)
    intra = jnp.where(real[None, :, None] & real[None, None, :], intra, 0.0)
    q_dec = jnp.where(real[None, :], jnp.exp((idx[None, :] + 1.0) * lg[:, None]), 0.0)
    k_dec = jnp.where(real[None, :], jnp.exp((c_real - 1.0 - idx)[None, :] * lg[:, None]), 0.0)
    c_dec = jnp.exp(c_real * lg)
    bc = lambda t: jnp.broadcast_to(t[:, :, None], (RET_HEADS, c_pad, RET_DK))
    return intra, bc(q_dec), bc(k_dec), c_dec


def _rope_tables(pos):
    half = RET_DK // 2
    inv_freq = 1.0 / (ROPE_BASE ** (jnp.arange(half, dtype=F32) / half))
    ang = pos.astype(F32)[:, None] * inv_freq[None, :]
    cos, sin = jnp.cos(ang), jnp.sin(ang)
    return jnp.concatenate([cos, cos], -1), jnp.concatenate([-sin, sin], -1)


def _ret_kernel(cdec_ref, rq_ref, rk_ref, rv_ref, rg_ref, cos_ref, sin_ref, intra_ref, qdec_ref,
                kdec_ref, s0_ref, o_ref, s_out_ref, s_scr, *, n_chunks):
    c = pl.program_id(1)

    @pl.when(c == 0)
    def _():
        s_scr[...] = s0_ref[...]

    cos2 = cos_ref[...]
    sin2 = sin_ref[...]

    def rot(x):
        return x * cos2 + pltpu.roll(x, RET_DK // 2, 1) * sin2

    for bi in range(rq_ref.shape[0]):
        for h in range(RET_HEADS):
            q = rot(rq_ref[bi, :, h * RET_DK:(h + 1) * RET_DK])
            k = rot(rk_ref[bi, :, h * RET_DK:(h + 1) * RET_DK]) * (RET_DK ** -0.5)
            v = rv_ref[bi, :, h * RET_DV:(h + 1) * RET_DV].astype(BF)
            s_old = s_scr[bi, h]
            a = _bdot_nt(q, k) * intra_ref[h]
            o = _bdot(a, v) + _bdot(q * qdec_ref[h], s_old)
            s_scr[bi, h] = s_old * cdec_ref[h] + _bdot_tn(k * kdec_ref[h], v)
            o = o * lax.rsqrt(jnp.mean(o * o, -1, keepdims=True) + NORM_EPS)
            g = rg_ref[bi, :, h * RET_DV:(h + 1) * RET_DV]
            o_ref[bi, :, h * RET_DV:(h + 1) * RET_DV] = o * (g * jax.nn.sigmoid(g))

    @pl.when(c == n_chunks - 1)
    def _():
        s_out_ref[...] = s_scr[...]


def _retention(h3, pos, state0, c_real, c_pad, tb):
    b, l, _ = h3.shape
    n_chunks = l // c_pad
    intra, qdec, kdec, cdec = _ret_tables(c_real, c_pad)
    cos2, sin2 = _rope_tables(pos)
    full3 = lambda shape: pl.BlockSpec(shape, lambda i, c: (0, 0, 0))
    return pl.pallas_call(
        functools.partial(_ret_kernel, n_chunks=n_chunks),
        grid=(b // tb, n_chunks),
        in_specs=[
            pl.BlockSpec(memory_space=pltpu.SMEM),
            pl.BlockSpec((tb, c_pad, 512), lambda i, c: (i, c, COL_RQ // 512)),
            pl.BlockSpec((tb, c_pad, 512), lambda i, c: (i, c, COL_RK // 512)),
            pl.BlockSpec((tb, c_pad, 1024), lambda i, c: (i, c, COL_RV // 1024)),
            pl.BlockSpec((tb, c_pad, 1024), lambda i, c: (i, c, COL_RG // 1024)),
            pl.BlockSpec((c_pad, RET_DK), lambda i, c: (c, 0)),
            pl.BlockSpec((c_pad, RET_DK), lambda i, c: (c, 0)),
            full3((RET_HEADS, c_pad, c_pad)),
            full3((RET_HEADS, c_pad, RET_DK)),
            full3((RET_HEADS, c_pad, RET_DK)),
            pl.BlockSpec((tb, RET_HEADS, RET_DK, RET_DV), lambda i, c: (i, 0, 0, 0)),
        ],
        out_specs=[
            pl.BlockSpec((tb, c_pad, 1024), lambda i, c: (i, c, 0)),
            pl.BlockSpec((tb, RET_HEADS, RET_DK, RET_DV), lambda i, c: (i, 0, 0, 0)),
        ],
        out_shape=[jax.ShapeDtypeStruct((b, l, 1024), F32),
                   jax.ShapeDtypeStruct((b, RET_HEADS, RET_DK, RET_DV), F32)],
        scratch_shapes=[pltpu.VMEM((tb, RET_HEADS, RET_DK, RET_DV), F32)],
        compiler_params=_params("parallel", "arbitrary"),
        name="retention",
    )(cdec, h3, h3, h3, h3, cos2, sin2, intra, qdec, kdec, state0)


def _sink_softmax(s, sink):
    m = jnp.maximum(jnp.max(s, -1, keepdims=True), sink)
    p = jnp.exp(s - m)
    return p / (jnp.sum(p, -1, keepdims=True) + jnp.exp(sink - m))


def _swa_prompt_kernel(sinks_ref, q_ref, kp_ref, kc_ref, vp_ref, vc_ref, o_ref):
    n = pl.program_id(0)
    rows = SWA_GROUP * SWA_BLOCK
    qi = lax.broadcasted_iota(jnp.int32, (rows, 2 * SWA_BLOCK), 0) % SWA_BLOCK
    kj = lax.broadcasted_iota(jnp.int32, (rows, 2 * SWA_BLOCK), 1)
    rel = SWA_BLOCK + qi - kj
    valid = (rel >= 0) & (rel <= WINDOW) & ((kj >= SWA_BLOCK) | (n > 0))
    row_head = lax.broadcasted_iota(jnp.int32, (rows, 1), 0) // SWA_BLOCK
    for h in range(SWA_KV_HEADS):
        sl = slice(h * SWA_DH, (h + 1) * SWA_DH)
        k2 = jnp.concatenate([kp_ref[:, sl], kc_ref[:, sl]], 0).astype(BF)
        v2 = jnp.concatenate([vp_ref[:, sl], vc_ref[:, sl]], 0).astype(BF)
        heads = [h * SWA_GROUP + g for g in range(SWA_GROUP)]
        q = jnp.concatenate([q_ref[:, hq * SWA_DH:(hq + 1) * SWA_DH] for hq in heads], axis=0)
        sink = jnp.zeros((rows, 1), F32)
        for g, hq in enumerate(heads):
            sink = jnp.where(row_head == g, sinks_ref[hq], sink)
        s = _bdot_nt(q, k2) * (SWA_DH ** -0.5)
        s = jnp.where(valid, s, NEG)
        o = _bdot(_sink_softmax(s, sink), v2)
        for g, hq in enumerate(heads):
            o_ref[:, hq * SWA_DH:(hq + 1) * SWA_DH] = o[g * SWA_BLOCK:(g + 1) * SWA_BLOCK, :]


def _swa_prompt(h2, sinks):
    t = h2.shape[0]
    nb = t // SWA_BLOCK
    prev = lambda col: (lambda n: (jnp.maximum(n - 1, 0), col))
    cur = lambda col: (lambda n: (n, col))
    ck, cv = COL_SK // 256, COL_SV // 256
    return pl.pallas_call(
        _swa_prompt_kernel,
        grid=(nb,),
        in_specs=[
            pl.BlockSpec(memory_space=pltpu.SMEM),
            pl.BlockSpec((SWA_BLOCK, 1024), cur(COL_SQ // 1024)),
            pl.BlockSpec((SWA_BLOCK, 256), prev(ck)),
            pl.BlockSpec((SWA_BLOCK, 256), cur(ck)),
            pl.BlockSpec((SWA_BLOCK, 256), prev(cv)),
            pl.BlockSpec((SWA_BLOCK, 256), cur(cv)),
        ],
        out_specs=pl.BlockSpec((SWA_BLOCK, 1024), lambda n: (n, 0)),
        out_shape=jax.ShapeDtypeStruct((t, 1024), F32),
        compiler_params=_params("parallel"),
        name="swa_prompt",
    )(sinks, h2, h2, h2, h2, h2)


def _swa_sample_kernel(sinks_ref, q_ref, kn_ref, vn_ref, kp_ref, vp_ref, o_ref, *, n_new):
    tb = q_ref.shape[0]
    w = kp_ref.shape[1]
    p_ = SAMPLE_PAD
    rows = SWA_GROUP * p_
    qi = lax.broadcasted_iota(jnp.int32, (tb, rows, w), 1) % p_
    kj = lax.broadcasted_iota(jnp.int32, (tb, rows, w), 2)
    rel_prev = w + qi - kj
    valid_prev = (rel_prev >= 0) & (rel_prev <= WINDOW)
    qn = lax.broadcasted_iota(jnp.int32, (tb, rows, p_), 1) % p_
    kn = lax.broadcasted_iota(jnp.int32, (tb, rows, p_), 2)
    valid_new = (qn - kn >= 0) & (qn - kn <= WINDOW) & (kn < n_new)
    row_head = lax.broadcasted_iota(jnp.int32, (1, rows, 1), 1) // p_
    bdot = lambda eq, a, b: jnp.einsum(eq, a.astype(BF), b.astype(BF), preferred_element_type=F32)
    for h in range(SWA_KV_HEADS):
        sl = slice(h * SWA_DH, (h + 1) * SWA_DH)
        kp, vp = kp_ref[:, :, sl], vp_ref[:, :, sl]
        kn_h, vn_h = kn_ref[:, :, sl], vn_ref[:, :, sl]
        heads = [h * SWA_GROUP + g for g in range(SWA_GROUP)]
        q = jnp.concatenate([q_ref[:, :, hq * SWA_DH:(hq + 1) * SWA_DH] for hq in heads], axis=1)
        sink = jnp.zeros((1, rows, 1), F32)
        for g, hq in enumerate(heads):
            sink = jnp.where(row_head == g, sinks_ref[hq], sink)
        sp = bdot('bqd,bkd->bqk', q, kp) * (SWA_DH ** -0.5)
        sn = bdot('bqd,bkd->bqk', q, kn_h) * (SWA_DH ** -0.5)
        sp = jnp.where(valid_prev, sp, NEG)
        sn = jnp.where(valid_new, sn, NEG)
        m = jnp.maximum(jnp.maximum(jnp.max(sp, -1, keepdims=True),
                                    jnp.max(sn, -1, keepdims=True)), sink)
        pp = jnp.exp(sp - m)
        pn = jnp.exp(sn - m)
        den = jnp.sum(pp, -1, keepdims=True) + jnp.sum(pn, -1, keepdims=True) + jnp.exp(sink - m)
        o = bdot('bqk,bkd->bqd', pp / den, vp) + bdot('bqk,bkd->bqd', pn / den, vn_h)
        for g, hq in enumerate(heads):
            o_ref[:, :, hq * SWA_DH:(hq + 1) * SWA_DH] = o[:, g * p_:(g + 1) * p_, :]


def _swa_sample(h3, prev_k, prev_v, sinks, n_new, tb=8):
    b = h3.shape[0]
    w = prev_k.shape[1]
    return pl.pallas_call(
        functools.partial(_swa_sample_kernel, n_new=n_new),
        grid=(b // tb,),
        in_specs=[
            pl.BlockSpec(memory_space=pltpu.SMEM),
            pl.BlockSpec((tb, SAMPLE_PAD, 1024), lambda i: (i, 0, COL_SQ // 1024)),
            pl.BlockSpec((tb, SAMPLE_PAD, 256), lambda i: (i, 0, COL_SK // 256)),
            pl.BlockSpec((tb, SAMPLE_PAD, 256), lambda i: (i, 0, COL_SV // 256)),
            pl.BlockSpec((tb, w, 256), lambda i: (i, 0, 0)),
            pl.BlockSpec((tb, w, 256), lambda i: (i, 0, 0)),
        ],
        out_specs=pl.BlockSpec((tb, SAMPLE_PAD, 1024), lambda i: (i, 0, 0)),
        out_shape=jax.ShapeDtypeStruct((b, SAMPLE_PAD, 1024), F32),
        compiler_params=_params("parallel"),
        name="swa_sample",
    )(sinks, h3, h3, h3, prev_k, prev_v)


def _mem_head(q, k, v):
    s = _bdot_nt(q, k) * (MEM_DH ** -0.5)
    m = jnp.max(s, -1, keepdims=True)
    e = jnp.exp(s - m)
    return _bdot(e / jnp.sum(e, -1, keepdims=True), v)


def _mem_kernel(q_ref, mk_ref, mv_ref, o_ref):
    for h in range(MEM_HEADS):
        sl = slice(h * MEM_DH, (h + 1) * MEM_DH)
        o_ref[0, :, sl] = _mem_head(q_ref[0, :, sl], mk_ref[0, :, sl], mv_ref[0, :, sl])


def _mem_cached_kernel(q_ref, mk_hbm, mv_hbm, o_ref, kv_ref, sem_ref):
    b = pl.program_id(0)
    slot = b % 2

    def copies(seq, s):
        return [pltpu.make_async_copy(src.at[seq, :, h, :], kv_ref.at[s, j, h], sem_ref.at[s, j, h])
                for j, src in enumerate((mk_hbm, mv_hbm)) for h in range(MEM_HEADS)]

    @pl.when(b == 0)
    def _():
        for c in copies(0, 0):
            c.start()

    @pl.when(b + 1 < pl.num_programs(0))
    def _():
        for c in copies(b + 1, 1 - slot):
            c.start()

    for c in copies(b, slot):
        c.wait()
    for h in range(MEM_HEADS):
        sl = slice(h * MEM_DH, (h + 1) * MEM_DH)
        o_ref[0, :, sl] = _mem_head(q_ref[0, :, sl], kv_ref[slot, 0, h], kv_ref[slot, 1, h])


def _mem_attend_cached(h3, mk, mv):
    b, l, _ = h3.shape
    return pl.pallas_call(
        _mem_cached_kernel,
        grid=(b,),
        in_specs=[
            pl.BlockSpec((1, l, 1024), lambda i: (i, 0, COL_MQ // 1024)),
            pl.BlockSpec(memory_space=pl.ANY),
            pl.BlockSpec(memory_space=pl.ANY),
        ],
        out_specs=pl.BlockSpec((1, l, 1024), lambda i: (i, 0, 0)),
        out_shape=jax.ShapeDtypeStruct((b, l, 1024), F32),
        scratch_shapes=[pltpu.VMEM((2, 2, MEM_HEADS, N_MEM, MEM_DH), F32),
                        pltpu.SemaphoreType.DMA((2, 2, MEM_HEADS))],
        compiler_params=_params("arbitrary"),
        name="mem_attend_cached",
    )(h3, mk, mv)


def _mem_attend(h3, mk, mv, tl):
    b, l, _ = h3.shape
    return pl.pallas_call(
        _mem_kernel,
        grid=(b, l // tl),
        in_specs=[
            pl.BlockSpec((1, tl, 1024), lambda i, j: (i, j, COL_MQ // 1024)),
            pl.BlockSpec((1, N_MEM, 1024), lambda i, j: (i, 0, 0)),
            pl.BlockSpec((1, N_MEM, 1024), lambda i, j: (i, 0, 0)),
        ],
        out_specs=pl.BlockSpec((1, tl, 1024), lambda i, j: (i, j, 0)),
        out_shape=jax.ShapeDtypeStruct((b, l, 1024), F32),
        compiler_params=_params("parallel", "parallel"),
        name="mem_attend",
    )(h3, mk, mv)


def _merge_kernel(ro_ref, so_ref, mo_ref, g0_ref, g1_ref, g2_ref, x_ref, wb_ref, wo_ref, g_ref,
                  b_ref, o_ref, *, alpha):
    acc = None
    for n, (br, gl) in enumerate(((ro_ref, g0_ref), (so_ref, g1_ref), (mo_ref, g2_ref))):
        term = jax.nn.sigmoid(gl[...]) * jnp.dot(br[...].astype(BF), wb_ref[n],
                                                 preferred_element_type=F32)
        acc = term if acc is None else acc + term
    a = jnp.dot(acc.astype(BF), wo_ref[...], preferred_element_type=F32)
    _store_tiled_rows(o_ref, (), _layer_norm(alpha * x_ref[...] + a, g_ref[...], b_ref[...]))


def _merge(ro, so, mo, h2, x2, wb_bf, wo_bf, g, b, alpha, tm):
    t = x2.shape[0]
    tile = lambda col: pl.BlockSpec((tm, 1024), lambda i: (i, col))
    gl0 = COL_GL // 1024
    return pl.pallas_call(
        functools.partial(_merge_kernel, alpha=alpha),
        grid=(t // tm,),
        in_specs=[tile(0), tile(0), tile(0), tile(gl0), tile(gl0 + 1), tile(gl0 + 2), tile(0),
                  pl.BlockSpec((N_BRANCH, 1024, 1024), lambda i: (0, 0, 0)),
                  pl.BlockSpec((1024, 1024), lambda i: (0, 0)),
                  pl.BlockSpec((1, 1024), lambda i: (0, 0)),
                  pl.BlockSpec((1, 1024), lambda i: (0, 0))],
        out_specs=pl.BlockSpec((tm * ROW_SUB, 128), lambda i: (i, 0)),
        out_shape=jax.ShapeDtypeStruct((t * ROW_SUB, 128), F32),
        compiler_params=_params("parallel"),
        name="merge_ln1",
    )(ro, so, mo, h2, h2, h2, x2, wb_bf, wo_bf, g, b)


def _first_index_of_max(v, iota, big, axes):
    m = jnp.max(v, axis=axes, keepdims=True)
    idx = jnp.min(jnp.where(v == m, iota, big), axis=axes, keepdims=True)
    return m, idx


def _router_kernel(x_ref, wr_ref, bias_ref, eidx_ref, ew_ref):
    tt = x_ref.shape[0] // ROW_SUB
    x = _load_tiled_rows(x_ref, (), tt).astype(BF)
    logits = lax.dot_general(wr_ref[...], x, (((1,), (1,)), ((), ())),
                             preferred_element_type=F32)
    s = jax.nn.sigmoid(logits).reshape(N_GROUPS, GROUP_SIZE, tt)
    sb = s + bias_ref[...].reshape(N_GROUPS, GROUP_SIZE, 1)
    ninf = -jnp.inf
    r_iota = lax.broadcasted_iota(jnp.int32, sb.shape, 1)
    m1, i1 = _first_index_of_max(sb, r_iota, GROUP_SIZE, 1)
    m2 = jnp.max(jnp.where(r_iota == i1, ninf, sb), axis=1, keepdims=True)
    gsc = (m1 + m2).reshape(N_GROUPS, tt)
    g_iota = lax.broadcasted_iota(jnp.int32, gsc.shape, 0)
    gmask = jnp.zeros(gsc.shape, jnp.bool_)
    for _ in range(TOPK_GROUPS):
        _, gi = _first_index_of_max(gsc, g_iota, N_GROUPS, 0)
        hit = g_iota == gi
        gmask = gmask | hit
        gsc = jnp.where(hit, ninf, gsc)
    cand = jnp.where(gmask.reshape(N_GROUPS, 1, tt), sb, ninf)
    e_iota = lax.broadcasted_iota(jnp.int32, sb.shape, 0) * GROUP_SIZE + r_iota
    idxs, ws = [], []
    for _ in range(TOP_K):
        _, ei = _first_index_of_max(cand, e_iota, N_EXPERTS, (0, 1))
        hit = e_iota == ei
        idxs.append(ei.reshape(1, tt))
        ws.append(jnp.sum(jnp.where(hit, s, 0.0), axis=(0, 1)).reshape(1, tt))
        cand = jnp.where(hit, ninf, cand)
    w = jnp.concatenate(ws, 0)
    eidx_ref[...] = jnp.concatenate(idxs, 0)
    ew_ref[...] = w / jnp.sum(w, 0, keepdims=True) * ROUTED_SCALE


def _router(xt, wr_t_bf, bias_col, tt):
    t = xt.shape[0] // ROW_SUB
    return pl.pallas_call(
        _router_kernel,
        grid=(t // tt,),
        in_specs=[pl.BlockSpec((tt * ROW_SUB, 128), lambda i: (i, 0)),
                  pl.BlockSpec((N_EXPERTS, 1024), lambda i: (0, 0)),
                  pl.BlockSpec((N_EXPERTS, 1), lambda i: (0, 0))],
        out_specs=[pl.BlockSpec((TOP_K, tt), lambda i: (0, i)),
                   pl.BlockSpec((TOP_K, tt), lambda i: (0, i))],
        out_shape=[jax.ShapeDtypeStruct((TOP_K, t), jnp.int32),
                   jax.ShapeDtypeStruct((TOP_K, t), F32)],
        compiler_params=_params("parallel"),
        name="router",
    )(xt, wr_t_bf, bias_col)


IDX_BITS = 18


def _sorted_keys(eidx):
    n_grp = eidx.shape[0]
    flat_e = eidx.reshape(n_grp, -1)
    a = flat_e.shape[1]
    assert a < (1 << IDX_BITS) and n_grp * N_EXPERTS << IDX_BITS < 2 ** 31
    grp_exp = jnp.arange(n_grp, dtype=jnp.int32)[:, None] * N_EXPERTS + flat_e
    keys = grp_exp * (1 << IDX_BITS) + jnp.arange(a, dtype=jnp.int32)[None, :]
    return jnp.sort(keys.reshape(-1)).reshape(n_grp, a) & ((N_EXPERTS << IDX_BITS) - 1)


def _lookup(table, idx):
    hit = idx[..., None] == jnp.arange(table.shape[0], dtype=idx.dtype)
    return jnp.sum(jnp.where(hit, table, 0), axis=-1)


def _dispatch(eidx, ew, skey, t):
    a = t * TOP_K
    nblk = -(-a // MOE_BLOCK) + N_EXPERTS
    assert nblk % STEP_BLOCKS == 0 and STEP_BLOCKS % 2 == 0
    flat_e = eidx.reshape(-1)
    si = skey & ((1 << IDX_BITS) - 1)
    experts = jnp.arange(N_EXPERTS, dtype=jnp.int32)
    counts = jnp.sum((flat_e[None, :] == experts[:, None]).astype(jnp.int32), axis=1)
    grp_start = jnp.cumsum(counts) - counts
    padded = (counts + MOE_BLOCK - 1) // MOE_BLOCK * MOE_BLOCK
    pad_end = jnp.cumsum(padded)
    pad_start = pad_end - padded
    blk_first = jnp.arange(nblk, dtype=jnp.int32) * MOE_BLOCK
    blk_exp = jnp.minimum(jnp.sum((pad_end[None, :] <= blk_first[:, None]).astype(jnp.int32), axis=1),
                          N_EXPERTS - 1)
    off = (jnp.arange(nblk * MOE_BLOCK, dtype=jnp.int32).reshape(nblk, MOE_BLOCK)
           - _lookup(pad_start, blk_exp)[:, None])
    valid = off < _lookup(counts, blk_exp)[:, None]
    src = jnp.clip(_lookup(grp_start, blk_exp)[:, None] + off, 0, a - 1)
    row_si = si[src]
    row_tok = jnp.where(valid, row_si >> 3, t).astype(jnp.int32)
    row_w = jnp.where(valid, ew.reshape(-1)[row_si], 0.0)
    nused = (pad_end[-1] // MOE_BLOCK).astype(jnp.int32).reshape(1)
    gidx, nxt = _group_tables(counts, blk_exp)
    step_rows = STEP_BLOCKS * MOE_BLOCK
    return (row_tok.reshape(-1), row_w.reshape(nblk // STEP_BLOCKS, 1, step_rows), blk_exp, nused,
            gidx, nxt)


def _group_tables(counts, blk_exp):
    nonempty = counts > 0
    gidx = _lookup(jnp.cumsum(nonempty.astype(jnp.int32)) - 1, blk_exp)
    experts = jnp.arange(N_EXPERTS, dtype=jnp.int32)
    cand = jnp.where(nonempty, experts, N_EXPERTS)
    later = lax.cummin(cand, axis=0, reverse=True)
    nxt = jnp.concatenate([later[1:], jnp.full((2,), N_EXPERTS, jnp.int32)])
    nxt1 = _lookup(nxt, blk_exp)
    nxt2 = _lookup(nxt, jnp.minimum(nxt1, N_EXPERTS))
    none = lambda a: jnp.where(a >= N_EXPERTS, -1, a).astype(jnp.int32)
    return gidx.astype(jnp.int32), jnp.concatenate([none(nxt1), none(nxt2)])


WEIGHT_SLOTS = 3


def _weight_copies(hbm_refs, buf_ref, sem_ref, e, slot):
    return [pltpu.make_async_copy(h.at[e], buf_ref.at[slot, k], sem_ref.at[slot, k])
            for k, h in enumerate(hbm_refs)]


def _stage_weights(b, be_ref, nused_ref, gidx_ref, nxt_ref, hbm_refs, buf_ref, sem_ref, cache_refs):
    first = ((b == 0) | (be_ref[b] != be_ref[jnp.maximum(b - 1, 0)])) & (b < nused_ref[0])
    nblk = be_ref.shape[0]

    def start(e, slot):
        for c in _weight_copies(hbm_refs, buf_ref, sem_ref, e, slot):
            c.start()

    @pl.when(first)
    def _():
        slot = gidx_ref[b] % WEIGHT_SLOTS

        @pl.when(b == 0)
        def _():
            start(be_ref[0], 0)

            @pl.when(nxt_ref[0] >= 0)
            def _():
                start(nxt_ref[0], 1)

        for c in _weight_copies(hbm_refs, buf_ref, sem_ref, be_ref[b], slot):
            c.wait()
        nxt2 = nxt_ref[nblk + b]

        @pl.when(nxt2 >= 0)
        def _():
            start(nxt2, (slot + 2) % WEIGHT_SLOTS)

        for k, cache in enumerate(cache_refs):
            cache[...] = buf_ref[slot, k].astype(BF)


def _gather_rows(tok_ref, b, x_ref, xg_ref, slot):
    for r in range(MOE_BLOCK):
        xg_ref[slot, r * ROW_SUB:(r + 1) * ROW_SUB, :] = x_ref[tok_ref[b * MOE_BLOCK + r]]


def _up_kernel(be_ref, nused_ref, gidx_ref, nxt_ref, tok_ref, x_ref, wg_hbm, wu_hbm, act_ref,
               xg_ref, wbuf_ref, wgb_ref, wub_ref, sem_ref):
    i = pl.program_id(0)
    nblk = pl.num_programs(0) * STEP_BLOCKS
    b0 = i * STEP_BLOCKS

    @pl.when(i == 0)
    def _():
        _gather_rows(tok_ref, 0, x_ref, xg_ref, 0)

    @pl.when(b0 < nused_ref[0])
    def _():
        for j in range(STEP_BLOCKS):
            b = b0 + j
            _stage_weights(b, be_ref, nused_ref, gidx_ref, nxt_ref, (wg_hbm, wu_hbm), wbuf_ref,
                           sem_ref, (wgb_ref, wub_ref))
            _gather_rows(tok_ref, jnp.minimum(b + 1, nblk - 1), x_ref, xg_ref, (j + 1) % 2)
            x = _load_tiled_rows(xg_ref, (j % 2,), MOE_BLOCK).astype(BF)
            g = jnp.dot(x, wgb_ref[...], preferred_element_type=F32)
            u = jnp.dot(x, wub_ref[...], preferred_element_type=F32)
            act_ref[j * MOE_BLOCK:(j + 1) * MOE_BLOCK, :] = ((g * jax.nn.sigmoid(g)) * u).astype(BF)

    @pl.when(b0 >= nused_ref[0])
    def _():
        act_ref[...] = jnp.zeros_like(act_ref)


def _experts_up(x, row_tok, blk_exp, nused, gidx, nxt, w_gate, w_up):
    nblk = blk_exp.shape[0]
    step_rows = STEP_BLOCKS * MOE_BLOCK
    return pl.pallas_call(
        _up_kernel,
        grid_spec=pltpu.PrefetchScalarGridSpec(
            num_scalar_prefetch=5,
            grid=(nblk // STEP_BLOCKS,),
            in_specs=[pl.BlockSpec(memory_space=pltpu.VMEM),
                      pl.BlockSpec(memory_space=pl.ANY),
                      pl.BlockSpec(memory_space=pl.ANY)],
            out_specs=pl.BlockSpec((step_rows, D_EXPERT), lambda i, *_: (i, 0)),
            scratch_shapes=[pltpu.VMEM((2, MOE_BLOCK * ROW_SUB, 128), F32),
                            pltpu.VMEM((WEIGHT_SLOTS, 2, D_MODEL, D_EXPERT), F32),
                            pltpu.VMEM((D_MODEL, D_EXPERT), BF),
                            pltpu.VMEM((D_MODEL, D_EXPERT), BF),
                            pltpu.SemaphoreType.DMA((WEIGHT_SLOTS, 2))],
        ),
        out_shape=jax.ShapeDtypeStruct((nblk * MOE_BLOCK, D_EXPERT), BF),
        compiler_params=_params("arbitrary"),
        name="experts_up",
    )(blk_exp, nused, gidx, nxt, row_tok, x, w_gate, w_up)


SCATTER_GROUP = 8


def _row_to_column(row):
    n = row.shape[1]
    eye = lax.broadcasted_iota(jnp.int32, (n, n), 0) == lax.broadcasted_iota(jnp.int32, (n, n), 1)
    return jnp.sum(jnp.where(eye, jnp.broadcast_to(row, (n, n)), 0.0), axis=1, keepdims=True)


def _scatter_add_rows(tok_ref, b, y_ref, yb_ref, slot):
    for r0 in range(0, MOE_BLOCK, SCATTER_GROUP):
        rs = range(r0, r0 + SCATTER_GROUP)
        toks = [tok_ref[b * MOE_BLOCK + r] for r in rs]
        new = [y_ref[t] + yb_ref[slot, r * ROW_SUB:(r + 1) * ROW_SUB, :] for r, t in zip(rs, toks)]
        for t, v in zip(toks, new):
            y_ref[t] = v


def _down_kernel(be_ref, nused_ref, gidx_ref, nxt_ref, tok_ref, act_ref, rw_ref, wd_hbm, y_ref,
                 yb_ref, wbuf_ref, wdb_ref, sem_ref):
    i = pl.program_id(0)
    nblk = pl.num_programs(0) * STEP_BLOCKS
    b0 = i * STEP_BLOCKS

    @pl.when(i == 0)
    def _():
        y_ref[...] = jnp.zeros_like(y_ref)
        yb_ref[1] = jnp.zeros(yb_ref.shape[1:], F32)

    @pl.when(b0 <= nused_ref[0])
    def _():
        for j in range(STEP_BLOCKS):
            b = b0 + j
            rows = slice(j * MOE_BLOCK, (j + 1) * MOE_BLOCK)
            _stage_weights(b, be_ref, nused_ref, gidx_ref, nxt_ref, (wd_hbm,), wbuf_ref, sem_ref,
                           (wdb_ref,))
            yb = jnp.dot(act_ref[rows, :], wdb_ref[...],
                         preferred_element_type=F32) * _row_to_column(rw_ref[0, :, rows])
            _store_tiled_rows(yb_ref, (j % 2,), yb)
            _scatter_add_rows(tok_ref, jnp.maximum(b - 1, 0), y_ref, yb_ref, (j + 1) % 2)

    @pl.when((i == pl.num_programs(0) - 1) & (nused_ref[0] >= nblk))
    def _():
        _scatter_add_rows(tok_ref, nblk - 1, y_ref, yb_ref, (STEP_BLOCKS - 1) % 2)


def _experts_down(act, row_tok, row_w, blk_exp, nused, gidx, nxt, w_down, t_rows):
    nblk = blk_exp.shape[0]
    step_rows = STEP_BLOCKS * MOE_BLOCK
    return pl.pallas_call(
        _down_kernel,
        grid_spec=pltpu.PrefetchScalarGridSpec(
            num_scalar_prefetch=5,
            grid=(nblk // STEP_BLOCKS,),
            in_specs=[pl.BlockSpec((step_rows, D_EXPERT), lambda i, *_: (i, 0)),
                      pl.BlockSpec((1, 1, step_rows), lambda i, *_: (i, 0, 0)),
                      pl.BlockSpec(memory_space=pl.ANY)],
            out_specs=pl.BlockSpec(memory_space=pltpu.VMEM),
            scratch_shapes=[pltpu.VMEM((2, MOE_BLOCK * ROW_SUB, 128), F32),
                            pltpu.VMEM((WEIGHT_SLOTS, 1, D_EXPERT, D_MODEL), F32),
                            pltpu.VMEM((D_EXPERT, D_MODEL), BF),
                            pltpu.SemaphoreType.DMA((WEIGHT_SLOTS, 1))],
        ),
        out_shape=jax.ShapeDtypeStruct((t_rows, ROW_SUB, 128), F32),
        compiler_params=_params("arbitrary"),
        name="experts_down",
    )(blk_exp, nused, gidx, nxt, row_tok, act, row_w, w_down)


def _final_kernel(h_ref, yr_ref, wg_ref, wu_ref, wd_ref, g_ref, b_ref, o_ref, *, alpha):
    h = _load_tiled_rows(h_ref, (), o_ref.shape[0])
    hb = h.astype(BF)
    g = jnp.dot(hb, wg_ref[...], preferred_element_type=F32)
    u = jnp.dot(hb, wu_ref[...], preferred_element_type=F32)
    shared = jnp.dot(((g * jax.nn.sigmoid(g)) * u).astype(BF), wd_ref[...],
                     preferred_element_type=F32)
    f = _load_tiled_rows(yr_ref, (), h.shape[0]) + shared
    o_ref[...] = _layer_norm(alpha * h + f, g_ref[...], b_ref[...])


def _final(ht, yr, wg_bf, wu_bf, wd_bf, g, b, alpha, tm):
    t = ht.shape[0] // ROW_SUB
    tiled = pl.BlockSpec((tm * ROW_SUB, 128), lambda i: (i, 0))
    full = lambda shape: pl.BlockSpec(shape, lambda i: (0, 0))
    return pl.pallas_call(
        functools.partial(_final_kernel, alpha=alpha),
        grid=(t // tm,),
        in_specs=[tiled, tiled, full((1024, 256)), full((1024, 256)), full((256, 1024)),
                  full((1, 1024)), full((1, 1024))],
        out_specs=pl.BlockSpec((tm, 1024), lambda i: (i, 0)),
        out_shape=jax.ShapeDtypeStruct((t, 1024), F32),
        compiler_params=_params("parallel"),
        name="shared_ln2",
    )(ht, yr, wg_bf, wu_bf, wd_bf, g, b)


def _moe_ln2(ht, tables, w_gate, w_up, w_down, wsg_bf, wsu_bf, wsd_bf, g, b, alpha):
    t = ht.shape[0] // ROW_SUB
    row_tok, row_w, blk_exp, nused, gidx, nxt = tables
    act = _experts_up(ht.reshape(t, ROW_SUB, 128), jnp.minimum(row_tok, t - 1), blk_exp, nused, gidx,
                      nxt, w_gate, w_up)
    yr = _experts_down(act, row_tok, row_w, blk_exp, nused, gidx, nxt, w_down, t + 8)
    return _final(ht, yr.reshape((t + 8) * ROW_SUB, 128), wsg_bf, wsu_bf, wsd_bf, g, b, alpha, 256)


def _permute_w_in(w_in):
    rq, rk, rv, rg, sq, sk, sv, mq, gl = jnp.split(
        w_in, [512, 1024, 2048, 3072, 4096, 4352, 4608, 5632], axis=-1)
    return jnp.concatenate([rq, rk, rv, rg, sq, mq, gl, sk, sv], -1)


def kernel(x_prompt, x_sample, mem_prompt, cache_ret_state, cache_swa_k, cache_swa_v, cache_mem_k,
           cache_mem_v, w_in, swa_sinks, w_mem_kv, w_branch, w_o, ln1_g, ln1_b, w_router,
           router_bias, w_gate_e, w_up_e, w_down_e, w_sh_gate, w_sh_up, w_sh_down, ln2_g, ln2_b):
    depth = w_in.shape[0]
    assert depth == 1
    alpha = (2.0 * depth) ** 0.25
    bp, lp, d = x_prompt.shape
    bs, ls, _ = x_sample.shape
    l = 0

    w_in_bf = _permute_w_in(w_in[l]).astype(BF)
    sinks = swa_sinks[l]
    wb_bf = w_branch[l].astype(BF)
    wo_bf = w_o[l].astype(BF)
    g1, b1 = ln1_g[l].reshape(1, d), ln1_b[l].reshape(1, d)
    g2, b2 = ln2_g[l].reshape(1, d), ln2_b[l].reshape(1, d)

    tp = bp * lp
    xp2 = x_prompt.reshape(tp, d)
    hp2 = _proj(xp2, w_in_bf, 1024, PROJ_TN)
    hp3 = hp2.reshape(bp, lp, D_IN)
    mkv = _proj(mem_prompt.reshape(bp * N_MEM, d), w_mem_kv[l].astype(BF), N_MEM, 512)
    mk_p, mv_p = mkv[:, :1024].reshape(bp, N_MEM, 1024), mkv[:, 1024:].reshape(bp, N_MEM, 1024)
    rs0 = jnp.zeros((bp, RET_HEADS, RET_DK, RET_DV), F32)
    ro_p, rs_p = _retention(hp3, jnp.arange(lp), rs0, RET_CHUNK, RET_CHUNK, 1)
    so_p = _swa_prompt(hp2, sinks)
    mo_p = _mem_attend(hp3, mk_p, mv_p, 256)
    hmid_p = _merge(ro_p.reshape(tp, d), so_p, mo_p.reshape(tp, d), hp2, xp2, wb_bf, wo_bf,
                    g1, b1, alpha, 256)

    ts = bs * SAMPLE_PAD
    xs3 = jnp.pad(x_sample, ((0, 0), (0, SAMPLE_PAD - ls), (0, 0)))
    xs2 = xs3.reshape(ts, d)
    hs2 = _proj(xs2, w_in_bf, ts, PROJ_TN)
    hs3 = hs2.reshape(bs, SAMPLE_PAD, D_IN)
    pos_s = PAST_LEN + jnp.arange(SAMPLE_PAD)
    ro_s, rs_s = _retention(hs3, pos_s, cache_ret_state.reshape(bs, RET_HEADS, RET_DK, RET_DV), ls,
                            SAMPLE_PAD, 4)
    w_buf = cache_swa_k.shape[2]
    prev_k = cache_swa_k.reshape(bs, w_buf, SWA_KV_HEADS * SWA_DH)
    prev_v = cache_swa_v.reshape(bs, w_buf, SWA_KV_HEADS * SWA_DH)
    so_s = _swa_sample(hs3, prev_k, prev_v, sinks, ls)
    mo_s = _mem_attend_cached(hs3, cache_mem_k.reshape(bs, N_MEM, MEM_HEADS, MEM_DH),
                              cache_mem_v.reshape(bs, N_MEM, MEM_HEADS, MEM_DH))
    hmid_s = _merge(ro_s.reshape(ts, d), so_s.reshape(ts, d), mo_s.reshape(ts, d), hs2, xs2,
                    wb_bf, wo_bf, g1, b1, alpha, 256)
    hmid_s = hmid_s.reshape(bs, SAMPLE_PAD, ROW_SUB, 128)[:, :ls].reshape(bs * ls * ROW_SUB, 128)

    t_all = tp + bs * ls
    assert t_all % (2 * ROUTER_TILE) == 0 and t_all // 2 <= tp
    th = t_all // 2
    groups = (hmid_p[:th * ROW_SUB], jnp.concatenate([hmid_p[th * ROW_SUB:], hmid_s], 0))
    wr_t_bf, bias_col = w_router[l].T.astype(BF), router_bias[l].reshape(N_EXPERTS, 1)
    routed = [_router(ht, wr_t_bf, bias_col, ROUTER_TILE) for ht in groups]
    eidx = jnp.stack([e.T for e, _ in routed])
    tables = jax.vmap(functools.partial(_dispatch, t=th))(
        eidx, jnp.stack([w.T for _, w in routed]), _sorted_keys(eidx))
    y_a, y_b = (
        _moe_ln2(ht, jax.tree.map(lambda a: a[i], tables), w_gate_e[l], w_up_e[l], w_down_e[l],
                 w_sh_gate[l].astype(BF), w_sh_up[l].astype(BF), w_sh_down[l].astype(BF), g2, b2, alpha)
        for i, ht in enumerate(groups))

    y_p = jnp.concatenate([y_a, y_b[:tp - th]], 0).reshape(bp, lp, d)
    y_s = y_b[tp - th:].reshape(bs, ls, d)
    kv4 = lambda a, n: a.reshape(1, a.shape[0], n, SWA_KV_HEADS, SWA_DH)
    k_p = kv4(hp3[:, lp - WINDOW:, COL_SK:COL_SK + 256], WINDOW)
    v_p = kv4(hp3[:, lp - WINDOW:, COL_SV:COL_SV + 256], WINDOW)
    mem4 = lambda a: a.reshape(1, bp, N_MEM, MEM_HEADS, MEM_DH)
    k_s = kv4(jnp.concatenate([prev_k, hs3[:, :ls, COL_SK:COL_SK + 256]], 1)[:, -w_buf:], w_buf)
    v_s = kv4(jnp.concatenate([prev_v, hs3[:, :ls, COL_SV:COL_SV + 256]], 1)[:, -w_buf:], w_buf)
    return (y_p, y_s, rs_p[None], k_p, v_p, mem4(mk_p), mem4(mv_p), rs_s[None], k_s, v_s)
```

```python
import functools

import jax
import jax.numpy as jnp
from jax import lax
from jax.experimental import pallas as pl
from jax.experimental.pallas import tpu as pltpu

BF = jnp.bfloat16
F32 = jnp.float32

D_MODEL = 1024
RET_HEADS = 4
RET_DK = 128
RET_DV = 256
RET_CHUNK = 128
ROPE_BASE = 10000.0
SWA_HEADS = 16
SWA_KV_HEADS = 4
SWA_GROUP = SWA_HEADS // SWA_KV_HEADS
SWA_DH = 64
WINDOW = 128
SWA_BLOCK = 128
N_MEM = 256
MEM_HEADS = 4
MEM_DH = 256
N_BRANCH = 3
N_EXPERTS = 256
TOP_K = 8
N_GROUPS = 8
GROUP_SIZE = N_EXPERTS // N_GROUPS
TOPK_GROUPS = 4
D_EXPERT = 256
ROUTED_SCALE = 2.5
MOE_BLOCK = 128
ROUTER_TILE = 384
ROW_SUB = D_MODEL // 128
STEP_BLOCKS = 8
LN_EPS = 1e-5
NORM_EPS = 1e-6
NEG = -1e30
PAST_LEN = 16384
SAMPLE_PAD = 8

COL_RQ, COL_RK, COL_RV, COL_RG, COL_SQ, COL_MQ, COL_GL, COL_SK, COL_SV = (
    0, 512, 1024, 2048, 3072, 4096, 5120, 8192, 8448)
D_IN = 8704
PROJ_TN = D_IN // 4

VMEM_LIMIT = 56 * 1024 * 1024


def _params(*sem):
    return pltpu.CompilerParams(dimension_semantics=sem, vmem_limit_bytes=VMEM_LIMIT)


def _bdot(a, b):
    return jnp.dot(a.astype(BF), b.astype(BF), preferred_element_type=F32)


def _bdot_nt(a, b):
    return lax.dot_general(a.astype(BF), b.astype(BF), (((1,), (1,)), ((), ())),
                           preferred_element_type=F32)


def _bdot_tn(a, b):
    return lax.dot_general(a.astype(BF), b.astype(BF), (((0,), (0,)), ((), ())),
                           preferred_element_type=F32)


def _layer_norm(z, g, b):
    zc = z - jnp.mean(z, -1, keepdims=True)
    var = jnp.mean(zc * zc, -1, keepdims=True)
    return zc * lax.rsqrt(var + LN_EPS) * g + b


def _load_tiled_rows(ref, idx, m):
    return jnp.concatenate([ref[(*idx, pl.ds(s, m, stride=ROW_SUB), slice(None))]
                            for s in range(ROW_SUB)], axis=1)


def _store_tiled_rows(ref, idx, val):
    m = val.shape[0]
    for s in range(ROW_SUB):
        ref[(*idx, pl.ds(s, m, stride=ROW_SUB), slice(None))] = val[:, s * 128:(s + 1) * 128]


def _proj_kernel(x_ref, w_ref, o_ref, xb_ref):
    @pl.when(pl.program_id(1) == 0)
    def _():
        xb_ref[...] = x_ref[...].astype(BF)

    o_ref[...] = jnp.dot(xb_ref[...], w_ref[...], preferred_element_type=F32)


def _proj(x, w_bf, tm, tn):
    m, k = x.shape
    n = w_bf.shape[1]
    return pl.pallas_call(
        _proj_kernel,
        grid=(m // tm, n // tn),
        in_specs=[pl.BlockSpec((tm, k), lambda i, j: (i, 0)),
                  pl.BlockSpec((k, tn), lambda i, j: (0, j))],
        out_specs=pl.BlockSpec((tm, tn), lambda i, j: (i, j)),
        out_shape=jax.ShapeDtypeStruct((m, n), F32),
        scratch_shapes=[pltpu.VMEM((tm, k), BF)],
        compiler_params=_params("parallel", "arbitrary"),
        name="proj",
    )(x, w_bf)


def _ret_tables(c_real, c_pad):
    lg = jnp.log1p(-jnp.exp2(-5.0 - jnp.arange(RET_HEADS, dtype=F32)))
    idx = jnp.arange(c_pad, dtype=F32)
    real = idx < c_real
    rel = idx[:, None] - idx[None, :]
    intra = jnp.where(rel >= 0, jnp.exp(lg[:, None, None] * jnp.maximum(rel, 0.0)), 0.0)
    intra = jnp.where(real[None, :, None] & real[None, None, :], intra, 0.0)
    q_dec = jnp.where(real[None, :], jnp.exp((idx[None, :] + 1.0) * lg[:, None]), 0.0)
    k_dec = jnp.where(real[None, :], jnp.exp((c_real - 1.0 - idx)[None, :] * lg[:, None]), 0.0)
    c_dec = jnp.exp(c_real * lg)
    bc = lambda t: jnp.broadcast_to(t[:, :, None], (RET_HEADS, c_pad, RET_DK))
    return intra, bc(q_dec), bc(k_dec), c_dec


def _rope_tables(pos):
    half = RET_DK // 2
    inv_freq = 1.0 / (ROPE_BASE ** (jnp.arange(half, dtype=F32) / half))
    ang = pos.astype(F32)[:, None] * inv_freq[None, :]
    cos, sin = jnp.cos(ang), jnp.sin(ang)
    return jnp.concatenate([cos, cos], -1), jnp.concatenate([-sin, sin], -1)


def _ret_kernel(cdec_ref, rq_ref, rk_ref, rv_ref, rg_ref, cos_ref, sin_ref, intra_ref, qdec_ref,
                kdec_ref, s0_ref, o_ref, s_out_ref, s_scr, *, n_steps, c_len):
    c = pl.program_id(1)

    @pl.when(c == 0)
    def _():
        s_scr[...] = s0_ref[...]

    def rot(x, cos2, sin2):
        return x * cos2 + pltpu.roll(x, RET_DK // 2, 1) * sin2

    for bi in range(rq_ref.shape[0]):
        for sc in range(rq_ref.shape[1] // c_len):
            rows = slice(sc * c_len, (sc + 1) * c_len)
            cos2, sin2 = cos_ref[rows, :], sin_ref[rows, :]
            for h in range(RET_HEADS):
                q = rot(rq_ref[bi, rows, h * RET_DK:(h + 1) * RET_DK], cos2, sin2)
                k = rot(rk_ref[bi, rows, h * RET_DK:(h + 1) * RET_DK], cos2, sin2) * (RET_DK ** -0.5)
                v = rv_ref[bi, rows, h * RET_DV:(h + 1) * RET_DV].astype(BF)
                s_old = s_scr[bi, h]
                a = _bdot_nt(q, k) * intra_ref[h]
                o = _bdot(a, v) + _bdot(q * qdec_ref[h], s_old)
                s_scr[bi, h] = s_old * cdec_ref[h] + _bdot_tn(k * kdec_ref[h], v)
                o = o * lax.rsqrt(jnp.mean(o * o, -1, keepdims=True) + NORM_EPS)
                g = rg_ref[bi, rows, h * RET_DV:(h + 1) * RET_DV]
                o_ref[bi, rows, h * RET_DV:(h + 1) * RET_DV] = o * (g * jax.nn.sigmoid(g))

    @pl.when(c == n_steps - 1)
    def _():
        s_out_ref[...] = s_scr[...]


def _retention(h3, pos, state0, c_real, c_pad, tb, cps):
    b, l, _ = h3.shape
    rows = cps * c_pad
    n_steps = l // rows
    intra, qdec, kdec, cdec = _ret_tables(c_real, c_pad)
    cos2, sin2 = _rope_tables(pos)
    full3 = lambda shape: pl.BlockSpec(shape, lambda i, c: (0, 0, 0))
    return pl.pallas_call(
        functools.partial(_ret_kernel, n_steps=n_steps, c_len=c_pad),
        grid=(b // tb, n_steps),
        in_specs=[
            pl.BlockSpec(memory_space=pltpu.SMEM),
            pl.BlockSpec((tb, rows, 512), lambda i, c: (i, c, COL_RQ // 512)),
            pl.BlockSpec((tb, rows, 512), lambda i, c: (i, c, COL_RK // 512)),
            pl.BlockSpec((tb, rows, 1024), lambda i, c: (i, c, COL_RV // 1024)),
            pl.BlockSpec((tb, rows, 1024), lambda i, c: (i, c, COL_RG // 1024)),
            pl.BlockSpec((rows, RET_DK), lambda i, c: (c, 0)),
            pl.BlockSpec((rows, RET_DK), lambda i, c: (c, 0)),
            full3((RET_HEADS, c_pad, c_pad)),
            full3((RET_HEADS, c_pad, RET_DK)),
            full3((RET_HEADS, c_pad, RET_DK)),
            pl.BlockSpec((tb, RET_HEADS, RET_DK, RET_DV), lambda i, c: (i, 0, 0, 0)),
        ],
        out_specs=[
            pl.BlockSpec((tb, rows, 1024), lambda i, c: (i, c, 0)),
            pl.BlockSpec((tb, RET_HEADS, RET_DK, RET_DV), lambda i, c: (i, 0, 0, 0)),
        ],
        out_shape=[jax.ShapeDtypeStruct((b, l, 1024), F32),
                   jax.ShapeDtypeStruct((b, RET_HEADS, RET_DK, RET_DV), F32)],
        scratch_shapes=[pltpu.VMEM((tb, RET_HEADS, RET_DK, RET_DV), F32)],
        compiler_params=_params("parallel", "arbitrary"),
        name="retention",
    )(cdec, h3, h3, h3, h3, cos2, sin2, intra, qdec, kdec, state0)


def _sink_softmax(s, sink):
    m = jnp.maximum(jnp.max(s, -1, keepdims=True), sink)
    p = jnp.exp(s - m)
    return p / (jnp.sum(p, -1, keepdims=True) + jnp.exp(sink - m))


def _swa_prompt_kernel(sinks_ref, q_ref, kp_ref, kc_ref, vp_ref, vc_ref, o_ref):
    n = pl.program_id(0)
    rows = SWA_GROUP * SWA_BLOCK
    qi = lax.broadcasted_iota(jnp.int32, (rows, 2 * SWA_BLOCK), 0) % SWA_BLOCK
    kj = lax.broadcasted_iota(jnp.int32, (rows, 2 * SWA_BLOCK), 1)
    rel = SWA_BLOCK + qi - kj
    valid = (rel >= 0) & (rel <= WINDOW) & ((kj >= SWA_BLOCK) | (n > 0))
    row_head = lax.broadcasted_iota(jnp.int32, (rows, 1), 0) // SWA_BLOCK
    for h in range(SWA_KV_HEADS):
        sl = slice(h * SWA_DH, (h + 1) * SWA_DH)
        k2 = jnp.concatenate([kp_ref[:, sl], kc_ref[:, sl]], 0).astype(BF)
        v2 = jnp.concatenate([vp_ref[:, sl], vc_ref[:, sl]], 0).astype(BF)
        heads = [h * SWA_GROUP + g for g in range(SWA_GROUP)]
        q = jnp.concatenate([q_ref[:, hq * SWA_DH:(hq + 1) * SWA_DH] for hq in heads], axis=0)
        sink = jnp.zeros((rows, 1), F32)
        for g, hq in enumerate(heads):
            sink = jnp.where(row_head == g, sinks_ref[hq], sink)
        s = _bdot_nt(q, k2) * (SWA_DH ** -0.5)
        s = jnp.where(valid, s, NEG)
        o = _bdot(_sink_softmax(s, sink), v2)
        for g, hq in enumerate(heads):
            o_ref[:, hq * SWA_DH:(hq + 1) * SWA_DH] = o[g * SWA_BLOCK:(g + 1) * SWA_BLOCK, :]


def _swa_prompt(h2, sinks):
    t = h2.shape[0]
    nb = t // SWA_BLOCK
    prev = lambda col: (lambda n: (jnp.maximum(n - 1, 0), col))
    cur = lambda col: (lambda n: (n, col))
    ck, cv = COL_SK // 256, COL_SV // 256
    return pl.pallas_call(
        _swa_prompt_kernel,
        grid=(nb,),
        in_specs=[
            pl.BlockSpec(memory_space=pltpu.SMEM),
            pl.BlockSpec((SWA_BLOCK, 1024), cur(COL_SQ // 1024)),
            pl.BlockSpec((SWA_BLOCK, 256), prev(ck)),
            pl.BlockSpec((SWA_BLOCK, 256), cur(ck)),
            pl.BlockSpec((SWA_BLOCK, 256), prev(cv)),
            pl.BlockSpec((SWA_BLOCK, 256), cur(cv)),
        ],
        out_specs=pl.BlockSpec((SWA_BLOCK, 1024), lambda n: (n, 0)),
        out_shape=jax.ShapeDtypeStruct((t, 1024), F32),
        compiler_params=_params("parallel"),
        name="swa_prompt",
    )(sinks, h2, h2, h2, h2, h2)


def _swa_sample_kernel(sinks_ref, q_ref, kn_ref, vn_ref, kp_ref, vp_ref, o_ref, *, n_new):
    tb = q_ref.shape[0]
    w = kp_ref.shape[1]
    p_ = SAMPLE_PAD
    rows = SWA_GROUP * p_
    qi = lax.broadcasted_iota(jnp.int32, (tb, rows, w), 1) % p_
    kj = lax.broadcasted_iota(jnp.int32, (tb, rows, w), 2)
    rel_prev = w + qi - kj
    valid_prev = (rel_prev >= 0) & (rel_prev <= WINDOW)
    qn = lax.broadcasted_iota(jnp.int32, (tb, rows, p_), 1) % p_
    kn = lax.broadcasted_iota(jnp.int32, (tb, rows, p_), 2)
    valid_new = (qn - kn >= 0) & (qn - kn <= WINDOW) & (kn < n_new)
    row_head = lax.broadcasted_iota(jnp.int32, (1, rows, 1), 1) // p_
    bdot = lambda eq, a, b: jnp.einsum(eq, a.astype(BF), b.astype(BF), preferred_element_type=F32)
    for h in range(SWA_KV_HEADS):
        sl = slice(h * SWA_DH, (h + 1) * SWA_DH)
        kp, vp = kp_ref[:, :, sl], vp_ref[:, :, sl]
        kn_h, vn_h = kn_ref[:, :, sl], vn_ref[:, :, sl]
        heads = [h * SWA_GROUP + g for g in range(SWA_GROUP)]
        q = jnp.concatenate([q_ref[:, :, hq * SWA_DH:(hq + 1) * SWA_DH] for hq in heads], axis=1)
        sink = jnp.zeros((1, rows, 1), F32)
        for g, hq in enumerate(heads):
            sink = jnp.where(row_head == g, sinks_ref[hq], sink)
        sp = bdot('bqd,bkd->bqk', q, kp) * (SWA_DH ** -0.5)
        sn = bdot('bqd,bkd->bqk', q, kn_h) * (SWA_DH ** -0.5)
        sp = jnp.where(valid_prev, sp, NEG)
        sn = jnp.where(valid_new, sn, NEG)
        m = jnp.maximum(jnp.maximum(jnp.max(sp, -1, keepdims=True),
                                    jnp.max(sn, -1, keepdims=True)), sink)
        pp = jnp.exp(sp - m)
        pn = jnp.exp(sn - m)
        den = jnp.sum(pp, -1, keepdims=True) + jnp.sum(pn, -1, keepdims=True) + jnp.exp(sink - m)
        o = bdot('bqk,bkd->bqd', pp / den, vp) + bdot('bqk,bkd->bqd', pn / den, vn_h)
        for g, hq in enumerate(heads):
            o_ref[:, :, hq * SWA_DH:(hq + 1) * SWA_DH] = o[:, g * p_:(g + 1) * p_, :]


def _swa_sample(h3, prev_k, prev_v, sinks, n_new, tb=8):
    b = h3.shape[0]
    w = prev_k.shape[1]
    return pl.pallas_call(
        functools.partial(_swa_sample_kernel, n_new=n_new),
        grid=(b // tb,),
        in_specs=[
            pl.BlockSpec(memory_space=pltpu.SMEM),
            pl.BlockSpec((tb, SAMPLE_PAD, 1024), lambda i: (i, 0, COL_SQ // 1024)),
            pl.BlockSpec((tb, SAMPLE_PAD, 256), lambda i: (i, 0, COL_SK // 256)),
            pl.BlockSpec((tb, SAMPLE_PAD, 256), lambda i: (i, 0, COL_SV // 256)),
            pl.BlockSpec((tb, w, 256), lambda i: (i, 0, 0)),
            pl.BlockSpec((tb, w, 256), lambda i: (i, 0, 0)),
        ],
        out_specs=pl.BlockSpec((tb, SAMPLE_PAD, 1024), lambda i: (i, 0, 0)),
        out_shape=jax.ShapeDtypeStruct((b, SAMPLE_PAD, 1024), F32),
        compiler_params=_params("parallel"),
        name="swa_sample",
    )(sinks, h3, h3, h3, prev_k, prev_v)


def _mem_head(q, k, v):
    s = _bdot_nt(q, k) * (MEM_DH ** -0.5)
    m = jnp.max(s, -1, keepdims=True)
    e = jnp.exp(s - m)
    return _bdot(e / jnp.sum(e, -1, keepdims=True), v)


def _mem_kernel(q_ref, mk_ref, mv_ref, o_ref):
    for h in range(MEM_HEADS):
        sl = slice(h * MEM_DH, (h + 1) * MEM_DH)
        o_ref[0, :, sl] = _mem_head(q_ref[0, :, sl], mk_ref[0, :, sl], mv_ref[0, :, sl])


def _mem_cached_kernel(q_ref, mk_hbm, mv_hbm, o_ref, kv_ref, sem_ref):
    b = pl.program_id(0)
    slot = b % 2

    def copies(seq, s):
        return [pltpu.make_async_copy(src.at[seq, :, h, :], kv_ref.at[s, j, h], sem_ref.at[s, j, h])
                for j, src in enumerate((mk_hbm, mv_hbm)) for h in range(MEM_HEADS)]

    @pl.when(b == 0)
    def _():
        for c in copies(0, 0):
            c.start()

    @pl.when(b + 1 < pl.num_programs(0))
    def _():
        for c in copies(b + 1, 1 - slot):
            c.start()

    for c in copies(b, slot):
        c.wait()
    for h in range(MEM_HEADS):
        sl = slice(h * MEM_DH, (h + 1) * MEM_DH)
        o_ref[0, :, sl] = _mem_head(q_ref[0, :, sl], kv_ref[slot, 0, h], kv_ref[slot, 1, h])


def _mem_attend_cached(h3, mk, mv):
    b, l, _ = h3.shape
    return pl.pallas_call(
        _mem_cached_kernel,
        grid=(b,),
        in_specs=[
            pl.BlockSpec((1, l, 1024), lambda i: (i, 0, COL_MQ // 1024)),
            pl.BlockSpec(memory_space=pl.ANY),
            pl.BlockSpec(memory_space=pl.ANY),
        ],
        out_specs=pl.BlockSpec((1, l, 1024), lambda i: (i, 0, 0)),
        out_shape=jax.ShapeDtypeStruct((b, l, 1024), F32),
        scratch_shapes=[pltpu.VMEM((2, 2, MEM_HEADS, N_MEM, MEM_DH), F32),
                        pltpu.SemaphoreType.DMA((2, 2, MEM_HEADS))],
        compiler_params=_params("arbitrary"),
        name="mem_attend_cached",
    )(h3, mk, mv)


def _mem_attend(h3, mk, mv, tl):
    b, l, _ = h3.shape
    return pl.pallas_call(
        _mem_kernel,
        grid=(b, l // tl),
        in_specs=[
            pl.BlockSpec((1, tl, 1024), lambda i, j: (i, j, COL_MQ // 1024)),
            pl.BlockSpec((1, N_MEM, 1024), lambda i, j: (i, 0, 0)),
            pl.BlockSpec((1, N_MEM, 1024), lambda i, j: (i, 0, 0)),
        ],
        out_specs=pl.BlockSpec((1, tl, 1024), lambda i, j: (i, j, 0)),
        out_shape=jax.ShapeDtypeStruct((b, l, 1024), F32),
        compiler_params=_params("parallel", "parallel"),
        name="mem_attend",
    )(h3, mk, mv)


def _merge_kernel(ro_ref, so_ref, mo_ref, g0_ref, g1_ref, g2_ref, x_ref, wb_ref, wo_ref, g_ref,
                  b_ref, o_ref, *, alpha):
    acc = None
    for n, (br, gl) in enumerate(((ro_ref, g0_ref), (so_ref, g1_ref), (mo_ref, g2_ref))):
        term = jax.nn.sigmoid(gl[...]) * jnp.dot(br[...].astype(BF), wb_ref[n],
                                                 preferred_element_type=F32)
        acc = term if acc is None else acc + term
    a = jnp.dot(acc.astype(BF), wo_ref[...], preferred_element_type=F32)
    _store_tiled_rows(o_ref, (), _layer_norm(alpha * x_ref[...] + a, g_ref[...], b_ref[...]))


def _merge(ro, so, mo, h2, x2, wb_bf, wo_bf, g, b, alpha, tm):
    t = x2.shape[0]
    tile = lambda col: pl.BlockSpec((tm, 1024), lambda i: (i, col))
    gl0 = COL_GL // 1024
    return pl.pallas_call(
        functools.partial(_merge_kernel, alpha=alpha),
        grid=(t // tm,),
        in_specs=[tile(0), tile(0), tile(0), tile(gl0), tile(gl0 + 1), tile(gl0 + 2), tile(0),
                  pl.BlockSpec((N_BRANCH, 1024, 1024), lambda i: (0, 0, 0)),
                  pl.BlockSpec((1024, 1024), lambda i: (0, 0)),
                  pl.BlockSpec((1, 1024), lambda i: (0, 0)),
                  pl.BlockSpec((1, 1024), lambda i: (0, 0))],
        out_specs=pl.BlockSpec((tm * ROW_SUB, 128), lambda i: (i, 0)),
        out_shape=jax.ShapeDtypeStruct((t * ROW_SUB, 128), F32),
        compiler_params=_params("parallel"),
        name="merge_ln1",
    )(ro, so, mo, h2, h2, h2, x2, wb_bf, wo_bf, g, b)


def _first_index_of_max(v, iota, big, axes):
    m = jnp.max(v, axis=axes, keepdims=True)
    idx = jnp.min(jnp.where(v == m, iota, big), axis=axes, keepdims=True)
    return m, idx


def _router_kernel(x_ref, wr_ref, bias_ref, eidx_ref, ew_ref):
    tt = x_ref.shape[0] // ROW_SUB
    x = _load_tiled_rows(x_ref, (), tt).astype(BF)
    logits = lax.dot_general(wr_ref[...], x, (((1,), (1,)), ((), ())),
                             preferred_element_type=F32)
    s = jax.nn.sigmoid(logits).reshape(N_GROUPS, GROUP_SIZE, tt)
    sb = s + bias_ref[...].reshape(N_GROUPS, GROUP_SIZE, 1)
    ninf = -jnp.inf
    r_iota = lax.broadcasted_iota(jnp.int32, sb.shape, 1)
    m1, i1 = _first_index_of_max(sb, r_iota, GROUP_SIZE, 1)
    m2 = jnp.max(jnp.where(r_iota == i1, ninf, sb), axis=1, keepdims=True)
    gsc = (m1 + m2).reshape(N_GROUPS, tt)
    g_iota = lax.broadcasted_iota(jnp.int32, gsc.shape, 0)
    gmask = jnp.zeros(gsc.shape, jnp.bool_)
    for _ in range(TOPK_GROUPS):
        _, gi = _first_index_of_max(gsc, g_iota, N_GROUPS, 0)
        hit = g_iota == gi
        gmask = gmask | hit
        gsc = jnp.where(hit, ninf, gsc)
    cand = jnp.where(gmask.reshape(N_GROUPS, 1, tt), sb, ninf)
    e_iota = lax.broadcasted_iota(jnp.int32, sb.shape, 0) * GROUP_SIZE + r_iota
    idxs, ws = [], []
    for _ in range(TOP_K):
        _, ei = _first_index_of_max(cand, e_iota, N_EXPERTS, (0, 1))
        hit = e_iota == ei
        idxs.append(ei.reshape(1, tt))
        ws.append(jnp.sum(jnp.where(hit, s, 0.0), axis=(0, 1)).reshape(1, tt))
        cand = jnp.where(hit, ninf, cand)
    w = jnp.concatenate(ws, 0)
    eidx_ref[...] = jnp.concatenate(idxs, 0)
    ew_ref[...] = w / jnp.sum(w, 0, keepdims=True) * ROUTED_SCALE


def _router(xt, wr_t_bf, bias_col, tt):
    t = xt.shape[0] // ROW_SUB
    return pl.pallas_call(
        _router_kernel,
        grid=(t // tt,),
        in_specs=[pl.BlockSpec((tt * ROW_SUB, 128), lambda i: (i, 0)),
                  pl.BlockSpec((N_EXPERTS, 1024), lambda i: (0, 0)),
                  pl.BlockSpec((N_EXPERTS, 1), lambda i: (0, 0))],
        out_specs=[pl.BlockSpec((TOP_K, tt), lambda i: (0, i)),
                   pl.BlockSpec((TOP_K, tt), lambda i: (0, i))],
        out_shape=[jax.ShapeDtypeStruct((TOP_K, t), jnp.int32),
                   jax.ShapeDtypeStruct((TOP_K, t), F32)],
        compiler_params=_params("parallel"),
        name="router",
    )(xt, wr_t_bf, bias_col)


IDX_BITS = 18


def _sorted_keys(eidx):
    n_grp = eidx.shape[0]
    flat_e = eidx.reshape(n_grp, -1)
    a = flat_e.shape[1]
    assert a < (1 << IDX_BITS) and n_grp * N_EXPERTS << IDX_BITS < 2 ** 31
    grp_exp = jnp.arange(n_grp, dtype=jnp.int32)[:, None] * N_EXPERTS + flat_e
    keys = grp_exp * (1 << IDX_BITS) + jnp.arange(a, dtype=jnp.int32)[None, :]
    return jnp.sort(keys.reshape(-1)).reshape(n_grp, a) & ((N_EXPERTS << IDX_BITS) - 1)


def _lookup(table, idx):
    hit = idx[..., None] == jnp.arange(table.shape[0], dtype=idx.dtype)
    return jnp.sum(jnp.where(hit, table, 0), axis=-1)


def _dispatch(eidx, ew, skey, t):
    a = t * TOP_K
    nblk = -(-a // MOE_BLOCK) + N_EXPERTS
    assert nblk % STEP_BLOCKS == 0 and STEP_BLOCKS % 2 == 0
    flat_e = eidx.reshape(-1)
    si = skey & ((1 << IDX_BITS) - 1)
    experts = jnp.arange(N_EXPERTS, dtype=jnp.int32)
    counts = jnp.sum((flat_e[None, :] == experts[:, None]).astype(jnp.int32), axis=1)
    grp_start = jnp.cumsum(counts) - counts
    padded = (counts + MOE_BLOCK - 1) // MOE_BLOCK * MOE_BLOCK
    pad_end = jnp.cumsum(padded)
    pad_start = pad_end - padded
    blk_first = jnp.arange(nblk, dtype=jnp.int32) * MOE_BLOCK
    blk_exp = jnp.minimum(jnp.sum((pad_end[None, :] <= blk_first[:, None]).astype(jnp.int32), axis=1),
                          N_EXPERTS - 1)
    off = (jnp.arange(nblk * MOE_BLOCK, dtype=jnp.int32).reshape(nblk, MOE_BLOCK)
           - _lookup(pad_start, blk_exp)[:, None])
    valid = off < _lookup(counts, blk_exp)[:, None]
    src = jnp.clip(_lookup(grp_start, blk_exp)[:, None] + off, 0, a - 1)
    row_si = si[src]
    row_tok = jnp.where(valid, row_si >> 3, t).astype(jnp.int32)
    row_w = jnp.where(valid, ew.reshape(-1)[row_si], 0.0)
    nused = (pad_end[-1] // MOE_BLOCK).astype(jnp.int32).reshape(1)
    gidx, nxt = _group_tables(counts, blk_exp)
    step_rows = STEP_BLOCKS * MOE_BLOCK
    return (row_tok.reshape(-1), row_w.reshape(nblk // STEP_BLOCKS, 1, step_rows), blk_exp, nused,
            gidx, nxt)


def _group_tables(counts, blk_exp):
    nonempty = counts > 0
    gidx = _lookup(jnp.cumsum(nonempty.astype(jnp.int32)) - 1, blk_exp)
    experts = jnp.arange(N_EXPERTS, dtype=jnp.int32)
    cand = jnp.where(nonempty, experts, N_EXPERTS)
    later = lax.cummin(cand, axis=0, reverse=True)
    nxt = jnp.concatenate([later[1:], jnp.full((2,), N_EXPERTS, jnp.int32)])
    nxt1 = _lookup(nxt, blk_exp)
    nxt2 = _lookup(nxt, jnp.minimum(nxt1, N_EXPERTS))
    none = lambda a: jnp.where(a >= N_EXPERTS, -1, a).astype(jnp.int32)
    return gidx.astype(jnp.int32), jnp.concatenate([none(nxt1), none(nxt2)])


WEIGHT_SLOTS = 3


def _weight_copies(hbm_refs, buf_ref, sem_ref, e, slot):
    return [pltpu.make_async_copy(h.at[e], buf_ref.at[slot, k], sem_ref.at[slot, k])
            for k, h in enumerate(hbm_refs)]


def _stage_weights(b, be_ref, nused_ref, gidx_ref, nxt_ref, hbm_refs, buf_ref, sem_ref, cache_refs):
    first = ((b == 0) | (be_ref[b] != be_ref[jnp.maximum(b - 1, 0)])) & (b < nused_ref[0])
    nblk = be_ref.shape[0]

    def start(e, slot):
        for c in _weight_copies(hbm_refs, buf_ref, sem_ref, e, slot):
            c.start()

    @pl.when(first)
    def _():
        slot = gidx_ref[b] % WEIGHT_SLOTS

        @pl.when(b == 0)
        def _():
            start(be_ref[0], 0)

            @pl.when(nxt_ref[0] >= 0)
            def _():
                start(nxt_ref[0], 1)

        for c in _weight_copies(hbm_refs, buf_ref, sem_ref, be_ref[b], slot):
            c.wait()
        nxt2 = nxt_ref[nblk + b]

        @pl.when(nxt2 >= 0)
        def _():
            start(nxt2, (slot + 2) % WEIGHT_SLOTS)

        for k, cache in enumerate(cache_refs):
            cache[...] = buf_ref[slot, k].astype(BF)


def _gather_rows(tok_ref, b, x_ref, xg_ref, slot):
    for r in range(MOE_BLOCK):
        xg_ref[slot, r * ROW_SUB:(r + 1) * ROW_SUB, :] = x_ref[tok_ref[b * MOE_BLOCK + r]]


def _up_kernel(be_ref, nused_ref, gidx_ref, nxt_ref, tok_ref, x_ref, wg_hbm, wu_hbm, act_ref,
               xg_ref, wbuf_ref, wgb_ref, wub_ref, sem_ref):
    i = pl.program_id(0)
    nblk = pl.num_programs(0) * STEP_BLOCKS
    b0 = i * STEP_BLOCKS

    @pl.when(i == 0)
    def _():
        _gather_rows(tok_ref, 0, x_ref, xg_ref, 0)

    @pl.when(b0 < nused_ref[0])
    def _():
        for j in range(STEP_BLOCKS):
            b = b0 + j
            _stage_weights(b, be_ref, nused_ref, gidx_ref, nxt_ref, (wg_hbm, wu_hbm), wbuf_ref,
                           sem_ref, (wgb_ref, wub_ref))
            _gather_rows(tok_ref, jnp.minimum(b + 1, nblk - 1), x_ref, xg_ref, (j + 1) % 2)
            x = _load_tiled_rows(xg_ref, (j % 2,), MOE_BLOCK).astype(BF)
            g = jnp.dot(x, wgb_ref[...], preferred_element_type=F32)
            u = jnp.dot(x, wub_ref[...], preferred_element_type=F32)
            act_ref[j * MOE_BLOCK:(j + 1) * MOE_BLOCK, :] = ((g * jax.nn.sigmoid(g)) * u).astype(BF)

    @pl.when(b0 >= nused_ref[0])
    def _():
        act_ref[...] = jnp.zeros_like(act_ref)


def _experts_up(x, row_tok, blk_exp, nused, gidx, nxt, w_gate, w_up):
    nblk = blk_exp.shape[0]
    step_rows = STEP_BLOCKS * MOE_BLOCK
    return pl.pallas_call(
        _up_kernel,
        grid_spec=pltpu.PrefetchScalarGridSpec(
            num_scalar_prefetch=5,
            grid=(nblk // STEP_BLOCKS,),
            in_specs=[pl.BlockSpec(memory_space=pltpu.VMEM),
                      pl.BlockSpec(memory_space=pl.ANY),
                      pl.BlockSpec(memory_space=pl.ANY)],
            out_specs=pl.BlockSpec((step_rows, D_EXPERT), lambda i, *_: (i, 0)),
            scratch_shapes=[pltpu.VMEM((2, MOE_BLOCK * ROW_SUB, 128), F32),
                            pltpu.VMEM((WEIGHT_SLOTS, 2, D_MODEL, D_EXPERT), F32),
                            pltpu.VMEM((D_MODEL, D_EXPERT), BF),
                            pltpu.VMEM((D_MODEL, D_EXPERT), BF),
                            pltpu.SemaphoreType.DMA((WEIGHT_SLOTS, 2))],
        ),
        out_shape=jax.ShapeDtypeStruct((nblk * MOE_BLOCK, D_EXPERT), BF),
        compiler_params=_params("arbitrary"),
        name="experts_up",
    )(blk_exp, nused, gidx, nxt, row_tok, x, w_gate, w_up)


SCATTER_GROUP = 8


def _row_to_column(row):
    n = row.shape[1]
    eye = lax.broadcasted_iota(jnp.int32, (n, n), 0) == lax.broadcasted_iota(jnp.int32, (n, n), 1)
    return jnp.sum(jnp.where(eye, jnp.broadcast_to(row, (n, n)), 0.0), axis=1, keepdims=True)


def _scatter_add_rows(tok_ref, b, y_ref, yb_ref, slot):
    for r0 in range(0, MOE_BLOCK, SCATTER_GROUP):
        rs = range(r0, r0 + SCATTER_GROUP)
        toks = [tok_ref[b * MOE_BLOCK + r] for r in rs]
        new = [y_ref[t] + yb_ref[slot, r * ROW_SUB:(r + 1) * ROW_SUB, :] for r, t in zip(rs, toks)]
        for t, v in zip(toks, new):
            y_ref[t] = v


def _down_kernel(be_ref, nused_ref, gidx_ref, nxt_ref, tok_ref, act_ref, rw_ref, wd_hbm, y_ref,
                 yb_ref, wbuf_ref, wdb_ref, sem_ref):
    i = pl.program_id(0)
    nblk = pl.num_programs(0) * STEP_BLOCKS
    b0 = i * STEP_BLOCKS

    @pl.when(i == 0)
    def _():
        y_ref[...] = jnp.zeros_like(y_ref)
        yb_ref[1] = jnp.zeros(yb_ref.shape[1:], F32)

    @pl.when(b0 <= nused_ref[0])
    def _():
        for j in range(STEP_BLOCKS):
            b = b0 + j
            rows = slice(j * MOE_BLOCK, (j + 1) * MOE_BLOCK)
            _stage_weights(b, be_ref, nused_ref, gidx_ref, nxt_ref, (wd_hbm,), wbuf_ref, sem_ref,
                           (wdb_ref,))
            yb = jnp.dot(act_ref[rows, :], wdb_ref[...],
                         preferred_element_type=F32) * _row_to_column(rw_ref[0, :, rows])
            _store_tiled_rows(yb_ref, (j % 2,), yb)
            _scatter_add_rows(tok_ref, jnp.maximum(b - 1, 0), y_ref, yb_ref, (j + 1) % 2)

    @pl.when((i == pl.num_programs(0) - 1) & (nused_ref[0] >= nblk))
    def _():
        _scatter_add_rows(tok_ref, nblk - 1, y_ref, yb_ref, (STEP_BLOCKS - 1) % 2)


def _experts_down(act, row_tok, row_w, blk_exp, nused, gidx, nxt, w_down, t_rows):
    nblk = blk_exp.shape[0]
    step_rows = STEP_BLOCKS * MOE_BLOCK
    return pl.pallas_call(
        _down_kernel,
        grid_spec=pltpu.PrefetchScalarGridSpec(
            num_scalar_prefetch=5,
            grid=(nblk // STEP_BLOCKS,),
            in_specs=[pl.BlockSpec((step_rows, D_EXPERT), lambda i, *_: (i, 0)),
                      pl.BlockSpec((1, 1, step_rows), lambda i, *_: (i, 0, 0)),
                      pl.BlockSpec(memory_space=pl.ANY)],
            out_specs=pl.BlockSpec(memory_space=pltpu.VMEM),
            scratch_shapes=[pltpu.VMEM((2, MOE_BLOCK * ROW_SUB, 128), F32),
                            pltpu.VMEM((WEIGHT_SLOTS, 1, D_EXPERT, D_MODEL), F32),
                            pltpu.VMEM((D_EXPERT, D_MODEL), BF),
                            pltpu.SemaphoreType.DMA((WEIGHT_SLOTS, 1))],
        ),
        out_shape=jax.ShapeDtypeStruct((t_rows, ROW_SUB, 128), F32),
        compiler_params=_params("arbitrary"),
        name="experts_down",
    )(blk_exp, nused, gidx, nxt, row_tok, act, row_w, w_down)


def _final_kernel(h_ref, yr_ref, wg_ref, wu_ref, wd_ref, g_ref, b_ref, o_ref, *, alpha):
    h = _load_tiled_rows(h_ref, (), o_ref.shape[0])
    hb = h.astype(BF)
    g = jnp.dot(hb, wg_ref[...], preferred_element_type=F32)
    u = jnp.dot(hb, wu_ref[...], preferred_element_type=F32)
    shared = jnp.dot(((g * jax.nn.sigmoid(g)) * u).astype(BF), wd_ref[...],
                     preferred_element_type=F32)
    f = _load_tiled_rows(yr_ref, (), h.shape[0]) + shared
    o_ref[...] = _layer_norm(alpha * h + f, g_ref[...], b_ref[...])


def _final(ht, yr, wg_bf, wu_bf, wd_bf, g, b, alpha, tm):
    t = ht.shape[0] // ROW_SUB
    tiled = pl.BlockSpec((tm * ROW_SUB, 128), lambda i: (i, 0))
    full = lambda shape: pl.BlockSpec(shape, lambda i: (0, 0))
    return pl.pallas_call(
        functools.partial(_final_kernel, alpha=alpha),
        grid=(t // tm,),
        in_specs=[tiled, tiled, full((1024, 256)), full((1024, 256)), full((256, 1024)),
                  full((1, 1024)), full((1, 1024))],
        out_specs=pl.BlockSpec((tm, 1024), lambda i: (i, 0)),
        out_shape=jax.ShapeDtypeStruct((t, 1024), F32),
        compiler_params=_params("parallel"),
        name="shared_ln2",
    )(ht, yr, wg_bf, wu_bf, wd_bf, g, b)


def _moe_ln2(ht, tables, w_gate, w_up, w_down, wsg_bf, wsu_bf, wsd_bf, g, b, alpha):
    t = ht.shape[0] // ROW_SUB
    row_tok, row_w, blk_exp, nused, gidx, nxt = tables
    act = _experts_up(ht.reshape(t, ROW_SUB, 128), jnp.minimum(row_tok, t - 1), blk_exp, nused, gidx,
                      nxt, w_gate, w_up)
    yr = _experts_down(act, row_tok, row_w, blk_exp, nused, gidx, nxt, w_down, t + 8)
    return _final(ht, yr.reshape((t + 8) * ROW_SUB, 128), wsg_bf, wsu_bf, wsd_bf, g, b, alpha, 256)


def _permute_w_in(w_in):
    rq, rk, rv, rg, sq, sk, sv, mq, gl = jnp.split(
        w_in, [512, 1024, 2048, 3072, 4096, 4352, 4608, 5632], axis=-1)
    return jnp.concatenate([rq, rk, rv, rg, sq, mq, gl, sk, sv], -1)


def kernel(x_prompt, x_sample, mem_prompt, cache_ret_state, cache_swa_k, cache_swa_v, cache_mem_k,
           cache_mem_v, w_in, swa_sinks, w_mem_kv, w_branch, w_o, ln1_g, ln1_b, w_router,
           router_bias, w_gate_e, w_up_e, w_down_e, w_sh_gate, w_sh_up, w_sh_down, ln2_g, ln2_b):
    depth = w_in.shape[0]
    assert depth == 1
    alpha = (2.0 * depth) ** 0.25
    bp, lp, d = x_prompt.shape
    bs, ls, _ = x_sample.shape
    l = 0

    w_in_bf = _permute_w_in(w_in[l]).astype(BF)
    sinks = swa_sinks[l]
    wb_bf = w_branch[l].astype(BF)
    wo_bf = w_o[l].astype(BF)
    g1, b1 = ln1_g[l].reshape(1, d), ln1_b[l].reshape(1, d)
    g2, b2 = ln2_g[l].reshape(1, d), ln2_b[l].reshape(1, d)

    tp = bp * lp
    xp2 = x_prompt.reshape(tp, d)
    hp2 = _proj(xp2, w_in_bf, 1024, PROJ_TN)
    hp3 = hp2.reshape(bp, lp, D_IN)
    mkv = _proj(mem_prompt.reshape(bp * N_MEM, d), w_mem_kv[l].astype(BF), N_MEM, 512)
    mk_p, mv_p = mkv[:, :1024].reshape(bp, N_MEM, 1024), mkv[:, 1024:].reshape(bp, N_MEM, 1024)
    rs0 = jnp.zeros((bp, RET_HEADS, RET_DK, RET_DV), F32)
    ro_p, rs_p = _retention(hp3, jnp.arange(lp), rs0, RET_CHUNK, RET_CHUNK, 1, 4)
    so_p = _swa_prompt(hp2, sinks)
    mo_p = _mem_attend(hp3, mk_p, mv_p, 256)
    hmid_p = _merge(ro_p.reshape(tp, d), so_p, mo_p.reshape(tp, d), hp2, xp2, wb_bf, wo_bf,
                    g1, b1, alpha, 256)

    ts = bs * SAMPLE_PAD
    xs3 = jnp.pad(x_sample, ((0, 0), (0, SAMPLE_PAD - ls), (0, 0)))
    xs2 = xs3.reshape(ts, d)
    hs2 = _proj(xs2, w_in_bf, ts, PROJ_TN)
    hs3 = hs2.reshape(bs, SAMPLE_PAD, D_IN)
    pos_s = PAST_LEN + jnp.arange(SAMPLE_PAD)
    ro_s, rs_s = _retention(hs3, pos_s, cache_ret_state.reshape(bs, RET_HEADS, RET_DK, RET_DV), ls,
                            SAMPLE_PAD, 4, 1)
    w_buf = cache_swa_k.shape[2]
    prev_k = cache_swa_k.reshape(bs, w_buf, SWA_KV_HEADS * SWA_DH)
    prev_v = cache_swa_v.reshape(bs, w_buf, SWA_KV_HEADS * SWA_DH)
    so_s = _swa_sample(hs3, prev_k, prev_v, sinks, ls)
    mo_s = _mem_attend_cached(hs3, cache_mem_k.reshape(bs, N_MEM, MEM_HEADS, MEM_DH),
                              cache_mem_v.reshape(bs, N_MEM, MEM_HEADS, MEM_DH))
    hmid_s = _merge(ro_s.reshape(ts, d), so_s.reshape(ts, d), mo_s.reshape(ts, d), hs2, xs2,
                    wb_bf, wo_bf, g1, b1, alpha, 256)
    hmid_s = hmid_s.reshape(bs, SAMPLE_PAD, ROW_SUB, 128)[:, :ls].reshape(bs * ls * ROW_SUB, 128)

    t_all = tp + bs * ls
    assert t_all % (2 * ROUTER_TILE) == 0 and t_all // 2 <= tp
    th = t_all // 2
    groups = (hmid_p[:th * ROW_SUB], jnp.concatenate([hmid_p[th * ROW_SUB:], hmid_s], 0))
    wr_t_bf, bias_col = w_router[l].T.astype(BF), router_bias[l].reshape(N_EXPERTS, 1)
    routed = [_router(ht, wr_t_bf, bias_col, ROUTER_TILE) for ht in groups]
    eidx = jnp.stack([e.T for e, _ in routed])
    tables = jax.vmap(functools.partial(_dispatch, t=th))(
        eidx, jnp.stack([w.T for _, w in routed]), _sorted_keys(eidx))
    y_a, y_b = (
        _moe_ln2(ht, jax.tree.map(lambda a: a[i], tables), w_gate_e[l], w_up_e[l], w_down_e[l],
                 w_sh_gate[l].astype(BF), w_sh_up[l].astype(BF), w_sh_down[l].astype(BF), g2, b2, alpha)
        for i, ht in enumerate(groups))

    y_p = jnp.concatenate([y_a, y_b[:tp - th]], 0).reshape(bp, lp, d)
    y_s = y_b[tp - th:].reshape(bs, ls, d)
    kv4 = lambda a, n: a.reshape(1, a.shape[0], n, SWA_KV_HEADS, SWA_DH)
    k_p = kv4(hp3[:, lp - WINDOW:, COL_SK:COL_SK + 256], WINDOW)
    v_p = kv4(hp3[:, lp - WINDOW:, COL_SV:COL_SV + 256], WINDOW)
    mem4 = lambda a: a.reshape(1, bp, N_MEM, MEM_HEADS, MEM_DH)
    k_s = kv4(jnp.concatenate([prev_k, hs3[:, :ls, COL_SK:COL_SK + 256]], 1)[:, -w_buf:], w_buf)
    v_s = kv4(jnp.concatenate([prev_v, hs3[:, :ls, COL_SV:COL_SV + 256]], 1)[:, -w_buf:], w_buf)
    return (y_p, y_s, rs_p[None], k_p, v_p, mem4(mk_p), mem4(mv_p), rs_s[None], k_s, v_s)
```

```python
import functools

import jax
import jax.numpy as jnp
from jax import lax
from jax.experimental import pallas as pl
from jax.experimental.pallas import tpu as pltpu

BF = jnp.bfloat16
F32 = jnp.float32

D_MODEL = 1024
RET_HEADS = 4
RET_DK = 128
RET_DV = 256
RET_CHUNK = 128
ROPE_BASE = 10000.0
SWA_HEADS = 16
SWA_KV_HEADS = 4
SWA_GROUP = SWA_HEADS // SWA_KV_HEADS
SWA_DH = 64
WINDOW = 128
SWA_BLOCK = 128
N_MEM = 256
MEM_HEADS = 4
MEM_DH = 256
N_BRANCH = 3
N_EXPERTS = 256
TOP_K = 8
N_GROUPS = 8
GROUP_SIZE = N_EXPERTS // N_GROUPS
TOPK_GROUPS = 4
D_EXPERT = 256
ROUTED_SCALE = 2.5
MOE_BLOCK = 128
ROUTER_TILE = 384
ROW_SUB = D_MODEL // 128
STEP_BLOCKS = 8
LN_EPS = 1e-5
NORM_EPS = 1e-6
NEG = -1e30
PAST_LEN = 16384
SAMPLE_PAD = 8

COL_RQ, COL_RK, COL_RV, COL_RG, COL_SQ, COL_MQ, COL_GL, COL_SK, COL_SV = (
    0, 512, 1024, 2048, 3072, 4096, 5120, 8192, 8448)
D_IN = 8704
PROJ_TN = D_IN // 4

VMEM_LIMIT = 56 * 1024 * 1024


def _params(*sem):
    return pltpu.CompilerParams(dimension_semantics=sem, vmem_limit_bytes=VMEM_LIMIT)


def _bdot(a, b):
    return jnp.dot(a.astype(BF), b.astype(BF), preferred_element_type=F32)


def _bdot_nt(a, b):
    return lax.dot_general(a.astype(BF), b.astype(BF), (((1,), (1,)), ((), ())),
                           preferred_element_type=F32)


def _bdot_tn(a, b):
    return lax.dot_general(a.astype(BF), b.astype(BF), (((0,), (0,)), ((), ())),
                           preferred_element_type=F32)


def _layer_norm(z, g, b):
    zc = z - jnp.mean(z, -1, keepdims=True)
    var = jnp.mean(zc * zc, -1, keepdims=True)
    return zc * lax.rsqrt(var + LN_EPS) * g + b


def _load_tiled_rows(ref, idx, m):
    return jnp.concatenate([ref[(*idx, pl.ds(s, m, stride=ROW_SUB), slice(None))]
                            for s in range(ROW_SUB)], axis=1)


def _store_tiled_rows(ref, idx, val):
    m = val.shape[0]
    for s in range(ROW_SUB):
        ref[(*idx, pl.ds(s, m, stride=ROW_SUB), slice(None))] = val[:, s * 128:(s + 1) * 128]


def _proj_kernel(x_ref, w_ref, o_ref, xb_ref):
    @pl.when(pl.program_id(1) == 0)
    def _():
        xb_ref[...] = x_ref[...].astype(BF)

    o_ref[...] = jnp.dot(xb_ref[...], w_ref[...], preferred_element_type=F32)


def _proj(x, w_bf, tm, tn):
    m, k = x.shape
    n = w_bf.shape[1]
    return pl.pallas_call(
        _proj_kernel,
        grid=(m // tm, n // tn),
        in_specs=[pl.BlockSpec((tm, k), lambda i, j: (i, 0)),
                  pl.BlockSpec((k, tn), lambda i, j: (0, j))],
        out_specs=pl.BlockSpec((tm, tn), lambda i, j: (i, j)),
        out_shape=jax.ShapeDtypeStruct((m, n), F32),
        scratch_shapes=[pltpu.VMEM((tm, k), BF)],
        compiler_params=_params("parallel", "arbitrary"),
        name="proj",
    )(x, w_bf)


def _ret_tables(c_real, c_pad):
    lg = jnp.log1p(-jnp.exp2(-5.0 - jnp.arange(RET_HEADS, dtype=F32)))
    idx = jnp.arange(c_pad, dtype=F32)
    real = idx < c_real
    rel = idx[:, None] - idx[None, :]
    intra = jnp.where(rel >= 0, jnp.exp(lg[:, None, None] * jnp.maximum(rel, 0.0)), 0.0)
    intra = jnp.where(real[None, :, None] & real[None, None, :], intra, 0.0)
    q_dec = jnp.where(real[None, :], jnp.exp((idx[None, :] + 1.0) * lg[:, None]), 0.0)
    k_dec = jnp.where(real[None, :], jnp.exp((c_real - 1.0 - idx)[None, :] * lg[:, None]), 0.0)
    c_dec = jnp.exp(c_real * lg)
    bc = lambda t: jnp.broadcast_to(t[:, :, None], (RET_HEADS, c_pad, RET_DK))
    return intra, bc(q_dec), bc(k_dec), c_dec


def _rope_tables(pos):
    half = RET_DK // 2
    inv_freq = 1.0 / (ROPE_BASE ** (jnp.arange(half, dtype=F32) / half))
    ang = pos.astype(F32)[:, None] * inv_freq[None, :]
    cos, sin = jnp.cos(ang), jnp.sin(ang)
    return jnp.concatenate([cos, cos], -1), jnp.concatenate([-sin, sin], -1)


def _ret_kernel(cdec_ref, rq_ref, rk_ref, rv_ref, rg_ref, cos_ref, sin_ref, intra_ref, qdec_ref,
                kdec_ref, s0_ref, o_ref, s_out_ref, s_scr, *, n_steps, c_len):
    c = pl.program_id(1)

    @pl.when(c == 0)
    def _():
        s_scr[...] = s0_ref[...]

    def rot(x, cos2, sin2):
        return x * cos2 + pltpu.roll(x, RET_DK // 2, 1) * sin2

    for bi in range(rq_ref.shape[0]):
        for sc in range(rq_ref.shape[1] // c_len):
            rows = slice(sc * c_len, (sc + 1) * c_len)
            cos2, sin2 = cos_ref[rows, :], sin_ref[rows, :]
            for h in range(RET_HEADS):
                q = rot(rq_ref[bi, rows, h * RET_DK:(h + 1) * RET_DK], cos2, sin2)
                k = rot(rk_ref[bi, rows, h * RET_DK:(h + 1) * RET_DK], cos2, sin2) * (RET_DK ** -0.5)
                v = rv_ref[bi, rows, h * RET_DV:(h + 1) * RET_DV].astype(BF)
                s_old = s_scr[bi, h]
                a = _bdot_nt(q, k) * intra_ref[h]
                o = _bdot(a, v) + _bdot(q * qdec_ref[h], s_old)
                s_scr[bi, h] = s_old * cdec_ref[h] + _bdot_tn(k * kdec_ref[h], v)
                o = o * lax.rsqrt(jnp.mean(o * o, -1, keepdims=True) + NORM_EPS)
                g = rg_ref[bi, rows, h * RET_DV:(h + 1) * RET_DV]
                o_ref[bi, rows, h * RET_DV:(h + 1) * RET_DV] = o * (g * jax.nn.sigmoid(g))

    @pl.when(c == n_steps - 1)
    def _():
        s_out_ref[...] = s_scr[...]


def _retention(h3, pos, state0, c_real, c_pad, tb, cps):
    b, l, _ = h3.shape
    rows = cps * c_pad
    n_steps = l // rows
    intra, qdec, kdec, cdec = _ret_tables(c_real, c_pad)
    cos2, sin2 = _rope_tables(pos)
    full3 = lambda shape: pl.BlockSpec(shape, lambda i, c: (0, 0, 0))
    return pl.pallas_call(
        functools.partial(_ret_kernel, n_steps=n_steps, c_len=c_pad),
        grid=(b // tb, n_steps),
        in_specs=[
            pl.BlockSpec(memory_space=pltpu.SMEM),
            pl.BlockSpec((tb, rows, 512), lambda i, c: (i, c, COL_RQ // 512)),
            pl.BlockSpec((tb, rows, 512), lambda i, c: (i, c, COL_RK // 512)),
            pl.BlockSpec((tb, rows, 1024), lambda i, c: (i, c, COL_RV // 1024)),
            pl.BlockSpec((tb, rows, 1024), lambda i, c: (i, c, COL_RG // 1024)),
            pl.BlockSpec((rows, RET_DK), lambda i, c: (c, 0)),
            pl.BlockSpec((rows, RET_DK), lambda i, c: (c, 0)),
            full3((RET_HEADS, c_pad, c_pad)),
            full3((RET_HEADS, c_pad, RET_DK)),
            full3((RET_HEADS, c_pad, RET_DK)),
            pl.BlockSpec((tb, RET_HEADS, RET_DK, RET_DV), lambda i, c: (i, 0, 0, 0)),
        ],
        out_specs=[
            pl.BlockSpec((tb, rows, 1024), lambda i, c: (i, c, 0)),
            pl.BlockSpec((tb, RET_HEADS, RET_DK, RET_DV), lambda i, c: (i, 0, 0, 0)),
        ],
        out_shape=[jax.ShapeDtypeStruct((b, l, 1024), F32),
                   jax.ShapeDtypeStruct((b, RET_HEADS, RET_DK, RET_DV), F32)],
        scratch_shapes=[pltpu.VMEM((tb, RET_HEADS, RET_DK, RET_DV), F32)],
        compiler_params=_params("parallel", "arbitrary"),
        name="retention",
    )(cdec, h3, h3, h3, h3, cos2, sin2, intra, qdec, kdec, state0)


def _sink_softmax(s, sink):
    m = jnp.maximum(jnp.max(s, -1, keepdims=True), sink)
    p = jnp.exp(s - m)
    return p / (jnp.sum(p, -1, keepdims=True) + jnp.exp(sink - m))


def _swa_prompt_kernel(sinks_ref, q_ref, kp_ref, kc_ref, vp_ref, vc_ref, o_ref):
    n = pl.program_id(0)
    rows = SWA_GROUP * SWA_BLOCK
    qi = lax.broadcasted_iota(jnp.int32, (rows, 2 * SWA_BLOCK), 0) % SWA_BLOCK
    kj = lax.broadcasted_iota(jnp.int32, (rows, 2 * SWA_BLOCK), 1)
    rel = SWA_BLOCK + qi - kj
    valid = (rel >= 0) & (rel <= WINDOW) & ((kj >= SWA_BLOCK) | (n > 0))
    row_head = lax.broadcasted_iota(jnp.int32, (rows, 1), 0) // SWA_BLOCK
    for h in range(SWA_KV_HEADS):
        sl = slice(h * SWA_DH, (h + 1) * SWA_DH)
        k2 = jnp.concatenate([kp_ref[:, sl], kc_ref[:, sl]], 0).astype(BF)
        v2 = jnp.concatenate([vp_ref[:, sl], vc_ref[:, sl]], 0).astype(BF)
        heads = [h * SWA_GROUP + g for g in range(SWA_GROUP)]
        q = jnp.concatenate([q_ref[:, hq * SWA_DH:(hq + 1) * SWA_DH] for hq in heads], axis=0)
        sink = jnp.zeros((rows, 1), F32)
        for g, hq in enumerate(heads):
            sink = jnp.where(row_head == g, sinks_ref[hq], sink)
        s = _bdot_nt(q, k2) * (SWA_DH ** -0.5)
        s = jnp.where(valid, s, NEG)
        o = _bdot(_sink_softmax(s, sink), v2)
        for g, hq in enumerate(heads):
            o_ref[:, hq * SWA_DH:(hq + 1) * SWA_DH] = o[g * SWA_BLOCK:(g + 1) * SWA_BLOCK, :]


def _swa_prompt(h2, sinks):
    t = h2.shape[0]
    nb = t // SWA_BLOCK
    prev = lambda col: (lambda n: (jnp.maximum(n - 1, 0), col))
    cur = lambda col: (lambda n: (n, col))
    ck, cv = COL_SK // 256, COL_SV // 256
    return pl.pallas_call(
        _swa_prompt_kernel,
        grid=(nb,),
        in_specs=[
            pl.BlockSpec(memory_space=pltpu.SMEM),
            pl.BlockSpec((SWA_BLOCK, 1024), cur(COL_SQ // 1024)),
            pl.BlockSpec((SWA_BLOCK, 256), prev(ck)),
            pl.BlockSpec((SWA_BLOCK, 256), cur(ck)),
            pl.BlockSpec((SWA_BLOCK, 256), prev(cv)),
            pl.BlockSpec((SWA_BLOCK, 256), cur(cv)),
        ],
        out_specs=pl.BlockSpec((SWA_BLOCK, 1024), lambda n: (n, 0)),
        out_shape=jax.ShapeDtypeStruct((t, 1024), F32),
        compiler_params=_params("parallel"),
        name="swa_prompt",
    )(sinks, h2, h2, h2, h2, h2)


def _swa_sample_kernel(sinks_ref, q_ref, kn_ref, vn_ref, kp_ref, vp_ref, o_ref, *, n_new):
    tb = q_ref.shape[0]
    w = kp_ref.shape[1]
    p_ = SAMPLE_PAD
    rows = SWA_GROUP * p_
    qi = lax.broadcasted_iota(jnp.int32, (tb, rows, w), 1) % p_
    kj = lax.broadcasted_iota(jnp.int32, (tb, rows, w), 2)
    rel_prev = w + qi - kj
    valid_prev = (rel_prev >= 0) & (rel_prev <= WINDOW)
    qn = lax.broadcasted_iota(jnp.int32, (tb, rows, p_), 1) % p_
    kn = lax.broadcasted_iota(jnp.int32, (tb, rows, p_), 2)
    valid_new = (qn - kn >= 0) & (qn - kn <= WINDOW) & (kn < n_new)
    row_head = lax.broadcasted_iota(jnp.int32, (1, rows, 1), 1) // p_
    bdot = lambda eq, a, b: jnp.einsum(eq, a.astype(BF), b.astype(BF), preferred_element_type=F32)
    for h in range(SWA_KV_HEADS):
        sl = slice(h * SWA_DH, (h + 1) * SWA_DH)
        kp, vp = kp_ref[:, :, sl], vp_ref[:, :, sl]
        kn_h, vn_h = kn_ref[:, :, sl], vn_ref[:, :, sl]
        heads = [h * SWA_GROUP + g for g in range(SWA_GROUP)]
        q = jnp.concatenate([q_ref[:, :, hq * SWA_DH:(hq + 1) * SWA_DH] for hq in heads], axis=1)
        sink = jnp.zeros((1, rows, 1), F32)
        for g, hq in enumerate(heads):
            sink = jnp.where(row_head == g, sinks_ref[hq], sink)
        sp = bdot('bqd,bkd->bqk', q, kp) * (SWA_DH ** -0.5)
        sn = bdot('bqd,bkd->bqk', q, kn_h) * (SWA_DH ** -0.5)
        sp = jnp.where(valid_prev, sp, NEG)
        sn = jnp.where(valid_new, sn, NEG)
        m = jnp.maximum(jnp.maximum(jnp.max(sp, -1, keepdims=True),
                                    jnp.max(sn, -1, keepdims=True)), sink)
        pp = jnp.exp(sp - m)
        pn = jnp.exp(sn - m)
        den = jnp.sum(pp, -1, keepdims=True) + jnp.sum(pn, -1, keepdims=True) + jnp.exp(sink - m)
        o = bdot('bqk,bkd->bqd', pp / den, vp) + bdot('bqk,bkd->bqd', pn / den, vn_h)
        for g, hq in enumerate(heads):
            o_ref[:, :, hq * SWA_DH:(hq + 1) * SWA_DH] = o[:, g * p_:(g + 1) * p_, :]


def _swa_sample(h3, prev_k, prev_v, sinks, n_new, tb=8):
    b = h3.shape[0]
    w = prev_k.shape[1]
    return pl.pallas_call(
        functools.partial(_swa_sample_kernel, n_new=n_new),
        grid=(b // tb,),
        in_specs=[
            pl.BlockSpec(memory_space=pltpu.SMEM),
            pl.BlockSpec((tb, SAMPLE_PAD, 1024), lambda i: (i, 0, COL_SQ // 1024)),
            pl.BlockSpec((tb, SAMPLE_PAD, 256), lambda i: (i, 0, COL_SK // 256)),
            pl.BlockSpec((tb, SAMPLE_PAD, 256), lambda i: (i, 0, COL_SV // 256)),
            pl.BlockSpec((tb, w, 256), lambda i: (i, 0, 0)),
            pl.BlockSpec((tb, w, 256), lambda i: (i, 0, 0)),
        ],
        out_specs=pl.BlockSpec((tb, SAMPLE_PAD, 1024), lambda i: (i, 0, 0)),
        out_shape=jax.ShapeDtypeStruct((b, SAMPLE_PAD, 1024), F32),
        compiler_params=_params("parallel"),
        name="swa_sample",
    )(sinks, h3, h3, h3, prev_k, prev_v)


def _mem_head(q, k, v):
    s = _bdot_nt(q, k) * (MEM_DH ** -0.5)
    m = jnp.max(s, -1, keepdims=True)
    e = jnp.exp(s - m)
    return _bdot(e / jnp.sum(e, -1, keepdims=True), v)


def _mem_kernel(q_ref, mk_ref, mv_ref, o_ref):
    for h in range(MEM_HEADS):
        sl = slice(h * MEM_DH, (h + 1) * MEM_DH)
        o_ref[0, :, sl] = _mem_head(q_ref[0, :, sl], mk_ref[0, :, sl], mv_ref[0, :, sl])


def _mem_cached_kernel(q_ref, mk_hbm, mv_hbm, o_ref, kv_ref, sem_ref):
    b = pl.program_id(0)
    slot = b % 2

    def copies(seq, s):
        return [pltpu.make_async_copy(src.at[seq, :, h, :], kv_ref.at[s, j, h], sem_ref.at[s, j, h])
                for j, src in enumerate((mk_hbm, mv_hbm)) for h in range(MEM_HEADS)]

    @pl.when(b == 0)
    def _():
        for c in copies(0, 0):
            c.start()

    @pl.when(b + 1 < pl.num_programs(0))
    def _():
        for c in copies(b + 1, 1 - slot):
            c.start()

    for c in copies(b, slot):
        c.wait()
    for h in range(MEM_HEADS):
        sl = slice(h * MEM_DH, (h + 1) * MEM_DH)
        o_ref[0, :, sl] = _mem_head(q_ref[0, :, sl], kv_ref[slot, 0, h], kv_ref[slot, 1, h])


def _mem_attend_cached(h3, mk, mv):
    b, l, _ = h3.shape
    return pl.pallas_call(
        _mem_cached_kernel,
        grid=(b,),
        in_specs=[
            pl.BlockSpec((1, l, 1024), lambda i: (i, 0, COL_MQ // 1024)),
            pl.BlockSpec(memory_space=pl.ANY),
            pl.BlockSpec(memory_space=pl.ANY),
        ],
        out_specs=pl.BlockSpec((1, l, 1024), lambda i: (i, 0, 0)),
        out_shape=jax.ShapeDtypeStruct((b, l, 1024), F32),
        scratch_shapes=[pltpu.VMEM((2, 2, MEM_HEADS, N_MEM, MEM_DH), F32),
                        pltpu.SemaphoreType.DMA((2, 2, MEM_HEADS))],
        compiler_params=_params("arbitrary"),
        name="mem_attend_cached",
    )(h3, mk, mv)


def _mem_attend(h3, mk, mv, tl):
    b, l, _ = h3.shape
    return pl.pallas_call(
        _mem_kernel,
        grid=(b, l // tl),
        in_specs=[
            pl.BlockSpec((1, tl, 1024), lambda i, j: (i, j, COL_MQ // 1024)),
            pl.BlockSpec((1, N_MEM, 1024), lambda i, j: (i, 0, 0)),
            pl.BlockSpec((1, N_MEM, 1024), lambda i, j: (i, 0, 0)),
        ],
        out_specs=pl.BlockSpec((1, tl, 1024), lambda i, j: (i, j, 0)),
        out_shape=jax.ShapeDtypeStruct((b, l, 1024), F32),
        compiler_params=_params("parallel", "parallel"),
        name="mem_attend",
    )(h3, mk, mv)


def _merge_kernel(ro_ref, so_ref, mo_ref, g0_ref, g1_ref, g2_ref, x_ref, wb_ref, wo_ref, g_ref,
                  b_ref, o_ref, *, alpha):
    acc = None
    for n, (br, gl) in enumerate(((ro_ref, g0_ref), (so_ref, g1_ref), (mo_ref, g2_ref))):
        term = jax.nn.sigmoid(gl[...]) * jnp.dot(br[...].astype(BF), wb_ref[n],
                                                 preferred_element_type=F32)
        acc = term if acc is None else acc + term
    a = jnp.dot(acc.astype(BF), wo_ref[...], preferred_element_type=F32)
    _store_tiled_rows(o_ref, (), _layer_norm(alpha * x_ref[...] + a, g_ref[...], b_ref[...]))


def _merge(ro, so, mo, h2, x2, wb_bf, wo_bf, g, b, alpha, tm):
    t = x2.shape[0]
    tile = lambda col: pl.BlockSpec((tm, 1024), lambda i: (i, col))
    gl0 = COL_GL // 1024
    return pl.pallas_call(
        functools.partial(_merge_kernel, alpha=alpha),
        grid=(t // tm,),
        in_specs=[tile(0), tile(0), tile(0), tile(gl0), tile(gl0 + 1), tile(gl0 + 2), tile(0),
                  pl.BlockSpec((N_BRANCH, 1024, 1024), lambda i: (0, 0, 0)),
                  pl.BlockSpec((1024, 1024), lambda i: (0, 0)),
                  pl.BlockSpec((1, 1024), lambda i: (0, 0)),
                  pl.BlockSpec((1, 1024), lambda i: (0, 0))],
        out_specs=pl.BlockSpec((tm * ROW_SUB, 128), lambda i: (i, 0)),
        out_shape=jax.ShapeDtypeStruct((t * ROW_SUB, 128), F32),
        compiler_params=_params("parallel"),
        name="merge_ln1",
    )(ro, so, mo, h2, h2, h2, x2, wb_bf, wo_bf, g, b)


def _first_index_of_max(v, iota, big, axes):
    m = jnp.max(v, axis=axes, keepdims=True)
    idx = jnp.min(jnp.where(v == m, iota, big), axis=axes, keepdims=True)
    return m, idx


def _router_kernel(x_ref, wr_ref, bias_ref, eidx_ref, ew_ref):
    tt = x_ref.shape[0] // ROW_SUB
    x = _load_tiled_rows(x_ref, (), tt).astype(BF)
    logits = lax.dot_general(wr_ref[...], x, (((1,), (1,)), ((), ())),
                             preferred_element_type=F32)
    s = jax.nn.sigmoid(logits).reshape(N_GROUPS, GROUP_SIZE, tt)
    sb = s + bias_ref[...].reshape(N_GROUPS, GROUP_SIZE, 1)
    ninf = -jnp.inf
    r_iota = lax.broadcasted_iota(jnp.int32, sb.shape, 1)
    m1, i1 = _first_index_of_max(sb, r_iota, GROUP_SIZE, 1)
    m2 = jnp.max(jnp.where(r_iota == i1, ninf, sb), axis=1, keepdims=True)
    gsc = (m1 + m2).reshape(N_GROUPS, tt)
    g_iota = lax.broadcasted_iota(jnp.int32, gsc.shape, 0)
    gmask = jnp.zeros(gsc.shape, jnp.bool_)
    for _ in range(TOPK_GROUPS):
        _, gi = _first_index_of_max(gsc, g_iota, N_GROUPS, 0)
        hit = g_iota == gi
        gmask = gmask | hit
        gsc = jnp.where(hit, ninf, gsc)
    cand = jnp.where(gmask.reshape(N_GROUPS, 1, tt), sb, ninf)
    e_iota = lax.broadcasted_iota(jnp.int32, sb.shape, 0) * GROUP_SIZE + r_iota
    idxs, ws = [], []
    for _ in range(TOP_K):
        _, ei = _first_index_of_max(cand, e_iota, N_EXPERTS, (0, 1))
        hit = e_iota == ei
        idxs.append(ei.reshape(1, tt))
        ws.append(jnp.sum(jnp.where(hit, s, 0.0), axis=(0, 1)).reshape(1, tt))
        cand = jnp.where(hit, ninf, cand)
    w = jnp.concatenate(ws, 0)
    eidx_ref[...] = jnp.concatenate(idxs, 0)
    ew_ref[...] = w / jnp.sum(w, 0, keepdims=True) * ROUTED_SCALE


def _router(xt, wr_t_bf, bias_col, tt):
    t = xt.shape[0] // ROW_SUB
    return pl.pallas_call(
        _router_kernel,
        grid=(t // tt,),
        in_specs=[pl.BlockSpec((tt * ROW_SUB, 128), lambda i: (i, 0)),
                  pl.BlockSpec((N_EXPERTS, 1024), lambda i: (0, 0)),
                  pl.BlockSpec((N_EXPERTS, 1), lambda i: (0, 0))],
        out_specs=[pl.BlockSpec((TOP_K, tt), lambda i: (0, i)),
                   pl.BlockSpec((TOP_K, tt), lambda i: (0, i))],
        out_shape=[jax.ShapeDtypeStruct((TOP_K, t), jnp.int32),
                   jax.ShapeDtypeStruct((TOP_K, t), F32)],
        compiler_params=_params("parallel"),
        name="router",
    )(xt, wr_t_bf, bias_col)


IDX_BITS = 18


def _sorted_keys(eidx):
    n_grp = eidx.shape[0]
    flat_e = eidx.reshape(n_grp, -1)
    a = flat_e.shape[1]
    assert a < (1 << IDX_BITS) and n_grp * N_EXPERTS << IDX_BITS < 2 ** 31
    grp_exp = jnp.arange(n_grp, dtype=jnp.int32)[:, None] * N_EXPERTS + flat_e
    keys = grp_exp * (1 << IDX_BITS) + jnp.arange(a, dtype=jnp.int32)[None, :]
    return jnp.sort(keys.reshape(-1)).reshape(n_grp, a) & ((N_EXPERTS << IDX_BITS) - 1)


def _lookup(table, idx):
    hit = idx[..., None] == jnp.arange(table.shape[0], dtype=idx.dtype)
    return jnp.sum(jnp.where(hit, table, 0), axis=-1)


def _dispatch(eidx, ew, skey, t):
    a = t * TOP_K
    nblk = -(-a // MOE_BLOCK) + N_EXPERTS
    assert nblk % STEP_BLOCKS == 0 and STEP_BLOCKS % 2 == 0
    flat_e = eidx.reshape(-1)
    si = skey & ((1 << IDX_BITS) - 1)
    experts = jnp.arange(N_EXPERTS, dtype=jnp.int32)
    counts = jnp.sum((flat_e[None, :] == experts[:, None]).astype(jnp.int32), axis=1)
    grp_start = jnp.cumsum(counts) - counts
    padded = (counts + MOE_BLOCK - 1) // MOE_BLOCK * MOE_BLOCK
    pad_end = jnp.cumsum(padded)
    pad_start = pad_end - padded
    blk_first = jnp.arange(nblk, dtype=jnp.int32) * MOE_BLOCK
    blk_exp = jnp.minimum(jnp.sum((pad_end[None, :] <= blk_first[:, None]).astype(jnp.int32), axis=1),
                          N_EXPERTS - 1)
    off = (jnp.arange(nblk * MOE_BLOCK, dtype=jnp.int32).reshape(nblk, MOE_BLOCK)
           - _lookup(pad_start, blk_exp)[:, None])
    valid = off < _lookup(counts, blk_exp)[:, None]
    src = jnp.clip(_lookup(grp_start, blk_exp)[:, None] + off, 0, a - 1)
    row_si = si[src]
    row_tok = jnp.where(valid, row_si >> 3, t).astype(jnp.int32)
    row_w = jnp.where(valid, ew.reshape(-1)[row_si], 0.0)
    nused = (pad_end[-1] // MOE_BLOCK).astype(jnp.int32).reshape(1)
    gidx, nxt = _group_tables(counts, blk_exp)
    step_rows = STEP_BLOCKS * MOE_BLOCK
    return (row_tok.reshape(-1), row_w.reshape(nblk // STEP_BLOCKS, 1, step_rows), blk_exp, nused,
            gidx, nxt)


def _group_tables(counts, blk_exp):
    nonempty = counts > 0
    gidx = _lookup(jnp.cumsum(nonempty.astype(jnp.int32)) - 1, blk_exp)
    experts = jnp.arange(N_EXPERTS, dtype=jnp.int32)
    cand = jnp.where(nonempty, experts, N_EXPERTS)
    later = lax.cummin(cand, axis=0, reverse=True)
    nxt = jnp.concatenate([later[1:], jnp.full((2,), N_EXPERTS, jnp.int32)])
    nxt1 = _lookup(nxt, blk_exp)
    nxt2 = _lookup(nxt, jnp.minimum(nxt1, N_EXPERTS))
    none = lambda a: jnp.where(a >= N_EXPERTS, -1, a).astype(jnp.int32)
    return gidx.astype(jnp.int32), jnp.concatenate([none(nxt1), none(nxt2)])


WEIGHT_SLOTS = 3


def _weight_copies(hbm_refs, buf_ref, sem_ref, e, slot):
    return [pltpu.make_async_copy(h.at[e], buf_ref.at[slot, k], sem_ref.at[slot, k])
            for k, h in enumerate(hbm_refs)]


def _stage_weights(b, be_ref, nused_ref, gidx_ref, nxt_ref, hbm_refs, buf_ref, sem_ref, cache_refs):
    first = ((b == 0) | (be_ref[b] != be_ref[jnp.maximum(b - 1, 0)])) & (b < nused_ref[0])
    nblk = be_ref.shape[0]

    def start(e, slot):
        for c in _weight_copies(hbm_refs, buf_ref, sem_ref, e, slot):
            c.start()

    @pl.when(first)
    def _():
        slot = gidx_ref[b] % WEIGHT_SLOTS

        @pl.when(b == 0)
        def _():
            start(be_ref[0], 0)

            @pl.when(nxt_ref[0] >= 0)
            def _():
                start(nxt_ref[0], 1)

        for c in _weight_copies(hbm_refs, buf_ref, sem_ref, be_ref[b], slot):
            c.wait()
        nxt2 = nxt_ref[nblk + b]

        @pl.when(nxt2 >= 0)
        def _():
            start(nxt2, (slot + 2) % WEIGHT_SLOTS)

        for k, cache in enumerate(cache_refs):
            cache[...] = buf_ref[slot, k].astype(BF)


def _gather_rows(tok_ref, b, x_ref, xg_ref, slot):
    for r in range(MOE_BLOCK):
        xg_ref[slot, r * ROW_SUB:(r + 1) * ROW_SUB, :] = x_ref[tok_ref[b * MOE_BLOCK + r]]


def _up_kernel(be_ref, nused_ref, gidx_ref, nxt_ref, tok_ref, x_ref, wg_hbm, wu_hbm, act_ref,
               xg_ref, wbuf_ref, wgb_ref, wub_ref, sem_ref):
    i = pl.program_id(0)
    nblk = pl.num_programs(0) * STEP_BLOCKS
    b0 = i * STEP_BLOCKS

    @pl.when(i == 0)
    def _():
        _gather_rows(tok_ref, 0, x_ref, xg_ref, 0)

    @pl.when(b0 < nused_ref[0])
    def _():
        for j in range(STEP_BLOCKS):
            b = b0 + j
            _stage_weights(b, be_ref, nused_ref, gidx_ref, nxt_ref, (wg_hbm, wu_hbm), wbuf_ref,
                           sem_ref, (wgb_ref, wub_ref))
            _gather_rows(tok_ref, jnp.minimum(b + 1, nblk - 1), x_ref, xg_ref, (j + 1) % 2)
            x = _load_tiled_rows(xg_ref, (j % 2,), MOE_BLOCK).astype(BF)
            g = jnp.dot(x, wgb_ref[...], preferred_element_type=F32)
            u = jnp.dot(x, wub_ref[...], preferred_element_type=F32)
            act_ref[j * MOE_BLOCK:(j + 1) * MOE_BLOCK, :] = ((g * jax.nn.sigmoid(g)) * u).astype(BF)

    @pl.when(b0 >= nused_ref[0])
    def _():
        act_ref[...] = jnp.zeros_like(act_ref)


def _experts_up(x, row_tok, blk_exp, nused, gidx, nxt, w_gate, w_up):
    nblk = blk_exp.shape[0]
    step_rows = STEP_BLOCKS * MOE_BLOCK
    return pl.pallas_call(
        _up_kernel,
        grid_spec=pltpu.PrefetchScalarGridSpec(
            num_scalar_prefetch=5,
            grid=(nblk // STEP_BLOCKS,),
            in_specs=[pl.BlockSpec(memory_space=pltpu.VMEM),
                      pl.BlockSpec(memory_space=pl.ANY),
                      pl.BlockSpec(memory_space=pl.ANY)],
            out_specs=pl.BlockSpec((step_rows, D_EXPERT), lambda i, *_: (i, 0)),
            scratch_shapes=[pltpu.VMEM((2, MOE_BLOCK * ROW_SUB, 128), F32),
                            pltpu.VMEM((WEIGHT_SLOTS, 2, D_MODEL, D_EXPERT), F32),
                            pltpu.VMEM((D_MODEL, D_EXPERT), BF),
                            pltpu.VMEM((D_MODEL, D_EXPERT), BF),
                            pltpu.SemaphoreType.DMA((WEIGHT_SLOTS, 2))],
        ),
        out_shape=jax.ShapeDtypeStruct((nblk * MOE_BLOCK, D_EXPERT), BF),
        compiler_params=_params("arbitrary"),
        name="experts_up",
    )(blk_exp, nused, gidx, nxt, row_tok, x, w_gate, w_up)


SCATTER_GROUP = 8


def _row_to_column(row):
    n = row.shape[1]
    eye = lax.broadcasted_iota(jnp.int32, (n, n), 0) == lax.broadcasted_iota(jnp.int32, (n, n), 1)
    return jnp.sum(jnp.where(eye, jnp.broadcast_to(row, (n, n)), 0.0), axis=1, keepdims=True)


def _scatter_add_rows(tok_ref, b, y_ref, yb_ref, slot):
    for r0 in range(0, MOE_BLOCK, SCATTER_GROUP):
        rs = range(r0, r0 + SCATTER_GROUP)
        toks = [tok_ref[b * MOE_BLOCK + r] for r in rs]
        new = [y_ref[t] + yb_ref[slot, r * ROW_SUB:(r + 1) * ROW_SUB, :] for r, t in zip(rs, toks)]
        for t, v in zip(toks, new):
            y_ref[t] = v


def _down_kernel(be_ref, nused_ref, gidx_ref, nxt_ref, tok_ref, act_ref, rw_ref, wd_hbm, y_ref,
                 yb_ref, wbuf_ref, wdb_ref, sem_ref):
    i = pl.program_id(0)
    nblk = pl.num_programs(0) * STEP_BLOCKS
    b0 = i * STEP_BLOCKS

    @pl.when(i == 0)
    def _():
        y_ref[...] = jnp.zeros_like(y_ref)
        yb_ref[1] = jnp.zeros(yb_ref.shape[1:], F32)

    @pl.when(b0 <= nused_ref[0])
    def _():
        for j in range(STEP_BLOCKS):
            b = b0 + j
            rows = slice(j * MOE_BLOCK, (j + 1) * MOE_BLOCK)
            _stage_weights(b, be_ref, nused_ref, gidx_ref, nxt_ref, (wd_hbm,), wbuf_ref, sem_ref,
                           (wdb_ref,))
            yb = jnp.dot(act_ref[rows, :], wdb_ref[...],
                         preferred_element_type=F32) * _row_to_column(rw_ref[0, :, rows])
            _store_tiled_rows(yb_ref, (j % 2,), yb)
            _scatter_add_rows(tok_ref, jnp.maximum(b - 1, 0), y_ref, yb_ref, (j + 1) % 2)

    @pl.when((i == pl.num_programs(0) - 1) & (nused_ref[0] >= nblk))
    def _():
        _scatter_add_rows(tok_ref, nblk - 1, y_ref, yb_ref, (STEP_BLOCKS - 1) % 2)


def _experts_down(act, row_tok, row_w, blk_exp, nused, gidx, nxt, w_down, t_rows):
    nblk = blk_exp.shape[0]
    step_rows = STEP_BLOCKS * MOE_BLOCK
    return pl.pallas_call(
        _down_kernel,
        grid_spec=pltpu.PrefetchScalarGridSpec(
            num_scalar_prefetch=5,
            grid=(nblk // STEP_BLOCKS,),
            in_specs=[pl.BlockSpec((step_rows, D_EXPERT), lambda i, *_: (i, 0)),
                      pl.BlockSpec((1, 1, step_rows), lambda i, *_: (i, 0, 0)),
                      pl.BlockSpec(memory_space=pl.ANY)],
            out_specs=pl.BlockSpec(memory_space=pltpu.VMEM),
            scratch_shapes=[pltpu.VMEM((2, MOE_BLOCK * ROW_SUB, 128), F32),
                            pltpu.VMEM((WEIGHT_SLOTS, 1, D_EXPERT, D_MODEL), F32),
                            pltpu.VMEM((D_EXPERT, D_MODEL), BF),
                            pltpu.SemaphoreType.DMA((WEIGHT_SLOTS, 1))],
        ),
        out_shape=jax.ShapeDtypeStruct((t_rows, ROW_SUB, 128), F32),
        compiler_params=_params("arbitrary"),
        name="experts_down",
    )(blk_exp, nused, gidx, nxt, row_tok, act, row_w, w_down)


def _final_kernel(h_ref, yr_ref, wg_ref, wu_ref, wd_ref, g_ref, b_ref, o_ref, *, alpha):
    h = _load_tiled_rows(h_ref, (), o_ref.shape[0])
    hb = h.astype(BF)
    g = jnp.dot(hb, wg_ref[...], preferred_element_type=F32)
    u = jnp.dot(hb, wu_ref[...], preferred_element_type=F32)
    shared = jnp.dot(((g * jax.nn.sigmoid(g)) * u).astype(BF), wd_ref[...],
                     preferred_element_type=F32)
    f = _load_tiled_rows(yr_ref, (), h.shape[0]) + shared
    o_ref[...] = _layer_norm(alpha * h + f, g_ref[...], b_ref[...])


def _final(ht, yr, wg_bf, wu_bf, wd_bf, g, b, alpha, tm):
    t = ht.shape[0] // ROW_SUB
    tiled = pl.BlockSpec((tm * ROW_SUB, 128), lambda i: (i, 0))
    full = lambda shape: pl.BlockSpec(shape, lambda i: (0, 0))
    return pl.pallas_call(
        functools.partial(_final_kernel, alpha=alpha),
        grid=(t // tm,),
        in_specs=[tiled, tiled, full((1024, 256)), full((1024, 256)), full((256, 1024)),
                  full((1, 1024)), full((1, 1024))],
        out_specs=pl.BlockSpec((tm, 1024), lambda i: (i, 0)),
        out_shape=jax.ShapeDtypeStruct((t, 1024), F32),
        compiler_params=_params("parallel"),
        name="shared_ln2",
    )(ht, yr, wg_bf, wu_bf, wd_bf, g, b)


def _moe_ln2(ht, tables, w_gate, w_up, w_down, wsg_bf, wsu_bf, wsd_bf, g, b, alpha):
    t = ht.shape[0] // ROW_SUB
    row_tok, row_w, blk_exp, nused, gidx, nxt = tables
    act = _experts_up(ht.reshape(t, ROW_SUB, 128), jnp.minimum(row_tok, t - 1), blk_exp, nused, gidx,
                      nxt, w_gate, w_up)
    yr = _experts_down(act, row_tok, row_w, blk_exp, nused, gidx, nxt, w_down, t + 8)
    return _final(ht, yr.reshape((t + 8) * ROW_SUB, 128), wsg_bf, wsu_bf, wsd_bf, g, b, alpha, 256)


def _permute_w_in(w_in):
    rq, rk, rv, rg, sq, sk, sv, mq, gl = jnp.split(
        w_in, [512, 1024, 2048, 3072, 4096, 4352, 4608, 5632], axis=-1)
    return jnp.concatenate([rq, rk, rv, rg, sq, mq, gl, sk, sv], -1)


def kernel(x_prompt, x_sample, mem_prompt, cache_ret_state, cache_swa_k, cache_swa_v, cache_mem_k,
           cache_mem_v, w_in, swa_sinks, w_mem_kv, w_branch, w_o, ln1_g, ln1_b, w_router,
           router_bias, w_gate_e, w_up_e, w_down_e, w_sh_gate, w_sh_up, w_sh_down, ln2_g, ln2_b):
    depth = w_in.shape[0]
    assert depth == 1
    alpha = (2.0 * depth) ** 0.25
    bp, lp, d = x_prompt.shape
    bs, ls, _ = x_sample.shape
    l = 0

    w_in_bf = _permute_w_in(w_in[l]).astype(BF)
    sinks = swa_sinks[l]
    wb_bf = w_branch[l].astype(BF)
    wo_bf = w_o[l].astype(BF)
    g1, b1 = ln1_g[l].reshape(1, d), ln1_b[l].reshape(1, d)
    g2, b2 = ln2_g[l].reshape(1, d), ln2_b[l].reshape(1, d)

    tp = bp * lp
    xp2 = x_prompt.reshape(tp, d)
    hp2 = _proj(xp2, w_in_bf, 1024, PROJ_TN)
    hp3 = hp2.reshape(bp, lp, D_IN)
    mkv = _proj(mem_prompt.reshape(bp * N_MEM, d), w_mem_kv[l].astype(BF), N_MEM, 512)
    mk_p, mv_p = mkv[:, :1024].reshape(bp, N_MEM, 1024), mkv[:, 1024:].reshape(bp, N_MEM, 1024)
    rs0 = jnp.zeros((bp, RET_HEADS, RET_DK, RET_DV), F32)
    ro_p, rs_p = _retention(hp3, jnp.arange(lp), rs0, RET_CHUNK, RET_CHUNK, 1, 4)
    so_p = _swa_prompt(hp2, sinks)
    mo_p = _mem_attend(hp3, mk_p, mv_p, 512)
    hmid_p = _merge(ro_p.reshape(tp, d), so_p, mo_p.reshape(tp, d), hp2, xp2, wb_bf, wo_bf,
                    g1, b1, alpha, 256)

    ts = bs * SAMPLE_PAD
    xs3 = jnp.pad(x_sample, ((0, 0), (0, SAMPLE_PAD - ls), (0, 0)))
    xs2 = xs3.reshape(ts, d)
    hs2 = _proj(xs2, w_in_bf, ts, PROJ_TN)
    hs3 = hs2.reshape(bs, SAMPLE_PAD, D_IN)
    pos_s = PAST_LEN + jnp.arange(SAMPLE_PAD)
    ro_s, rs_s = _retention(hs3, pos_s, cache_ret_state.reshape(bs, RET_HEADS, RET_DK, RET_DV), ls,
                            SAMPLE_PAD, 8, 1)
    w_buf = cache_swa_k.shape[2]
    prev_k = cache_swa_k.reshape(bs, w_buf, SWA_KV_HEADS * SWA_DH)
    prev_v = cache_swa_v.reshape(bs, w_buf, SWA_KV_HEADS * SWA_DH)
    so_s = _swa_sample(hs3, prev_k, prev_v, sinks, ls)
    mo_s = _mem_attend_cached(hs3, cache_mem_k.reshape(bs, N_MEM, MEM_HEADS, MEM_DH),
                              cache_mem_v.reshape(bs, N_MEM, MEM_HEADS, MEM_DH))
    hmid_s = _merge(ro_s.reshape(ts, d), so_s.reshape(ts, d), mo_s.reshape(ts, d), hs2, xs2,
                    wb_bf, wo_bf, g1, b1, alpha, 256)
    hmid_s = hmid_s.reshape(bs, SAMPLE_PAD, ROW_SUB, 128)[:, :ls].reshape(bs * ls * ROW_SUB, 128)

    t_all = tp + bs * ls
    assert t_all % (2 * ROUTER_TILE) == 0 and t_all // 2 <= tp
    th = t_all // 2
    groups = (hmid_p[:th * ROW_SUB], jnp.concatenate([hmid_p[th * ROW_SUB:], hmid_s], 0))
    wr_t_bf, bias_col = w_router[l].T.astype(BF), router_bias[l].reshape(N_EXPERTS, 1)
    routed = [_router(ht, wr_t_bf, bias_col, ROUTER_TILE) for ht in groups]
    eidx = jnp.stack([e.T for e, _ in routed])
    tables = jax.vmap(functools.partial(_dispatch, t=th))(
        eidx, jnp.stack([w.T for _, w in routed]), _sorted_keys(eidx))
    y_a, y_b = (
        _moe_ln2(ht, jax.tree.map(lambda a: a[i], tables), w_gate_e[l], w_up_e[l], w_down_e[l],
                 w_sh_gate[l].astype(BF), w_sh_up[l].astype(BF), w_sh_down[l].astype(BF), g2, b2, alpha)
        for i, ht in enumerate(groups))

    y_p = jnp.concatenate([y_a, y_b[:tp - th]], 0).reshape(bp, lp, d)
    y_s = y_b[tp - th:].reshape(bs, ls, d)
    kv4 = lambda a, n: a.reshape(1, a.shape[0], n, SWA_KV_HEADS, SWA_DH)
    k_p = kv4(hp3[:, lp - WINDOW:, COL_SK:COL_SK + 256], WINDOW)
    v_p = kv4(hp3[:, lp - WINDOW:, COL_SV:COL_SV + 256], WINDOW)
    mem4 = lambda a: a.reshape(1, bp, N_MEM, MEM_HEADS, MEM_DH)
    k_s = kv4(jnp.concatenate([prev_k, hs3[:, :ls, COL_SK:COL_SK + 256]], 1)[:, -w_buf:], w_buf)
    v_s = kv4(jnp.concatenate([prev_v, hs3[:, :ls, COL_SV:COL_SV + 256]], 1)[:, -w_buf:], w_buf)
    return (y_p, y_s, rs_p[None], k_p, v_p, mem4(mk_p), mem4(mv_p), rs_s[None], k_s, v_s)
```
